```python
import math
import jax, jax.numpy as jnp
from jax import lax
import numpy as np

D_MODEL = 1024
BATCH = 16
SEQ = 2048
DEPTH = 2
DEC_BATCH = 16
DEC_SEQ = 32
PAST_LEN = 2048

CHUNK = 64
N_AB_LAYERS = (DEPTH + 1) // 2
N_RET_LAYERS = DEPTH // 2
EPS = 1e-6

GLA_HEADS = 4
GLA_DK = 64
GLA_DV = 128
GLA_QK = GLA_HEADS * GLA_DK
A_WIDTH = GLA_HEADS * GLA_DV
GLA_GATE_RANK = 16
GLA_TAU = 16.0
A_COLS = 2 * GLA_QK + A_WIDTH + GLA_GATE_RANK + A_WIDTH
A_SPLITS = [GLA_QK, 2 * GLA_QK, 2 * GLA_QK + A_WIDTH, 2 * GLA_QK + A_WIDTH + GLA_GATE_RANK]

RWKV_HEADS = 8
RWKV_N = 64
B_WIDTH = RWKV_HEADS * RWKV_N
RWKV_DECAY_LORA = 64
RWKV_A_LORA = 64
RWKV_GATE_LORA = 128
RWKV_GN_EPS = 64e-5
B_COLS = 3 * B_WIDTH + RWKV_DECAY_LORA + RWKV_A_LORA + RWKV_GATE_LORA
B_SPLITS = [B_WIDTH, 2 * B_WIDTH, 3 * B_WIDTH, 3 * B_WIDTH + RWKV_DECAY_LORA,
            3 * B_WIDTH + RWKV_DECAY_LORA + RWKV_A_LORA]

RET_HEADS = 4
RET_DK = D_MODEL // RET_HEADS
RET_DV = 2 * D_MODEL // RET_HEADS
RET_QK = RET_HEADS * RET_DK
RET_V = RET_HEADS * RET_DV
RET_COLS = 2 * RET_QK + 2 * RET_V
RET_SPLITS = [RET_QK, 2 * RET_QK, 2 * RET_QK + RET_V]
RET_GN_EPS = 1e-5
ROPE_BASE = 10000.0

D_FF = 2816
N_EXPERTS = 8
TOP_K = 2

kernel_name = 'hybrid_gla_rwkv7_retention_stream_step'


def rmsnorm(x, g):
    xf = x.astype(jnp.float32)
    y = xf * lax.rsqrt(jnp.mean(xf * xf, axis=-1, keepdims=True) + EPS)
    return (y * g.astype(jnp.float32)).astype(x.dtype)


def head_groupnorm(x, g, b, eps):
    mu = jnp.mean(x, axis=-1, keepdims=True)
    var = jnp.mean(jnp.square(x - mu), axis=-1, keepdims=True)
    y = ((x - mu) * lax.rsqrt(var + eps)).reshape(*x.shape[:-2], -1)
    return y * g.astype(jnp.float32) + b.astype(jnp.float32)


def to_chunks(a, C):
    B, T, H, d = a.shape
    return a.reshape(B, T // C, C, H, d).transpose(1, 0, 3, 2, 4)


def from_chunks(o):
    nc, B, H, C, d = o.shape
    return o.transpose(1, 0, 3, 2, 4).reshape(B, nc * C, H, d)


def gla_chunked(q, k, v, logg, s0):
    T = q.shape[1]
    C = min(CHUNK, T)
    causal = jnp.tril(jnp.ones((C, C), dtype=bool))

    def step(s, inp):
        qc, kc, vc, gc = inp
        b = jnp.cumsum(gc, axis=-2)
        o_inter = jnp.einsum('bhik,bhkv->bhiv', qc * jnp.exp(b), s)
        diff = b[:, :, :, None, :] - b[:, :, None, :, :]
        decay = jnp.exp(jnp.where(causal[:, :, None], diff, -jnp.inf))
        att = jnp.einsum('bhik,bhjk,bhijk->bhij', qc, kc, decay)
        o = o_inter + jnp.einsum('bhij,bhjv->bhiv', att, vc)
        b_last = b[:, :, -1:, :]
        s_new = jnp.exp(b_last[:, :, 0, :])[..., None] * s + jnp.einsum(
            'bhjk,bhjv->bhkv', kc * jnp.exp(b_last - b), vc)
        return s_new, o

    s_fin, o = lax.scan(step, s0, (to_chunks(q, C), to_chunks(k, C), to_chunks(v, C), to_chunks(logg, C)))
    return from_chunks(o), s_fin


def retention_chunked(q, k, v, log_gamma, r0):
    T = q.shape[1]
    C = min(CHUNK, T)
    causal = jnp.tril(jnp.ones((C, C), dtype=bool))
    idx = jnp.arange(C, dtype=jnp.float32)
    b = (idx + 1.0)[None, :] * log_gamma[:, None]
    decay = jnp.exp(jnp.where(causal[None], b[:, :, None] - b[:, None, :], -jnp.inf))
    inter_scale = jnp.exp(b)[None, :, :, None]
    state_scale = jnp.exp(b[:, -1:] - b)[None, :, :, None]
    total = jnp.exp(b[:, -1])[None, :, None, None]

    def step(r, inp):
        qc, kc, vc = inp
        o_inter = jnp.einsum('bhik,bhkv->bhiv', qc, r) * inter_scale
        att = jnp.einsum('bhik,bhjk->bhij', qc, kc) * decay[None]
        o = o_inter + jnp.einsum('bhij,bhjv->bhiv', att, vc)
        r_new = total * r + jnp.einsum('bhjk,bhjv->bhkv', kc * state_scale, vc)
        return r_new, o

    r_fin, o = lax.scan(step, r0, (to_chunks(q, C), to_chunks(k, C), to_chunks(v, C)))
    return from_chunks(o), r_fin


def rwkv7_scan(r, logw, k, v, kk, a, s0):
    def step(s, inp):
        rt, wt, kt, vt, kkt, at = inp
        sa = jnp.einsum('bhij,bhj->bhi', s, -kkt)
        s = (s * jnp.exp(wt)[:, :, None, :] + sa[..., None] * (kkt * at)[:, :, None, :]
             + vt[..., None] * kt[:, :, None, :])
        y = jnp.einsum('bhij,bhj->bhi', s, rt)
        return s, y

    xs = tuple(t.transpose(1, 0, 2, 3) for t in (r, logw, k, v, kk, a))
    s_fin, y = lax.scan(step, s0, xs)
    return y.transpose(1, 0, 2, 3), s_fin


def rotary(x, pos):
    d = x.shape[-1]
    inv = 1.0 / (ROPE_BASE ** (jnp.arange(0, d, 2, dtype=jnp.float32) / d))
    ang = pos.astype(jnp.float32)[:, None] * inv[None, :]
    cos = jnp.cos(ang)[None, :, None, :]
    sin = jnp.sin(ang)[None, :, None, :]
    x1, x2 = x[..., : d // 2], x[..., d // 2:]
    return jnp.concatenate([x1 * cos - x2 * sin, x1 * sin + x2 * cos], axis=-1)


def ab_mixer(h, s_gla, s_rwkv, s_shift, w, i):
    f32 = jnp.float32
    B, T, _ = h.shape
    u = jnp.einsum('btd,dc->btc', h, w['ab_w_in'][i])
    ua, ub = u[..., :A_COLS], u[..., A_COLS:]
    qa, ka, va, ga, oga = jnp.split(ua.astype(f32), A_SPLITS, axis=-1)
    qa = qa.reshape(B, T, GLA_HEADS, GLA_DK) * (GLA_DK ** -0.5)
    ka = ka.reshape(B, T, GLA_HEADS, GLA_DK)
    va = va.reshape(B, T, GLA_HEADS, GLA_DV)
    logg = jax.nn.log_sigmoid(ga @ w['gla_w_gate2'][i].astype(f32) + w['gla_b_gate'][i].astype(f32)) / GLA_TAU
    oa, s_gla_new = gla_chunked(qa, ka, va, logg.reshape(B, T, GLA_HEADS, GLA_DK), s_gla.astype(f32))
    oa = oa * lax.rsqrt(jnp.mean(oa * oa, axis=-1, keepdims=True) + EPS) * w['gla_norm'][i].astype(f32)
    oa = oa.reshape(B, T, A_WIDTH) * jax.nn.silu(oga)
    ub_prev = jnp.concatenate([s_shift.astype(ub.dtype), ub[:, :-1]], axis=1)
    ubs = (ub + w['rwkv_mu'][i] * (ub_prev - ub)).astype(f32)
    rb, kb, vb, wl, al, gl = jnp.split(ubs, B_SPLITS, axis=-1)
    dec = -jax.nn.softplus(-(w['rwkv_w0'][i].astype(f32) + jnp.tanh(wl) @ w['rwkv_w2'][i].astype(f32))) - 0.5
    logw = -jnp.exp(dec)
    a = jax.nn.sigmoid(w['rwkv_a0'][i].astype(f32) + al @ w['rwkv_a2'][i].astype(f32))
    g = jax.nn.sigmoid(gl) @ w['rwkv_g2'][i].astype(f32)
    heads = lambda t: t.reshape(B, T, RWKV_HEADS, RWKV_N)
    kk = heads(kb * w['rwkv_k_k'][i].astype(f32))
    kk = kk / jnp.maximum(jnp.sqrt(jnp.sum(kk * kk, axis=-1, keepdims=True)), 1e-12)
    kb = kb * (1.0 + (a - 1.0) * w['rwkv_k_a'][i].astype(f32))
    rh, kh, vh = heads(rb), heads(kb), heads(vb)
    yb, s_rwkv_new = rwkv7_scan(rh, heads(logw), kh, vh, kk, heads(a), s_rwkv.astype(f32))
    bonus = jnp.sum(rh * kh * w['rwkv_r_k'][i].astype(f32), axis=-1, keepdims=True) * vh
    yb = head_groupnorm(yb, w['rwkv_gn_g'][i], w['rwkv_gn_b'][i], RWKV_GN_EPS) + bonus.reshape(B, T, B_WIDTH)
    yb = yb * g
    y = jnp.concatenate([oa, yb], axis=-1).astype(h.dtype) @ w['ab_w_out'][i]
    return y, s_gla_new.astype(h.dtype), s_rwkv_new.astype(h.dtype), ub[:, -1:]


def ret_mixer(h, pos, s_ret, w, i):
    f32 = jnp.float32
    B, T, _ = h.shape
    u = jnp.einsum('btd,dc->btc', h, w['ret_w_in'][i]).astype(f32)
    q, k, v, g = jnp.split(u, RET_SPLITS, axis=-1)
    q = rotary(q.reshape(B, T, RET_HEADS, RET_DK), pos)
    k = rotary(k.reshape(B, T, RET_HEADS, RET_DK), pos) * (RET_DK ** -0.5)
    v = v.reshape(B, T, RET_HEADS, RET_DV)
    log_gamma = jnp.log1p(-jnp.exp2(-5.0 - jnp.arange(RET_HEADS, dtype=f32)))
    o, s_new = retention_chunked(q, k, v, log_gamma, s_ret.astype(f32))
    o = head_groupnorm(o, w['ret_gn_g'][i], w['ret_gn_b'][i], RET_GN_EPS)
    y = (jax.nn.silu(g) * o).astype(h.dtype) @ w['ret_w_out'][i]
    return y, s_new.astype(h.dtype)


def swiglu(h, w1, w3, w2):
    return (jax.nn.silu(h @ w1) * (h @ w3)) @ w2


def moe_swiglu(h, router, w1, w3, w2):
    probs = jax.nn.softmax((h @ router).astype(jnp.float32), axis=-1)
    top_p, top_i = lax.top_k(probs, TOP_K)
    top_p = top_p / jnp.sum(top_p, axis=-1, keepdims=True)
    gates = jnp.sum(jax.nn.one_hot(top_i, N_EXPERTS, dtype=jnp.float32) * top_p[..., None], axis=-2)
    gates = gates.astype(h.dtype)
    out = jnp.zeros_like(h)
    for e in range(N_EXPERTS):
        out = out + gates[..., e:e + 1] * swiglu(h, w1[e], w3[e], w2[e])
    return out


def trunk(x, pos, st_gla, st_rwkv, st_shift, st_ret, w):
    new_gla, new_rwkv, new_shift, new_ret = [], [], [], []
    for layer in range(DEPTH):
        i = layer // 2
        h = rmsnorm(x, w['norm_mix'][layer])
        if layer % 2 == 0:
            y, sg, sr, ss = ab_mixer(h, st_gla[i], st_rwkv[i], st_shift[i], w, i)
            new_gla.append(sg)
            new_rwkv.append(sr)
            new_shift.append(ss)
            x = x + y
            x = x + swiglu(rmsnorm(x, w['norm_ffn'][layer]), w['ffn_w1'][i], w['ffn_w3'][i], w['ffn_w2'][i])
        else:
            y, st = ret_mixer(h, pos, st_ret[i], w, i)
            new_ret.append(st)
            x = x + y
            x = x + moe_swiglu(rmsnorm(x, w['norm_ffn'][layer]), w['moe_router'][i],
                               w['moe_w1'][i], w['moe_w3'][i], w['moe_w2'][i])
    y = rmsnorm(x, w['norm_final'])
    return y, jnp.stack(new_gla), jnp.stack(new_rwkv), jnp.stack(new_shift), jnp.stack(new_ret)


def setup_inputs(seed: int = 0) -> dict:
    key = jax.random.key(seed)
    ks = iter(jax.random.split(key, 64))
    f32 = jnp.float32
    nrm = lambda shape, scale: jax.random.normal(next(ks), shape, f32) * scale
    uni = lambda shape, lo, hi: jax.random.uniform(next(ks), shape, f32, lo, hi)
    NA, NR = N_AB_LAYERS, N_RET_LAYERS
    return {
        'x_prompt': nrm((BATCH, SEQ, D_MODEL), 1.0),
        'x_sample': nrm((DEC_BATCH, DEC_SEQ, D_MODEL), 1.0),
        'state_gla': nrm((NA, DEC_BATCH, GLA_HEADS, GLA_DK, GLA_DV), 0.3),
        'state_rwkv': nrm((NA, DEC_BATCH, RWKV_HEADS, RWKV_N, RWKV_N), 0.3),
        'state_shift': nrm((NA, DEC_BATCH, 1, B_COLS), 1.0),
        'state_ret': nrm((NR, DEC_BATCH, RET_HEADS, RET_DK, RET_DV), 0.3),
        'norm_mix': 1.0 + nrm((DEPTH, D_MODEL), 0.02),
        'norm_ffn': 1.0 + nrm((DEPTH, D_MODEL), 0.02),
        'norm_final': 1.0 + nrm((D_MODEL,), 0.02),
        'ab_w_in': nrm((NA, D_MODEL, A_COLS + B_COLS), D_MODEL ** -0.5),
        'ab_w_out': nrm((NA, A_WIDTH + B_WIDTH, D_MODEL), (A_WIDTH + B_WIDTH) ** -0.5),
        'gla_w_gate2': nrm((NA, GLA_GATE_RANK, GLA_QK), GLA_GATE_RANK ** -0.5),
        'gla_b_gate': 2.0 + nrm((NA, GLA_QK), 0.5),
        'gla_norm': 1.0 + nrm((NA, GLA_DV), 0.02),
        'rwkv_mu': uni((NA, B_COLS), 0.0, 1.0),
        'rwkv_w0': uni((NA, B_WIDTH), -3.0, 0.5),
        'rwkv_w2': nrm((NA, RWKV_DECAY_LORA, B_WIDTH), 0.1 * RWKV_DECAY_LORA ** -0.5),
        'rwkv_a0': nrm((NA, B_WIDTH), 0.5),
        'rwkv_a2': nrm((NA, RWKV_A_LORA, B_WIDTH), 0.3 * RWKV_A_LORA ** -0.5),
        'rwkv_g2': nrm((NA, RWKV_GATE_LORA, B_WIDTH), RWKV_GATE_LORA ** -0.5),
        'rwkv_k_k': 0.85 + nrm((NA, B_WIDTH), 0.05),
        'rwkv_k_a': 1.0 + nrm((NA, B_WIDTH), 0.05),
        'rwkv_r_k': nrm((NA, RWKV_HEADS, RWKV_N), 0.1),
        'rwkv_gn_g': 1.0 + nrm((NA, B_WIDTH), 0.02),
        'rwkv_gn_b': nrm((NA, B_WIDTH), 0.02),
        'ret_w_in': nrm((NR, D_MODEL, RET_COLS), D_MODEL ** -0.5),
        'ret_gn_g': 1.0 + nrm((NR, RET_V), 0.02),
        'ret_gn_b': nrm((NR, RET_V), 0.02),
        'ret_w_out': nrm((NR, RET_V, D_MODEL), RET_V ** -0.5),
        'ffn_w1': nrm((NA, D_MODEL, D_FF), D_MODEL ** -0.5),
        'ffn_w3': nrm((NA, D_MODEL, D_FF), D_MODEL ** -0.5),
        'ffn_w2': nrm((NA, D_FF, D_MODEL), D_FF ** -0.5),
        'moe_router': nrm((NR, D_MODEL, N_EXPERTS), D_MODEL ** -0.5),
        'moe_w1': nrm((NR, N_EXPERTS, D_MODEL, D_FF), D_MODEL ** -0.5),
        'moe_w3': nrm((NR, N_EXPERTS, D_MODEL, D_FF), D_MODEL ** -0.5),
        'moe_w2': nrm((NR, N_EXPERTS, D_FF, D_MODEL), D_FF ** -0.5),
    }


def reference(x_prompt, x_sample, state_gla, state_rwkv, state_shift, state_ret,
              norm_mix, norm_ffn, norm_final,
              ab_w_in, ab_w_out, gla_w_gate2, gla_b_gate, gla_norm,
              rwkv_mu, rwkv_w0, rwkv_w2, rwkv_a0, rwkv_a2, rwkv_g2, rwkv_k_k, rwkv_k_a, rwkv_r_k,
              rwkv_gn_g, rwkv_gn_b,
              ret_w_in, ret_gn_g, ret_gn_b, ret_w_out,
              ffn_w1, ffn_w3, ffn_w2,
              moe_router, moe_w1, moe_w3, moe_w2):
    w = {
        'norm_mix': norm_mix, 'norm_ffn': norm_ffn, 'norm_final': norm_final,
        'ab_w_in': ab_w_in, 'ab_w_out': ab_w_out, 'gla_w_gate2': gla_w_gate2,
        'gla_b_gate': gla_b_gate, 'gla_norm': gla_norm,
        'rwkv_mu': rwkv_mu, 'rwkv_w0': rwkv_w0, 'rwkv_w2': rwkv_w2, 'rwkv_a0': rwkv_a0,
        'rwkv_a2': rwkv_a2, 'rwkv_g2': rwkv_g2, 'rwkv_k_k': rwkv_k_k, 'rwkv_k_a': rwkv_k_a,
        'rwkv_r_k': rwkv_r_k, 'rwkv_gn_g': rwkv_gn_g, 'rwkv_gn_b': rwkv_gn_b,
        'ret_w_in': ret_w_in, 'ret_gn_g': ret_gn_g, 'ret_gn_b': ret_gn_b, 'ret_w_out': ret_w_out,
        'ffn_w1': ffn_w1, 'ffn_w3': ffn_w3, 'ffn_w2': ffn_w2,
        'moe_router': moe_router, 'moe_w1': moe_w1, 'moe_w3': moe_w3, 'moe_w2': moe_w2,
    }
    bp, tp = x_prompt.shape[0], x_prompt.shape[1]
    dt = x_prompt.dtype
    z_gla = jnp.zeros((N_AB_LAYERS, bp, GLA_HEADS, GLA_DK, GLA_DV), dt)
    z_rwkv = jnp.zeros((N_AB_LAYERS, bp, RWKV_HEADS, RWKV_N, RWKV_N), dt)
    z_shift = jnp.zeros((N_AB_LAYERS, bp, 1, B_COLS), dt)
    z_ret = jnp.zeros((N_RET_LAYERS, bp, RET_HEADS, RET_DK, RET_DV), dt)
    pos_prompt = jnp.arange(tp)
    pos_sample = PAST_LEN + jnp.arange(x_sample.shape[1])
    y_prompt, p_gla, p_rwkv, p_shift, p_ret = trunk(x_prompt, pos_prompt, z_gla, z_rwkv, z_shift, z_ret, w)
    y_sample, s_gla, s_rwkv, s_shift, s_ret = trunk(x_sample, pos_sample, state_gla, state_rwkv,
                                                    state_shift, state_ret, w)
    return (y_prompt, y_sample, p_gla, p_rwkv, p_shift, p_ret, s_gla, s_rwkv, s_shift, s_ret)
```

```python
import functools
import math

import jax
import jax.numpy as jnp
from jax import lax
from jax.experimental import pallas as pl
from jax.experimental.pallas import tpu as pltpu

F32 = jnp.float32
BF16 = jnp.bfloat16

D_MODEL = 1024
EPS = 1e-6
GLA_HEADS, GLA_DK, GLA_DV = 4, 64, 128
GLA_QK = GLA_HEADS * GLA_DK
A_WIDTH = GLA_HEADS * GLA_DV
GLA_GATE_RANK = 16
GLA_TAU = 16.0
RWKV_HEADS, RWKV_N = 8, 64
B_WIDTH = RWKV_HEADS * RWKV_N
RWKV_GN_EPS = 64e-5
B_COLS = 3 * B_WIDTH + 64 + 64 + 128
A_COLS = 2 * GLA_QK + A_WIDTH + GLA_GATE_RANK + A_WIDTH
RET_HEADS = 4
RET_DK = D_MODEL // RET_HEADS
RET_DV = 2 * D_MODEL // RET_HEADS
RET_QK = RET_HEADS * RET_DK
RET_V = RET_HEADS * RET_DV
RET_GN_EPS = 1e-5
ROPE_BASE = 10000.0
D_FF = 2816
N_EXPERTS = 8
CHUNK = 64
SUB = 16
LANES = 128
GROUP = 4
UW = 3456
VMEM_LIMIT = 56 * 1024 * 1024

NN = (((1,), (0,)), ((), ()))
NT = (((1,), (1,)), ((), ()))
TN = (((0,), (0,)), ((), ()))


def _dot(a, b, dims=NN):
    return lax.dot_general(a.astype(BF16), b.astype(BF16), dims, preferred_element_type=F32)


def _split(x):
    hi = x.astype(BF16)
    lo = (x - hi.astype(F32)).astype(BF16)
    return hi, lo


def _dot3(a, b, dims=NN):
    ah, al = _split(a)
    bh, bl = _split(b)
    dg = lambda x, y: lax.dot_general(x, y, dims, preferred_element_type=F32)
    return dg(ah, bh) + dg(ah, bl) + dg(al, bh)


def _dot_exact_lhs(m, x, dims=NN):
    xh, xl = _split(x)
    mb = m.astype(BF16)
    dg = lambda y: lax.dot_general(mb, y, dims, preferred_element_type=F32)
    return dg(xh) + dg(xl)


def _dot_exact_rhs(x, m, dims=NN):
    xh, xl = _split(x)
    mb = m.astype(BF16)
    dg = lambda y: lax.dot_general(y, mb, dims, preferred_element_type=F32)
    return dg(xh) + dg(xl)


def _iota(shape, axis):
    return lax.broadcasted_iota(jnp.int32, shape, axis)


def _softplus(z):
    return jnp.maximum(z, 0.0) + jnp.log1p(jnp.exp(-jnp.abs(z)))


def _rms(x, g):
    return x * lax.rsqrt(jnp.mean(x * x, axis=-1, keepdims=True) + EPS) * g


def _params(*sem):
    return pltpu.CompilerParams(dimension_semantics=sem, vmem_limit_bytes=VMEM_LIMIT)


def _norm_matmul_kernel(x_ref, g_ref, w_ref, o_ref, h_ref):
    @pl.when(pl.program_id(1) == 0)
    def _():
        h_ref[...] = _rms(x_ref[...], g_ref[...]).astype(BF16)

    o_ref[...] = jnp.dot(h_ref[...], w_ref[...], preferred_element_type=F32)


def _norm_matmul(x, g, w, tn):
    n, d = x.shape
    nout = w.shape[1]
    tm = min(n, 1024)
    return pl.pallas_call(
        _norm_matmul_kernel,
        grid=(n // tm, nout // tn),
        in_specs=[
            pl.BlockSpec((tm, d), lambda i, j: (i, 0)),
            pl.BlockSpec((1, d), lambda i, j: (0, 0)),
            pl.BlockSpec((d, tn), lambda i, j: (0, j)),
        ],
        out_specs=pl.BlockSpec((tm, tn), lambda i, j: (i, j)),
        out_shape=jax.ShapeDtypeStruct((n, nout), F32),
        scratch_shapes=[pltpu.VMEM((tm, d), BF16)],
        compiler_params=_params("parallel", "arbitrary"),
        name="norm_matmul",
    )(x, g, w)


def _proj_res_kernel(*refs, n_in):
    a_refs = refs[:n_in]
    w_refs = refs[n_in:2 * n_in]
    res_ref, o_ref = refs[2 * n_in], refs[2 * n_in + 1]
    acc = res_ref[...]
    for a_ref, w_ref in zip(a_refs, w_refs):
        acc = acc + jnp.dot(a_ref[...].astype(BF16), w_ref[...], preferred_element_type=F32)
    o_ref[...] = acc


def _proj_res(a_list, w_list, res):
    n, d = res.shape
    tm = min(n, 512)
    n_in = len(a_list)
    in_specs = [pl.BlockSpec((tm, a.shape[1]), lambda i: (i, 0)) for a in a_list]
    in_specs += [pl.BlockSpec(w.shape, lambda i: (0, 0)) for w in w_list]
    in_specs += [pl.BlockSpec((tm, d), lambda i: (i, 0))]
    return pl.pallas_call(
        functools.partial(_proj_res_kernel, n_in=n_in),
        grid=(n // tm,),
        in_specs=in_specs,
        out_specs=pl.BlockSpec((tm, d), lambda i: (i, 0)),
        out_shape=jax.ShapeDtypeStruct((n, d), F32),
        compiler_params=_params("parallel"),
        name="proj_res",
    )(*a_list, *w_list, res)


def _gla_kernel(q_ref, k_ref, v_ref, ga_ref, og_ref, s0_ref, w2_ref, bg_ref, gn_ref,
                o_ref, sout_ref, s_scr, *, C):
    c = pl.program_id(1)

    @pl.when(c == 0)
    def _():
        s_scr[...] = s0_ref[0]

    x = _dot3(ga_ref[...], w2_ref[...]) + bg_ref[...]
    logg = (jnp.minimum(x, 0.0) - jnp.log1p(jnp.exp(-jnp.abs(x)))) * (1.0 / GLA_TAU)
    tril = _iota((C, C), 0) >= _iota((C, C), 1)
    b_all = _dot_exact_lhs(tril, logg)
    q_all = q_ref[...] * (GLA_DK ** -0.5)
    k_all = k_ref[...]
    v_all = v_ref[...]
    eye = _iota((GLA_DK, GLA_DK), 0) == _iota((GLA_DK, GLA_DK), 1)
    ridx = _iota((SUB, 1), 0)
    outs = []
    for h in range(GLA_HEADS):
        q = q_all[:, h * GLA_DK:(h + 1) * GLA_DK]
        k = k_all[:, h * GLA_DK:(h + 1) * GLA_DK]
        b = b_all[:, h * GLA_DK:(h + 1) * GLA_DK]
        v = v_all[:, h * GLA_DV:(h + 1) * GLA_DV]
        s = s_scr[h]
        o_inter = _dot(q * jnp.exp(b), s)
        blocks = []
        for i0 in range(0, C, SUB):
            qi, bi, ki, vi = q[i0:i0 + SUB], b[i0:i0 + SUB], k[i0:i0 + SUB], v[i0:i0 + SUB]
            oi = o_inter[i0:i0 + SUB]
            if i0 > 0:
                m = b[i0 - 1:i0]
                att = _dot(qi * jnp.exp(bi - m), k[:i0] * jnp.exp(m - b[:i0]), NT)
                oi = oi + _dot(att, v[:i0])
            for j in range(SUB):
                e = jnp.exp(jnp.minimum(bi - bi[j:j + 1], 0.0))
                sj = jnp.sum(qi * e * ki[j:j + 1], axis=-1, keepdims=True)
                oi = oi + jnp.where(ridx >= j, sj, 0.0) * vi[j:j + 1]
            blocks.append(oi)
        o = jnp.concatenate(blocks, axis=0)
        bl = b[C - 1:C]
        upd = _dot(k * jnp.exp(bl - b), v, TN)
        ecol = jnp.sum(jnp.where(eye, jnp.exp(bl), 0.0), axis=1, keepdims=True)
        s_scr[h] = ecol * s + upd
        outs.append(_rms(o, gn_ref[...]))
    og = og_ref[...]
    o_ref[...] = jnp.concatenate(outs, axis=1) * (og * jax.nn.sigmoid(og))
    sout_ref[0] = s_scr[...]


def _gla(u, s0, w2p, bg, gn, B, T, C):
    nc = T // C
    row = lambda b, c: b * nc + c
    return pl.pallas_call(
        functools.partial(_gla_kernel, C=C),
        grid=(B, nc),
        in_specs=[
            pl.BlockSpec((C, GLA_QK), lambda b, c: (row(b, c), 0)),
            pl.BlockSpec((C, GLA_QK), lambda b, c: (row(b, c), 1)),
            pl.BlockSpec((C, A_WIDTH), lambda b, c: (row(b, c), 1)),
            pl.BlockSpec((C, LANES), lambda b, c: (row(b, c), 24)),
            pl.BlockSpec((C, A_WIDTH), lambda b, c: (row(b, c), 2)),
            pl.BlockSpec((1, GLA_HEADS, GLA_DK, GLA_DV), lambda b, c: (b, 0, 0, 0)),
            pl.BlockSpec((LANES, GLA_QK), lambda b, c: (0, 0)),
            pl.BlockSpec((1, GLA_QK), lambda b, c: (0, 0)),
            pl.BlockSpec((1, GLA_DV), lambda b, c: (0, 0)),
        ],
        out_specs=[
            pl.BlockSpec((C, A_WIDTH), lambda b, c: (row(b, c), 0)),
            pl.BlockSpec((1, GLA_HEADS, GLA_DK, GLA_DV), lambda b, c: (b, 0, 0, 0)),
        ],
        out_shape=[
            jax.ShapeDtypeStruct((B * T, A_WIDTH), F32),
            jax.ShapeDtypeStruct((B, GLA_HEADS, GLA_DK, GLA_DV), F32),
        ],
        scratch_shapes=[pltpu.VMEM((GLA_HEADS, GLA_DK, GLA_DV), F32)],
        compiler_params=_params("parallel", "arbitrary"),
        name="gla",
    )(u, u, u, u, u, s0, w2p, bg, gn)


def _rwkv_kernel(r_ref, k_ref, v_ref, wa_ref, gl_ref, sh_r, sh_k, sh_v, sh_wa, sh_gl, h0_ref,
                 mu_r, mu_k, mu_v, mu_wa, mu_gl, w0_ref, w2_ref, a0_ref, a2_ref, g2_ref,
                 kk_ref, ka_ref, rk_ref, gng_ref, gnb_ref,
                 y_ref, hout_ref,
                 pr, pk, pv, pwa, pgl, h_scr, u_scr, *, C):
    c = pl.program_id(1)
    GW = GROUP * RWKV_N
    CH = GROUP * C
    logc = C.bit_length() - 1
    n_groups = RWKV_HEADS // GROUP

    def bd_rows(x, col_shift):
        t = jnp.concatenate([x] * GROUP, axis=0)
        w = x.shape[1]
        keep = (_iota((CH, w), 0) >> logc) == ((_iota((CH, w), 1) & (GW - 1)) >> col_shift)
        return jnp.where(keep, t, 0.0)

    @pl.when(c == 0)
    def _():
        pr[...] = sh_r[0]
        pk[...] = sh_k[0]
        pv[...] = sh_v[0]
        pwa[...] = sh_wa[0]
        pgl[...] = sh_gl[0]
        h0 = h0_ref[0]
        for g in range(n_groups):
            t = jnp.concatenate([h0[:, g * GW:(g + 1) * GW]] * GROUP, axis=0)
            keep = (_iota((GW, GW), 0) >> 6) == (_iota((GW, GW), 1) >> 6)
            h_scr[g] = jnp.where(keep, t, 0.0)

    rid = _iota((C, 1), 0)

    def shift(x_ref, prev, mu):
        x = x_ref[...]
        xp = jnp.where(rid == 0, prev[...], pltpu.roll(x, 1, axis=0))
        prev[...] = x[C - 1:C]
        return x + mu[...] * (xp - x)

    rs = shift(r_ref, pr, mu_r)
    ks = shift(k_ref, pk, mu_k)
    vs = shift(v_ref, pv, mu_v)
    was = shift(wa_ref, pwa, mu_wa)
    gls = shift(gl_ref, pgl, mu_gl)

    dec = -_softplus(-(w0_ref[...] + _dot3(jnp.tanh(was), w2_ref[...]))) - 0.5
    logw = -jnp.exp(dec)
    a = jax.nn.sigmoid(a0_ref[...] + _dot3(was, a2_ref[...]))
    gate = _dot3(jax.nn.sigmoid(gls), g2_ref[...])
    head_ones = (_iota((B_WIDTH, B_WIDTH), 0) >> 6) == (_iota((B_WIDTH, B_WIDTH), 1) >> 6)
    kkr = ks * kk_ref[...]
    kk = kkr / jnp.maximum(jnp.sqrt(_dot_exact_rhs(kkr * kkr, head_ones)), 1e-12)
    km = ks * (1.0 + (a - 1.0) * ka_ref[...])
    beta = kk * a
    tril = _iota((C, C), 0) >= _iota((C, C), 1)
    gc = _dot_exact_lhs(tril, logw)
    g_end = gc[C - 1:C]
    kap = kk * jnp.exp(gc - logw)
    rt = rs * jnp.exp(gc)
    e_neg = jnp.exp(-gc)
    b_inv = beta * e_neg
    k_inv = km * e_neg
    e_end = jnp.exp(g_end - gc)
    k_hat = km * e_end
    b_hat = beta * e_end

    col_s = _iota((C, CH), 1) & (C - 1)
    row_t = _iota((C, CH), 0)
    strict = row_t > col_s
    incl = row_t >= col_s
    same_sub = (row_t >> 4) == (col_s >> 4)
    eye_cat = (row_t == col_s).astype(F32)
    ones_c = jnp.ones((C, LANES), F32)

    def bd_sq(x):
        t = jnp.concatenate([x] * GROUP, axis=0)
        keep = (_iota((CH, CH), 0) >> logc) == (_iota((CH, CH), 1) >> logc)
        return jnp.where(keep, t, 0.0)

    ys = []
    for g in range(n_groups):
        L = slice(g * GW, (g + 1) * GW)
        p = jnp.concatenate([kap[:, L], rt[:, L]], axis=0)
        zb = _dot3(p, bd_rows(b_inv[:, L], 6), NT)
        zk = _dot3(p, bd_rows(k_inv[:, L], 6), NT)
        a_b = jnp.where(strict, zb[:C], 0.0)
        b_b = jnp.where(incl, zb[C:], 0.0)
        a_k = jnp.where(strict, zk[:C], 0.0)
        b_k = jnp.where(incl, zk[C:], 0.0)
        bd_v = bd_rows(vs[:, L], 6)
        rhs = jnp.concatenate([kap[:, L], _dot3(a_k, bd_v)], axis=1)

        x1 = -jnp.where(same_sub, a_b, 0.0)
        p1 = eye_cat + x1
        x2 = _dot3(x1, bd_sq(x1))
        r2 = _dot3(jnp.concatenate([x2, p1], axis=0), bd_sq(x2))
        x4, p2 = r2[:C], p1 + r2[C:]
        r4 = _dot3(jnp.concatenate([x4, p2], axis=0), bd_sq(x4))
        x8, p3 = r4[:C], p2 + r4[C:]
        t_d = p3 + _dot3(p3, bd_sq(x8))

        u_scr[...] = jnp.zeros_like(u_scr)
        sol = []
        for i0 in range(0, C, SUB):
            acc = rhs[i0:i0 + SUB]
            if i0 > 0:
                acc = acc - _dot3(a_b[i0:i0 + SUB], u_scr[...])

            def place(val):
                for hh in range(GROUP):
                    rows = slice(hh * C + i0, hh * C + i0 + SUB)
                    for half in range(2):
                        cols = slice(half * GW + hh * RWKV_N, half * GW + (hh + 1) * RWKV_N)
                        u_scr[rows, cols] = val[:, cols]

            place(acc)
            ui = _dot3(t_d[i0:i0 + SUB], u_scr[...])
            place(ui)
            sol.append(ui)
        sol = jnp.concatenate(sol, axis=0)
        w_mat, u0 = sol[:, :GW], sol[:, GW:]

        hbd = h_scr[g]
        u = _dot3(w_mat, hbd) + u0
        y = _dot3(rt[:, L], hbd) + _dot3(b_k, bd_v) - _dot3(b_b, bd_rows(u, 6))
        m = _dot3(k_hat[:, L], vs[:, L], TN) - _dot3(b_hat[:, L], u, TN)
        g_col = _dot_exact_rhs(logw[:, L], ones_c, TN)[:, :1]
        keep = (_iota((GW, GW), 0) >> 6) == (_iota((GW, GW), 1) >> 6)
        h_new = jnp.exp(g_col) * hbd + jnp.where(keep, m, 0.0)
        h_scr[g] = h_new
        hout_ref[0, :, L] = (h_new[0:RWKV_N] + h_new[RWKV_N:2 * RWKV_N]
                             + h_new[2 * RWKV_N:3 * RWKV_N] + h_new[3 * RWKV_N:4 * RWKV_N])
        ys.append(y)
    y = jnp.concatenate(ys, axis=1)

    inv_n = 1.0 / RWKV_N
    mu = _dot_exact_rhs(y, head_ones) * inv_n
    d = y - mu
    var = _dot_exact_rhs(d * d, head_ones) * inv_n
    yn = d * lax.rsqrt(var + RWKV_GN_EPS) * gng_ref[...] + gnb_ref[...]
    bonus = _dot_exact_rhs(rs * km * rk_ref[...], head_ones) * vs
    y_ref[...] = (yn + bonus) * gate


def _rwkv(u, shift0, h0, wts, B, T, C):
    nc = T // C
    row = lambda b, c: b * nc + c
    wide = lambda blk: pl.BlockSpec((C, B_WIDTH), lambda b, c: (row(b, c), blk))
    narrow = lambda blk: pl.BlockSpec((C, LANES), lambda b, c: (row(b, c), blk))
    sh = lambda w, blk: pl.BlockSpec((1, 1, w), lambda b, c: (b, 0, blk))
    const = lambda shape: pl.BlockSpec(shape, lambda b, c: (0,) * len(shape))
    (mu_r, mu_k, mu_v, mu_wa, mu_gl, w0, w2p, a0, a2p, g2, k_k, k_a, r_k, gn_g, gn_b) = wts
    in_specs = [wide(3), wide(4), wide(5), narrow(25), narrow(26),
                sh(B_WIDTH, 0), sh(B_WIDTH, 1), sh(B_WIDTH, 2), sh(LANES, 12), sh(LANES, 13),
                pl.BlockSpec((1, RWKV_N, B_WIDTH), lambda b, c: (b, 0, 0))]
    in_specs += [const(w.shape) for w in wts]
    GW = GROUP * RWKV_N
    return pl.pallas_call(
        functools.partial(_rwkv_kernel, C=C),
        grid=(B, nc),
        in_specs=in_specs,
        out_specs=[
            pl.BlockSpec((C, B_WIDTH), lambda b, c: (row(b, c), 0)),
            pl.BlockSpec((1, RWKV_N, B_WIDTH), lambda b, c: (b, 0, 0)),
        ],
        out_shape=[
            jax.ShapeDtypeStruct((B * T, B_WIDTH), F32),
            jax.ShapeDtypeStruct((B, RWKV_N, B_WIDTH), F32),
        ],
        scratch_shapes=[
            pltpu.VMEM((1, B_WIDTH), F32), pltpu.VMEM((1, B_WIDTH), F32), pltpu.VMEM((1, B_WIDTH), F32),
            pltpu.VMEM((1, LANES), F32), pltpu.VMEM((1, LANES), F32),
            pltpu.VMEM((RWKV_HEADS // GROUP, GW, GW), F32),
            pltpu.VMEM((GROUP * C, 2 * GW), F32),
        ],
        compiler_params=_params("parallel", "arbitrary"),
        name="rwkv7",
    )(u, u, u, u, u, shift0, shift0, shift0, shift0, shift0, h0, *wts)


def _ret_kernel(q_ref, k_ref, v_ref, g_ref, cos_ref, sin_ref, r0_ref, gng_ref, gnb_ref,
                o_ref, rout_ref, r_scr, *, C):
    c = pl.program_id(1)

    @pl.when(c == 0)
    def _():
        r_scr[...] = r0_ref[0]

    cos = cos_ref[...]
    sin = sin_ref[...]
    half = RET_DK // 2

    def rope(x):
        x1, x2 = x[:, :half], x[:, half:]
        return jnp.concatenate([x1 * cos - x2 * sin, x1 * sin + x2 * cos], axis=1)

    ri = _iota((C, C), 0)
    ci = _iota((C, C), 1)
    causal = ri >= ci
    diff = (ri - ci).astype(F32)
    pos1 = (_iota((C, 1), 0) + 1).astype(F32)
    outs = []
    for h in range(RET_HEADS):
        lg = math.log1p(-(2.0 ** (-5.0 - h)))
        q = rope(q_ref[:, h * RET_DK:(h + 1) * RET_DK])
        k = rope(k_ref[:, h * RET_DK:(h + 1) * RET_DK]) * (RET_DK ** -0.5)
        v = v_ref[:, h * RET_DV:(h + 1) * RET_DV]
        r = r_scr[h]
        decay = jnp.where(causal, jnp.exp(diff * lg), 0.0)
        att = _dot(q, k, NT) * decay
        o = _dot(att, v) + _dot(q, r) * jnp.exp(pos1 * lg)
        r_scr[h] = math.exp(C * lg) * r + _dot(k * jnp.exp((C - pos1) * lg), v, TN)
        mu = jnp.mean(o, axis=-1, keepdims=True)
        d = o - mu
        var = jnp.mean(d * d, axis=-1, keepdims=True)
        outs.append(d * lax.rsqrt(var + RET_GN_EPS))
    g = g_ref[...]
    o = jnp.concatenate(outs, axis=1) * gng_ref[...] + gnb_ref[...]
    o_ref[...] = (g * jax.nn.sigmoid(g)) * o
    rout_ref[0] = r_scr[...]


def _ret(u, cos, sin, r0, gn_g, gn_b, B, T, C):
    nc = T // C
    row = lambda b, c: b * nc + c
    return pl.pallas_call(
        functools.partial(_ret_kernel, C=C),
        grid=(B, nc),
        in_specs=[
            pl.BlockSpec((C, RET_QK), lambda b, c: (row(b, c), 0)),
            pl.BlockSpec((C, RET_QK), lambda b, c: (row(b, c), 1)),
            pl.BlockSpec((C, RET_V), lambda b, c: (row(b, c), 1)),
            pl.BlockSpec((C, RET_V), lambda b, c: (row(b, c), 2)),
            pl.BlockSpec((C, RET_DK // 2), lambda b, c: (c, 0)),
            pl.BlockSpec((C, RET_DK // 2), lambda b, c: (c, 0)),
            pl.BlockSpec((1, RET_HEADS, RET_DK, RET_DV), lambda b, c: (b, 0, 0, 0)),
            pl.BlockSpec((1, RET_V), lambda b, c: (0, 0)),
            pl.BlockSpec((1, RET_V), lambda b, c: (0, 0)),
        ],
        out_specs=[
            pl.BlockSpec((C, RET_V), lambda b, c: (row(b, c), 0)),
            pl.BlockSpec((1, RET_HEADS, RET_DK, RET_DV), lambda b, c: (b, 0, 0, 0)),
        ],
        out_shape=[
            jax.ShapeDtypeStruct((B * T, RET_V), F32),
            jax.ShapeDtypeStruct((B, RET_HEADS, RET_DK, RET_DV), F32),
        ],
        scratch_shapes=[pltpu.VMEM((RET_HEADS, RET_DK, RET_DV), F32)],
        compiler_params=_params("parallel", "arbitrary"),
        name="retention",
    )(u, u, u, u, cos, sin, r0, gn_g, gn_b)


def _router_kernel(x_ref, g_ref, wr_ref, o_ref):
    h = _rms(x_ref[...], g_ref[...])
    logits = _dot3(h, wr_ref[...])
    lane = _iota(logits.shape, 1)
    valid = lane < N_EXPERTS
    logits = jnp.where(valid, logits, -jnp.inf)
    m = jnp.max(logits, axis=-1, keepdims=True)
    e = jnp.where(valid, jnp.exp(logits - m), 0.0)
    p = e / jnp.sum(e, axis=-1, keepdims=True)
    big = jnp.int32(LANES)
    p1 = jnp.max(p, axis=-1, keepdims=True)
    i1 = jnp.min(jnp.where(p == p1, lane, big), axis=-1, keepdims=True)
    rest = jnp.where((lane == i1) | ~valid, -1.0, p)
    p2 = jnp.max(rest, axis=-1, keepdims=True)
    i2 = jnp.min(jnp.where(rest == p2, lane, big), axis=-1, keepdims=True)
    tot = p1 + p2
    o_ref[...] = jnp.where(lane == i1, p1 / tot, 0.0) + jnp.where(lane == i2, p2 / tot, 0.0)


def _router(x, g, wr):
    n, d = x.shape
    tm = min(n, 512)
    return pl.pallas_call(
        _router_kernel,
        grid=(n // tm,),
        in_specs=[
            pl.BlockSpec((tm, d), lambda i: (i, 0)),
            pl.BlockSpec((1, d), lambda i: (0, 0)),
            pl.BlockSpec((d, LANES), lambda i: (0, 0)),
        ],
        out_specs=pl.BlockSpec((tm, LANES), lambda i: (i, 0)),
        out_shape=jax.ShapeDtypeStruct((n, LANES), F32),
        compiler_params=_params("parallel"),
        name="router",
    )(x, g, wr)


def _ffn_kernel(*refs, gated, final_norm):
    if gated:
        x_ref, g_ref, gates_ref, w1_ref, w3_ref, w2_ref, gf_ref, o_ref, h_scr, acc = refs
    else:
        x_ref, g_ref, w1_ref, w3_ref, w2_ref, gf_ref, o_ref, h_scr, acc = refs
    e = pl.program_id(1)
    f = pl.program_id(2)

    @pl.when((e == 0) & (f == 0))
    def _():
        h_scr[...] = _rms(x_ref[...], g_ref[...]).astype(BF16)
        acc[...] = jnp.zeros_like(acc)

    h = h_scr[...]
    a1 = jnp.dot(h, w1_ref[0], preferred_element_type=F32)
    a3 = jnp.dot(h, w3_ref[0], preferred_element_type=F32)
    act = a1 * jax.nn.sigmoid(a1) * a3
    if gated:
        gates = gates_ref[...]
        ge = jnp.sum(jnp.where(_iota(gates.shape, 1) == e, gates, 0.0), axis=-1, keepdims=True)
        act = act * ge
    acc[...] += jnp.dot(act.astype(BF16), w2_ref[0], preferred_element_type=F32)

    @pl.when((e == pl.num_programs(1) - 1) & (f == pl.num_programs(2) - 1))
    def _():
        y = x_ref[...] + acc[...]
        o_ref[...] = _rms(y, gf_ref[...]) if final_norm else y


def _ffn(x, g, w1, w3, w2, gates=None, g_final=None):
    n, d = x.shape
    ne, _, dff = w1.shape
    tm = min(n, 1024)
    tf = 256
    gated = gates is not None
    final_norm = g_final is not None
    if g_final is None:
        g_final = g
    in_specs = [pl.BlockSpec((tm, d), lambda i, e, f: (i, 0)),
                pl.BlockSpec((1, d), lambda i, e, f: (0, 0))]
    args = [x, g]
    if gated:
        in_specs.append(pl.BlockSpec((tm, LANES), lambda i, e, f: (i, 0)))
        args.append(gates)
    in_specs += [pl.BlockSpec((1, d, tf), lambda i, e, f: (e, 0, f)),
                 pl.BlockSpec((1, d, tf), lambda i, e, f: (e, 0, f)),
                 pl.BlockSpec((1, tf, d), lambda i, e, f: (e, f, 0)),
                 pl.BlockSpec((1, d), lambda i, e, f: (0, 0))]
    args += [w1, w3, w2, g_final]
    return pl.pallas_call(
        functools.partial(_ffn_kernel, gated=gated, final_norm=final_norm),
        grid=(n // tm, ne, dff // tf),
        in_specs=in_specs,
        out_specs=pl.BlockSpec((tm, d), lambda i, e, f: (i, 0)),
        out_shape=jax.ShapeDtypeStruct((n, d), F32),
        scratch_shapes=[pltpu.VMEM((tm, d), BF16), pltpu.VMEM((tm, d), F32)],
        compiler_params=_params("parallel", "arbitrary", "arbitrary"),
        name="ffn",
    )(*args)


def _prep_weights(p):
    w_in = p["ab_w_in"][0]
    wa, wb = w_in[:, :A_COLS], w_in[:, A_COLS:]
    pad = jnp.zeros((D_MODEL, LANES - GLA_GATE_RANK), F32)
    w_in_p = jnp.concatenate(
        [wa[:, :2 * GLA_QK + A_WIDTH], wa[:, 2 * GLA_QK + A_WIDTH + GLA_GATE_RANK:], wb[:, :3 * B_WIDTH],
         wa[:, 2 * GLA_QK + A_WIDTH:2 * GLA_QK + A_WIDTH + GLA_GATE_RANK], pad, wb[:, 3 * B_WIDTH:]], axis=1)
    row = lambda v: v.reshape(1, -1).astype(F32)
    mu = p["rwkv_mu"][0]
    z64 = jnp.zeros((64, B_WIDTH), F32)
    w = dict(
        norm_mix=p["norm_mix"], norm_ffn=p["norm_ffn"], norm_final=row(p["norm_final"]),
        w_in=w_in_p.astype(BF16),
        w_out_a=p["ab_w_out"][0][:A_WIDTH].astype(BF16), w_out_b=p["ab_w_out"][0][A_WIDTH:].astype(BF16),
        gla_w2=jnp.concatenate([p["gla_w_gate2"][0], jnp.zeros((LANES - GLA_GATE_RANK, GLA_QK), F32)], axis=0),
        gla_bg=row(p["gla_b_gate"][0]), gla_norm=row(p["gla_norm"][0]),
        rwkv=(row(mu[:B_WIDTH]), row(mu[B_WIDTH:2 * B_WIDTH]), row(mu[2 * B_WIDTH:3 * B_WIDTH]),
              row(mu[3 * B_WIDTH:3 * B_WIDTH + LANES]), row(mu[3 * B_WIDTH + LANES:]),
              row(p["rwkv_w0"][0]), jnp.concatenate([p["rwkv_w2"][0], z64], axis=0),
              row(p["rwkv_a0"][0]), jnp.concatenate([z64, p["rwkv_a2"][0]], axis=0),
              p["rwkv_g2"][0], row(p["rwkv_k_k"][0]), row(p["rwkv_k_a"][0]), row(p["rwkv_r_k"][0]),
              row(p["rwkv_gn_g"][0]), row(p["rwkv_gn_b"][0])),
        ret_w_in=p["ret_w_in"][0].astype(BF16), ret_w_out=p["ret_w_out"][0].astype(BF16),
        ret_gn_g=row(p["ret_gn_g"][0]), ret_gn_b=row(p["ret_gn_b"][0]),
        ffn_w1=p["ffn_w1"].astype(BF16), ffn_w3=p["ffn_w3"].astype(BF16), ffn_w2=p["ffn_w2"].astype(BF16),
        router=jnp.concatenate([p["moe_router"][0], jnp.zeros((D_MODEL, LANES - N_EXPERTS), F32)], axis=1),
        moe_w1=p["moe_w1"][0].astype(BF16), moe_w3=p["moe_w3"][0].astype(BF16), moe_w2=p["moe_w2"][0].astype(BF16),
    )
    return w


def _trunk(x, pos0, st_gla, st_rwkv, st_shift, st_ret, w):
    B, T, D = x.shape
    n = B * T
    x0 = x.reshape(n, D)
    c_ab = min(CHUNK, T)
    c_ret = min(256, T)

    u = _norm_matmul(x0, w["norm_mix"][0:1], w["w_in"], tn=UW // 3)
    oa, s_gla = _gla(u, st_gla, w["gla_w2"], w["gla_bg"], w["gla_norm"], B, T, c_ab)
    h0 = st_rwkv.transpose(0, 3, 1, 2).reshape(B, RWKV_N, B_WIDTH)
    yb, h_fin = _rwkv(u, st_shift, h0, w["rwkv"], B, T, c_ab)
    s_rwkv = h_fin.reshape(B, RWKV_N, RWKV_HEADS, RWKV_N).transpose(0, 2, 3, 1)
    u_last = u.reshape(B, T, UW)[:, T - 1:, :]
    s_shift = jnp.concatenate([u_last[..., 3 * A_WIDTH:6 * A_WIDTH], u_last[..., UW - 2 * LANES:]], axis=-1)
    x1 = _proj_res([oa, yb], [w["w_out_a"], w["w_out_b"]], x0)
    x2 = _ffn(x1, w["norm_ffn"][0:1], w["ffn_w1"], w["ffn_w3"], w["ffn_w2"])

    ur = _norm_matmul(x2, w["norm_mix"][1:2], w["ret_w_in"], tn=1536)
    half = RET_DK // 2
    inv = 1.0 / (ROPE_BASE ** (jnp.arange(0, RET_DK, 2, dtype=F32) / RET_DK))
    ang = (pos0 + jnp.arange(T)).astype(F32)[:, None] * inv[None, :]
    o, s_ret = _ret(ur, jnp.cos(ang), jnp.sin(ang), st_ret, w["ret_gn_g"], w["ret_gn_b"], B, T, c_ret)
    x3 = _proj_res([o], [w["ret_w_out"]], x2)
    gates = _router(x3, w["norm_ffn"][1:2], w["router"])
    y = _ffn(x3, w["norm_ffn"][1:2], w["moe_w1"], w["moe_w3"], w["moe_w2"], gates=gates, g_final=w["norm_final"])
    return y.reshape(B, T, D), s_gla[None], s_rwkv[None], s_shift[None], s_ret[None]


def kernel(x_prompt, x_sample, state_gla, state_rwkv, state_shift, state_ret, norm_mix, norm_ffn, norm_final, ab_w_in, ab_w_out, gla_w_gate2, gla_b_gate, gla_norm, rwkv_mu, rwkv_w0, rwkv_w2, rwkv_a0, rwkv_a2, rwkv_g2, rwkv_k_k, rwkv_k_a, rwkv_r_k, rwkv_gn_g, rwkv_gn_b, ret_w_in, ret_gn_g, ret_gn_b, ret_w_out, ffn_w1, ffn_w3, ffn_w2, moe_router, moe_w1, moe_w3, moe_w2):
    p = dict(norm_mix=norm_mix, norm_ffn=norm_ffn, norm_final=norm_final, ab_w_in=ab_w_in, ab_w_out=ab_w_out,
             gla_w_gate2=gla_w_gate2, gla_b_gate=gla_b_gate, gla_norm=gla_norm, rwkv_mu=rwkv_mu, rwkv_w0=rwkv_w0,
             rwkv_w2=rwkv_w2, rwkv_a0=rwkv_a0, rwkv_a2=rwkv_a2, rwkv_g2=rwkv_g2, rwkv_k_k=rwkv_k_k,
             rwkv_k_a=rwkv_k_a, rwkv_r_k=rwkv_r_k, rwkv_gn_g=rwkv_gn_g, rwkv_gn_b=rwkv_gn_b, ret_w_in=ret_w_in,
             ret_gn_g=ret_gn_g, ret_gn_b=ret_gn_b, ret_w_out=ret_w_out, ffn_w1=ffn_w1, ffn_w3=ffn_w3, ffn_w2=ffn_w2,
             moe_router=moe_router, moe_w1=moe_w1, moe_w3=moe_w3, moe_w2=moe_w2)
    w = _prep_weights(p)
    bp, tp = x_prompt.shape[0], x_prompt.shape[1]
    dt = x_prompt.dtype
    z_gla = jnp.zeros((bp, GLA_HEADS, GLA_DK, GLA_DV), dt)
    z_rwkv = jnp.zeros((bp, RWKV_HEADS, RWKV_N, RWKV_N), dt)
    z_shift = jnp.zeros((bp, 1, B_COLS), dt)
    z_ret = jnp.zeros((bp, RET_HEADS, RET_DK, RET_DV), dt)
    past = 2048
    yp = _trunk(x_prompt, 0, z_gla, z_rwkv, z_shift, z_ret, w)
    ys = _trunk(x_sample, past, state_gla[0], state_rwkv[0], state_shift[0], state_ret[0], w)
    return (yp[0], ys[0], yp[1], yp[2], yp[3], yp[4], ys[1], ys[2], ys[3], ys[4])
```

```python
import functools
import math

import jax
import jax.numpy as jnp
from jax import lax
from jax.experimental import pallas as pl
from jax.experimental.pallas import tpu as pltpu

F32 = jnp.float32
BF16 = jnp.bfloat16

D_MODEL = 1024
EPS = 1e-6
GLA_HEADS, GLA_DK, GLA_DV = 4, 64, 128
GLA_QK = GLA_HEADS * GLA_DK
A_WIDTH = GLA_HEADS * GLA_DV
GLA_GATE_RANK = 16
GLA_TAU = 16.0
RWKV_HEADS, RWKV_N = 8, 64
B_WIDTH = RWKV_HEADS * RWKV_N
RWKV_GN_EPS = 64e-5
B_COLS = 3 * B_WIDTH + 64 + 64 + 128
A_COLS = 2 * GLA_QK + A_WIDTH + GLA_GATE_RANK + A_WIDTH
RET_HEADS = 4
RET_DK = D_MODEL // RET_HEADS
RET_DV = 2 * D_MODEL // RET_HEADS
RET_QK = RET_HEADS * RET_DK
RET_V = RET_HEADS * RET_DV
RET_GN_EPS = 1e-5
ROPE_BASE = 10000.0
D_FF = 2816
N_EXPERTS = 8
CHUNK = 64
SUB = 16
LANES = 128
GROUP = 4
MIXER_BATCH = 2
UW = 3456
VMEM_LIMIT = 56 * 1024 * 1024

NN = (((1,), (0,)), ((), ()))
NT = (((1,), (1,)), ((), ()))
TN = (((0,), (0,)), ((), ()))


def _dot(a, b, dims=NN):
    return lax.dot_general(a.astype(BF16), b.astype(BF16), dims, preferred_element_type=F32)


def _split(x):
    hi = x.astype(BF16)
    lo = (x - hi.astype(F32)).astype(BF16)
    return hi, lo


def _dot3(a, b, dims=NN):
    ah, al = _split(a)
    bh, bl = _split(b)
    dg = lambda x, y: lax.dot_general(x, y, dims, preferred_element_type=F32)
    return dg(ah, bh) + dg(ah, bl) + dg(al, bh)


def _dot_exact_lhs(m, x, dims=NN):
    xh, xl = _split(x)
    mb = m.astype(BF16)
    dg = lambda y: lax.dot_general(mb, y, dims, preferred_element_type=F32)
    return dg(xh) + dg(xl)


def _dot_exact_rhs(x, m, dims=NN):
    xh, xl = _split(x)
    mb = m.astype(BF16)
    dg = lambda y: lax.dot_general(y, mb, dims, preferred_element_type=F32)
    return dg(xh) + dg(xl)


def _iota(shape, axis):
    return lax.broadcasted_iota(jnp.int32, shape, axis)


def _softplus(z):
    return jnp.maximum(z, 0.0) + jnp.log1p(jnp.exp(-jnp.abs(z)))


def _rms(x, g):
    return x * lax.rsqrt(jnp.mean(x * x, axis=-1, keepdims=True) + EPS) * g


def _lockstep(chains):
    chains = list(chains)
    while chains:
        alive = []
        for ch in chains:
            try:
                next(ch)
                alive.append(ch)
            except StopIteration:
                pass
        chains = alive
        yield


def _params(*sem):
    return pltpu.CompilerParams(dimension_semantics=sem, vmem_limit_bytes=VMEM_LIMIT)


def _norm_matmul_kernel(x_ref, g_ref, w_ref, o_ref, h_ref):
    @pl.when(pl.program_id(1) == 0)
    def _():
        h_ref[...] = _rms(x_ref[...], g_ref[...]).astype(BF16)

    o_ref[...] = jnp.dot(h_ref[...], w_ref[...], preferred_element_type=F32)


def _norm_matmul(x, g, w, tn):
    n, d = x.shape
    nout = w.shape[1]
    tm = min(n, 1024)
    return pl.pallas_call(
        _norm_matmul_kernel,
        grid=(n // tm, nout // tn),
        in_specs=[
            pl.BlockSpec((tm, d), lambda i, j: (i, 0)),
            pl.BlockSpec((1, d), lambda i, j: (0, 0)),
            pl.BlockSpec((d, tn), lambda i, j: (0, j)),
        ],
        out_specs=pl.BlockSpec((tm, tn), lambda i, j: (i, j)),
        out_shape=jax.ShapeDtypeStruct((n, nout), F32),
        scratch_shapes=[pltpu.VMEM((tm, d), BF16)],
        compiler_params=_params("parallel", "arbitrary"),
        name="norm_matmul",
    )(x, g, w)


def _proj_res_kernel(*refs, n_in):
    a_refs = refs[:n_in]
    w_refs = refs[n_in:2 * n_in]
    res_ref, o_ref = refs[2 * n_in], refs[2 * n_in + 1]
    acc = res_ref[...]
    for a_ref, w_ref in zip(a_refs, w_refs):
        acc = acc + jnp.dot(a_ref[...].astype(BF16), w_ref[...], preferred_element_type=F32)
    o_ref[...] = acc


def _proj_res(a_list, w_list, res):
    n, d = res.shape
    tm = min(n, 512)
    n_in = len(a_list)
    in_specs = [pl.BlockSpec((tm, a.shape[1]), lambda i: (i, 0)) for a in a_list]
    in_specs += [pl.BlockSpec(w.shape, lambda i: (0, 0)) for w in w_list]
    in_specs += [pl.BlockSpec((tm, d), lambda i: (i, 0))]
    return pl.pallas_call(
        functools.partial(_proj_res_kernel, n_in=n_in),
        grid=(n // tm,),
        in_specs=in_specs,
        out_specs=pl.BlockSpec((tm, d), lambda i: (i, 0)),
        out_shape=jax.ShapeDtypeStruct((n, d), F32),
        compiler_params=_params("parallel"),
        name="proj_res",
    )(*a_list, *w_list, res)


def _gla_kernel(q_ref, k_ref, v_ref, ga_ref, og_ref, s0_ref, w2_ref, bg_ref, gn_ref,
                o_ref, sout_ref, s_scr, *, C, NB):
    keep_s = (_iota((GLA_QK, A_WIDTH), 0) >> 6) == (_iota((GLA_QK, A_WIDTH), 1) >> 7)

    @pl.when(pl.program_id(1) == 0)
    def _():
        for bb in range(NB):
            s_scr[bb] = jnp.where(keep_s, jnp.concatenate([s0_ref[bb]] * GLA_HEADS, axis=1), 0.0)

    chains = [_gla_one(q_ref.at[bb], k_ref.at[bb], v_ref.at[bb], ga_ref.at[bb], og_ref.at[bb],
                       w2_ref, bg_ref, gn_ref, o_ref.at[bb], sout_ref.at[bb], s_scr.at[bb], keep_s, C=C)
              for bb in range(NB)]
    for _ in _lockstep(chains):
        pass


def _gla_one(q_ref, k_ref, v_ref, ga_ref, og_ref, w2_ref, bg_ref, gn_ref,
             o_ref, sout_ref, s_scr, keep_s, *, C):
    x = _dot3(ga_ref[...], w2_ref[...]) + bg_ref[...]
    yield
    logg = (jnp.minimum(x, 0.0) - jnp.log1p(jnp.exp(-jnp.abs(x)))) * (1.0 / GLA_TAU)
    tril = _iota((C, C), 0) >= _iota((C, C), 1)
    b = _dot_exact_lhs(tril, logg)
    g_col = _dot_exact_rhs(logg, jnp.ones((C, LANES), F32), TN)
    yield
    q = q_ref[...] * (GLA_DK ** -0.5)
    k = k_ref[...]
    v = v_ref[...]
    s = s_scr[...]
    o_inter = _dot(q * jnp.exp(b), s)
    bl = b[C - 1:C]
    upd = _dot(k * jnp.exp(bl - b), v, TN)

    head_k = _iota((1, GLA_QK), 1) >> 6
    head_v = _iota((1, A_WIDTH), 1) >> 7
    ones_kv = keep_s.astype(BF16)
    ridx = _iota((SUB, 1), 0)

    def bd(x, head_of_lane):
        xb = x.astype(BF16)
        zero = jnp.zeros_like(xb)
        return jnp.concatenate([jnp.where(head_of_lane == h, xb, zero) for h in range(GLA_HEADS)], axis=0)

    starts = range(0, C, SUB)
    atts = {}
    for i0 in starts[1:]:
        m = b[i0 - 1:i0]
        qi, bi = q[i0:i0 + SUB], b[i0:i0 + SUB]
        atts[i0] = _dot(qi * jnp.exp(bi - m), bd(k[:i0] * jnp.exp(m - b[:i0]), head_k), NT)
    yield
    scs = {}
    for i0 in starts:
        qi, bi, ki = q[i0:i0 + SUB], b[i0:i0 + SUB], k[i0:i0 + SUB]
        cols = []
        for j in range(SUB):
            e = jnp.exp(jnp.minimum(bi - bi[j:j + 1], 0.0))
            cols.append(jnp.where(ridx >= j, qi * e * ki[j:j + 1], 0.0))
        scs[i0] = _dot(jnp.concatenate(cols, axis=0), ones_kv)
    offs = {i0: _dot(atts[i0], bd(v[:i0], head_v)) for i0 in starts[1:]}
    yield
    blocks = []
    for i0 in starts:
        vi = v[i0:i0 + SUB]
        oi = o_inter[i0:i0 + SUB]
        if i0 > 0:
            oi = oi + offs[i0]
        for j in range(SUB):
            oi = oi + scs[i0][j * SUB:(j + 1) * SUB] * vi[j:j + 1]
        blocks.append(oi)
    o = jnp.concatenate(blocks, axis=0)

    e_col = jnp.concatenate([jnp.exp(g_col)] * (A_WIDTH // LANES), axis=1)
    s_new = e_col * s + jnp.where(keep_s, upd, 0.0)
    s_scr[...] = s_new
    sout_ref[...] = (s_new[:, 0:GLA_DV] + s_new[:, GLA_DV:2 * GLA_DV]
                     + s_new[:, 2 * GLA_DV:3 * GLA_DV] + s_new[:, 3 * GLA_DV:4 * GLA_DV])

    outs = [_rms(o[:, h * GLA_DV:(h + 1) * GLA_DV], gn_ref[...]) for h in range(GLA_HEADS)]
    og = og_ref[...]
    o_ref[...] = jnp.concatenate(outs, axis=1) * (og * jax.nn.sigmoid(og))


def _gla(u, s0, w2p, bg, gn, B, T, C):
    nc = T // C
    NB = MIXER_BATCH
    s0 = s0.reshape(B, GLA_QK, GLA_DV)
    u = u.reshape(B, T, UW)
    o, s_fin = pl.pallas_call(
        functools.partial(_gla_kernel, C=C, NB=NB),
        grid=(B // NB, nc),
        in_specs=[
            pl.BlockSpec((NB, C, GLA_QK), lambda b, c: (b, c, 0)),
            pl.BlockSpec((NB, C, GLA_QK), lambda b, c: (b, c, 1)),
            pl.BlockSpec((NB, C, A_WIDTH), lambda b, c: (b, c, 1)),
            pl.BlockSpec((NB, C, LANES), lambda b, c: (b, c, 24)),
            pl.BlockSpec((NB, C, A_WIDTH), lambda b, c: (b, c, 2)),
            pl.BlockSpec((NB, GLA_QK, GLA_DV), lambda b, c: (b, 0, 0)),
            pl.BlockSpec((LANES, GLA_QK), lambda b, c: (0, 0)),
            pl.BlockSpec((1, GLA_QK), lambda b, c: (0, 0)),
            pl.BlockSpec((1, GLA_DV), lambda b, c: (0, 0)),
        ],
        out_specs=[
            pl.BlockSpec((NB, C, A_WIDTH), lambda b, c: (b, c, 0)),
            pl.BlockSpec((NB, GLA_QK, GLA_DV), lambda b, c: (b, 0, 0)),
        ],
        out_shape=[
            jax.ShapeDtypeStruct((B, T, A_WIDTH), F32),
            jax.ShapeDtypeStruct((B, GLA_QK, GLA_DV), F32),
        ],
        scratch_shapes=[pltpu.VMEM((NB, GLA_QK, A_WIDTH), F32)],
        compiler_params=_params("parallel", "arbitrary"),
        name="gla",
    )(u, u, u, u, u, s0, w2p, bg, gn)
    return o.reshape(B * T, A_WIDTH), s_fin.reshape(B, GLA_HEADS, GLA_DK, GLA_DV)


N_RWKV_PER_BATCH = (11, 2, 6)
N_RWKV_SHARED = 15


def _rwkv_kernel(*refs, C, NB):
    n_in, n_out, n_scr = N_RWKV_PER_BATCH
    ins, shared = refs[:n_in], refs[n_in:n_in + N_RWKV_SHARED]
    outs = refs[n_in + N_RWKV_SHARED:n_in + N_RWKV_SHARED + n_out]
    scr = refs[n_in + N_RWKV_SHARED + n_out:]
    GW = GROUP * RWKV_N
    keep_h = (_iota((GW, GW), 0) >> 6) == (_iota((GW, GW), 1) >> 6)

    @pl.when(pl.program_id(1) == 0)
    def _():
        h0_ref, h_scr = ins[10], scr[5]
        for bb in range(NB):
            for prev, sh in zip(scr[:5], ins[5:10]):
                prev[bb] = sh[bb]
            h0 = h0_ref[bb]
            for g in range(RWKV_HEADS // GROUP):
                t = jnp.concatenate([h0[:, g * GW:(g + 1) * GW]] * GROUP, axis=0)
                h_scr[bb, g] = jnp.where(keep_h, t, 0.0)

    at = lambda rs, bb: [r.at[bb] for r in rs]
    chains = [_rwkv_one(*at(ins[:5], bb), *shared, *at(outs, bb), *at(scr, bb), keep_h, C=C) for bb in range(NB)]
    for _ in _lockstep(chains):
        pass


def _rwkv_one(r_ref, k_ref, v_ref, wa_ref, gl_ref,
              mu_r, mu_k, mu_v, mu_wa, mu_gl, w0_ref, w2_ref, a0_ref, a2_ref, g2_ref,
              kk_ref, ka_ref, rk_ref, gng_ref, gnb_ref,
              y_ref, hout_ref,
              pr, pk, pv, pwa, pgl, h_scr, keep_h, *, C):
    GW = GROUP * RWKV_N
    CH = GROUP * C
    logc = C.bit_length() - 1
    n_groups = RWKV_HEADS // GROUP

    def head_mask(rows, cols, row_shift, col_of_lane):
        keep = (_iota((rows, cols), 0) >> row_shift) == col_of_lane(_iota((rows, cols), 1))
        return keep.astype(F32).astype(BF16)

    keep_rows = head_mask(CH, GW, logc, lambda l: l >> 6)
    keep_rows2 = head_mask(CH, 2 * GW, logc, lambda l: (l & (GW - 1)) >> 6)
    keep_sq = head_mask(CH, CH, logc, lambda l: l >> logc)

    def bd(x, keep):
        return jnp.concatenate([x.astype(BF16)] * GROUP, axis=0) * keep

    rid = _iota((C, 1), 0)

    def shift(x_ref, prev, mu):
        x = x_ref[...]
        xp = jnp.where(rid == 0, prev[...], pltpu.roll(x, 1, axis=0))
        prev[...] = x[C - 1:C]
        return x + mu[...] * (xp - x)

    rs = shift(r_ref, pr, mu_r)
    ks = shift(k_ref, pk, mu_k)
    vs = shift(v_ref, pv, mu_v)
    was = shift(wa_ref, pwa, mu_wa)
    gls = shift(gl_ref, pgl, mu_gl)

    dec = -_softplus(-(w0_ref[...] + _dot3(jnp.tanh(was), w2_ref[...]))) - 0.5
    logw = -jnp.exp(dec)
    a = jax.nn.sigmoid(a0_ref[...] + _dot(was, a2_ref[...]))
    gate = _dot(jax.nn.sigmoid(gls), g2_ref[...])
    yield
    head_ones = ((_iota((B_WIDTH, B_WIDTH), 0) >> 6) == (_iota((B_WIDTH, B_WIDTH), 1) >> 6)).astype(BF16)
    kkr = ks * kk_ref[...]
    kk = kkr / jnp.maximum(jnp.sqrt(_dot(kkr * kkr, head_ones)), 1e-12)
    km = ks * (1.0 + (a - 1.0) * ka_ref[...])
    beta = kk * a
    tril = _iota((C, C), 0) >= _iota((C, C), 1)
    gc = _dot_exact_lhs(tril, logw)
    g_end = gc[C - 1:C]
    kap = kk * jnp.exp(gc - logw)
    rt = rs * jnp.exp(gc)
    e_neg = jnp.exp(-gc)
    b_inv = beta * e_neg
    k_inv = km * e_neg
    e_end = jnp.exp(g_end - gc)
    k_hat = km * e_end
    b_hat = beta * e_end

    col_s = _iota((C, CH), 1) & (C - 1)
    row_t = _iota((C, CH), 0)
    strict = row_t > col_s
    incl = row_t >= col_s
    same_sub = (row_t >> 4) == (col_s >> 4)
    eye_cat = (row_t == col_s).astype(F32)
    ones_c = jnp.ones((C, LANES), F32)

    yield
    ys = [None] * n_groups

    def group(g):
        L = slice(g * GW, (g + 1) * GW)
        p = jnp.concatenate([kap[:, L], rt[:, L]], axis=0)
        zb = _dot(p, bd(b_inv[:, L], keep_rows), NT)
        zk = _dot(p, bd(k_inv[:, L], keep_rows), NT)
        yield
        a_b = jnp.where(strict, zb[:C], 0.0)
        b_b = jnp.where(incl, zb[C:], 0.0)
        a_k = jnp.where(strict, zk[:C], 0.0)
        b_k = jnp.where(incl, zk[C:], 0.0)
        bd_v = bd(vs[:, L], keep_rows)
        akv = _dot(a_k, bd_v)

        d_blk = jnp.where(same_sub, a_b, 0.0)
        x1 = -d_blk
        p1 = eye_cat + x1
        x2 = _dot(x1, bd(x1, keep_sq))
        yield
        rhs = jnp.concatenate([kap[:, L], akv], axis=1)
        r2 = _dot(jnp.concatenate([x2, p1], axis=0), bd(x2, keep_sq))
        yield
        x4, p2 = r2[:C], p1 + r2[C:]
        r4 = _dot(jnp.concatenate([x4, p2], axis=0), bd(x4, keep_sq))
        yield
        x8, p3 = r4[:C], p2 + r4[C:]
        t_d = p3 + _dot(p3, bd(x8, keep_sq))
        yield

        n1 = _dot(t_d, bd(a_b - d_blk, keep_sq))
        s1 = _dot(t_d, bd(rhs, keep_rows2))
        yield
        n2 = _dot(n1, bd(n1, keep_sq))
        yield
        s2 = s1 + _dot(n2, bd(s1, keep_rows2))
        yield
        sol = s2 - _dot(n1, bd(s2, keep_rows2))
        yield
        w_mat, u0 = sol[:, :GW], sol[:, GW:]

        hbd = h_scr[g]
        hb = hbd.astype(BF16)
        u = _dot(w_mat, hb) + u0
        yield
        y = _dot(rt[:, L], hb) + _dot(b_k, bd_v) - _dot(b_b, bd(u, keep_rows))
        m = _dot(k_hat[:, L], vs[:, L], TN) - _dot(b_hat[:, L], u, TN)
        g_col = _dot_exact_rhs(logw[:, L], ones_c, TN)[:, :1]
        yield
        h_new = jnp.exp(g_col) * hbd + jnp.where(keep_h, m, 0.0)
        h_scr[g] = h_new
        hout_ref[:, L] = (h_new[0:RWKV_N] + h_new[RWKV_N:2 * RWKV_N]
                          + h_new[2 * RWKV_N:3 * RWKV_N] + h_new[3 * RWKV_N:4 * RWKV_N])
        ys[g] = y

    yield from _lockstep([group(g) for g in range(n_groups)])
    y = jnp.concatenate(ys, axis=1)

    inv_n = 1.0 / RWKV_N
    mu = _dot(y, head_ones) * inv_n
    d = y - mu
    var = _dot(d * d, head_ones) * inv_n
    yn = d * lax.rsqrt(var + RWKV_GN_EPS) * gng_ref[...] + gnb_ref[...]
    bonus = _dot(rs * km * rk_ref[...], head_ones) * vs
    y_ref[...] = (yn + bonus) * gate


def _rwkv(u, shift0, h0, wts, B, T, C):
    nc = T // C
    NB = MIXER_BATCH
    wide = lambda blk: pl.BlockSpec((NB, C, B_WIDTH), lambda b, c: (b, c, blk))
    narrow = lambda blk: pl.BlockSpec((NB, C, LANES), lambda b, c: (b, c, blk))
    sh = lambda w, blk: pl.BlockSpec((NB, 1, w), lambda b, c: (b, 0, blk))
    const = lambda shape: pl.BlockSpec(shape, lambda b, c: (0,) * len(shape))
    assert len(wts) == N_RWKV_SHARED
    in_specs = [wide(3), wide(4), wide(5), narrow(25), narrow(26),
                sh(B_WIDTH, 0), sh(B_WIDTH, 1), sh(B_WIDTH, 2), sh(LANES, 12), sh(LANES, 13),
                pl.BlockSpec((NB, RWKV_N, B_WIDTH), lambda b, c: (b, 0, 0))]
    in_specs += [const(w.shape) for w in wts]
    GW = GROUP * RWKV_N
    u = u.reshape(B, T, UW)
    y, h_fin = pl.pallas_call(
        functools.partial(_rwkv_kernel, C=C, NB=NB),
        grid=(B // NB, nc),
        in_specs=in_specs,
        out_specs=[
            pl.BlockSpec((NB, C, B_WIDTH), lambda b, c: (b, c, 0)),
            pl.BlockSpec((NB, RWKV_N, B_WIDTH), lambda b, c: (b, 0, 0)),
        ],
        out_shape=[
            jax.ShapeDtypeStruct((B, T, B_WIDTH), F32),
            jax.ShapeDtypeStruct((B, RWKV_N, B_WIDTH), F32),
        ],
        scratch_shapes=[
            pltpu.VMEM((NB, 1, B_WIDTH), F32), pltpu.VMEM((NB, 1, B_WIDTH), F32), pltpu.VMEM((NB, 1, B_WIDTH), F32),
            pltpu.VMEM((NB, 1, LANES), F32), pltpu.VMEM((NB, 1, LANES), F32),
            pltpu.VMEM((NB, RWKV_HEADS // GROUP, GW, GW), F32),
        ],
        compiler_params=_params("parallel", "arbitrary"),
        name="rwkv7",
    )(u, u, u, u, u, shift0, shift0, shift0, shift0, shift0, h0, *wts)
    return y.reshape(B * T, B_WIDTH), h_fin


def _ret_kernel(q_ref, k_ref, v_ref, g_ref, cos_ref, sin_ref, r0_ref, gng_ref, gnb_ref,
                o_ref, rout_ref, r_scr, *, C):
    c = pl.program_id(1)

    @pl.when(c == 0)
    def _():
        r_scr[...] = r0_ref[0]

    cos = cos_ref[...]
    sin = sin_ref[...]
    half = RET_DK // 2

    def rope(x):
        x1, x2 = x[:, :half], x[:, half:]
        return jnp.concatenate([x1 * cos - x2 * sin, x1 * sin + x2 * cos], axis=1)

    ri = _iota((C, C), 0)
    ci = _iota((C, C), 1)
    causal = ri >= ci
    diff = (ri - ci).astype(F32)
    pos1 = (_iota((C, 1), 0) + 1).astype(F32)
    outs = []
    for h in range(RET_HEADS):
        lg = math.log1p(-(2.0 ** (-5.0 - h)))
        q = rope(q_ref[:, h * RET_DK:(h + 1) * RET_DK])
        k = rope(k_ref[:, h * RET_DK:(h + 1) * RET_DK]) * (RET_DK ** -0.5)
        v = v_ref[:, h * RET_DV:(h + 1) * RET_DV]
        r = r_scr[h]
        decay = jnp.where(causal, jnp.exp(diff * lg), 0.0)
        att = _dot(q, k, NT) * decay
        o = _dot(att, v) + _dot(q, r) * jnp.exp(pos1 * lg)
        r_scr[h] = math.exp(C * lg) * r + _dot(k * jnp.exp((C - pos1) * lg), v, TN)
        mu = jnp.mean(o, axis=-1, keepdims=True)
        d = o - mu
        var = jnp.mean(d * d, axis=-1, keepdims=True)
        outs.append(d * lax.rsqrt(var + RET_GN_EPS))
    g = g_ref[...]
    o = jnp.concatenate(outs, axis=1) * gng_ref[...] + gnb_ref[...]
    o_ref[...] = (g * jax.nn.sigmoid(g)) * o
    rout_ref[0] = r_scr[...]


def _ret(u, cos, sin, r0, gn_g, gn_b, B, T, C):
    nc = T // C
    row = lambda b, c: b * nc + c
    return pl.pallas_call(
        functools.partial(_ret_kernel, C=C),
        grid=(B, nc),
        in_specs=[
            pl.BlockSpec((C, RET_QK), lambda b, c: (row(b, c), 0)),
            pl.BlockSpec((C, RET_QK), lambda b, c: (row(b, c), 1)),
            pl.BlockSpec((C, RET_V), lambda b, c: (row(b, c), 1)),
            pl.BlockSpec((C, RET_V), lambda b, c: (row(b, c), 2)),
            pl.BlockSpec((C, RET_DK // 2), lambda b, c: (c, 0)),
            pl.BlockSpec((C, RET_DK // 2), lambda b, c: (c, 0)),
            pl.BlockSpec((1, RET_HEADS, RET_DK, RET_DV), lambda b, c: (b, 0, 0, 0)),
            pl.BlockSpec((1, RET_V), lambda b, c: (0, 0)),
            pl.BlockSpec((1, RET_V), lambda b, c: (0, 0)),
        ],
        out_specs=[
            pl.BlockSpec((C, RET_V), lambda b, c: (row(b, c), 0)),
            pl.BlockSpec((1, RET_HEADS, RET_DK, RET_DV), lambda b, c: (b, 0, 0, 0)),
        ],
        out_shape=[
            jax.ShapeDtypeStruct((B * T, RET_V), F32),
            jax.ShapeDtypeStruct((B, RET_HEADS, RET_DK, RET_DV), F32),
        ],
        scratch_shapes=[pltpu.VMEM((RET_HEADS, RET_DK, RET_DV), F32)],
        compiler_params=_params("parallel", "arbitrary"),
        name="retention",
    )(u, u, u, u, cos, sin, r0, gn_g, gn_b)


def _router_kernel(x_ref, g_ref, wr_ref, o_ref):
    h = _rms(x_ref[...], g_ref[...])
    logits = _dot3(h, wr_ref[...])
    lane = _iota(logits.shape, 1)
    valid = lane < N_EXPERTS
    logits = jnp.where(valid, logits, -jnp.inf)
    m = jnp.max(logits, axis=-1, keepdims=True)
    e = jnp.where(valid, jnp.exp(logits - m), 0.0)
    p = e / jnp.sum(e, axis=-1, keepdims=True)
    big = jnp.int32(LANES)
    p1 = jnp.max(p, axis=-1, keepdims=True)
    i1 = jnp.min(jnp.where(p == p1, lane, big), axis=-1, keepdims=True)
    rest = jnp.where((lane == i1) | ~valid, -1.0, p)
    p2 = jnp.max(rest, axis=-1, keepdims=True)
    i2 = jnp.min(jnp.where(rest == p2, lane, big), axis=-1, keepdims=True)
    tot = p1 + p2
    o_ref[...] = jnp.where(lane == i1, p1 / tot, 0.0) + jnp.where(lane == i2, p2 / tot, 0.0)


def _router(x, g, wr):
    n, d = x.shape
    tm = min(n, 512)
    return pl.pallas_call(
        _router_kernel,
        grid=(n // tm,),
        in_specs=[
            pl.BlockSpec((tm, d), lambda i: (i, 0)),
            pl.BlockSpec((1, d), lambda i: (0, 0)),
            pl.BlockSpec((d, LANES), lambda i: (0, 0)),
        ],
        out_specs=pl.BlockSpec((tm, LANES), lambda i: (i, 0)),
        out_shape=jax.ShapeDtypeStruct((n, LANES), F32),
        compiler_params=_params("parallel"),
        name="router",
    )(x, g, wr)


def _ffn_kernel(*refs, gated, final_norm):
    if gated:
        x_ref, g_ref, gates_ref, w1_ref, w3_ref, w2_ref, gf_ref, o_ref, h_scr, acc = refs
    else:
        x_ref, g_ref, w1_ref, w3_ref, w2_ref, gf_ref, o_ref, h_scr, acc = refs
    e = pl.program_id(1)
    f = pl.program_id(2)

    @pl.when((e == 0) & (f == 0))
    def _():
        h_scr[...] = _rms(x_ref[...], g_ref[...]).astype(BF16)
        acc[...] = jnp.zeros_like(acc)

    h = h_scr[...]
    a1 = jnp.dot(h, w1_ref[0], preferred_element_type=F32)
    a3 = jnp.dot(h, w3_ref[0], preferred_element_type=F32)
    act = a1 * jax.nn.sigmoid(a1) * a3
    if gated:
        gates = gates_ref[...]
        ge = jnp.sum(jnp.where(_iota(gates.shape, 1) == e, gates, 0.0), axis=-1, keepdims=True)
        act = act * ge
    acc[...] += jnp.dot(act.astype(BF16), w2_ref[0], preferred_element_type=F32)

    @pl.when((e == pl.num_programs(1) - 1) & (f == pl.num_programs(2) - 1))
    def _():
        y = x_ref[...] + acc[...]
        o_ref[...] = _rms(y, gf_ref[...]) if final_norm else y


def _ffn(x, g, w1, w3, w2, gates=None, g_final=None):
    n, d = x.shape
    ne, _, dff = w1.shape
    tm = min(n, 1024)
    tf = 256
    gated = gates is not None
    final_norm = g_final is not None
    if g_final is None:
        g_final = g
    in_specs = [pl.BlockSpec((tm, d), lambda i, e, f: (i, 0)),
                pl.BlockSpec((1, d), lambda i, e, f: (0, 0))]
    args = [x, g]
    if gated:
        in_specs.append(pl.BlockSpec((tm, LANES), lambda i, e, f: (i, 0)))
        args.append(gates)
    in_specs += [pl.BlockSpec((1, d, tf), lambda i, e, f: (e, 0, f)),
                 pl.BlockSpec((1, d, tf), lambda i, e, f: (e, 0, f)),
                 pl.BlockSpec((1, tf, d), lambda i, e, f: (e, f, 0)),
                 pl.BlockSpec((1, d), lambda i, e, f: (0, 0))]
    args += [w1, w3, w2, g_final]
    return pl.pallas_call(
        functools.partial(_ffn_kernel, gated=gated, final_norm=final_norm),
        grid=(n // tm, ne, dff // tf),
        in_specs=in_specs,
        out_specs=pl.BlockSpec((tm, d), lambda i, e, f: (i, 0)),
        out_shape=jax.ShapeDtypeStruct((n, d), F32),
        scratch_shapes=[pltpu.VMEM((tm, d), BF16), pltpu.VMEM((tm, d), F32)],
        compiler_params=_params("parallel", "arbitrary", "arbitrary"),
        name="ffn",
    )(*args)


def _prep_weights(p):
    w_in = p["ab_w_in"][0]
    wa, wb = w_in[:, :A_COLS], w_in[:, A_COLS:]
    pad = jnp.zeros((D_MODEL, LANES - GLA_GATE_RANK), F32)
    w_in_p = jnp.concatenate(
        [wa[:, :2 * GLA_QK + A_WIDTH], wa[:, 2 * GLA_QK + A_WIDTH + GLA_GATE_RANK:], wb[:, :3 * B_WIDTH],
         wa[:, 2 * GLA_QK + A_WIDTH:2 * GLA_QK + A_WIDTH + GLA_GATE_RANK], pad, wb[:, 3 * B_WIDTH:]], axis=1)
    row = lambda v: v.reshape(1, -1).astype(F32)
    mu = p["rwkv_mu"][0]
    z64 = jnp.zeros((64, B_WIDTH), F32)
    w = dict(
        norm_mix=p["norm_mix"], norm_ffn=p["norm_ffn"], norm_final=row(p["norm_final"]),
        w_in=w_in_p.astype(BF16),
        w_out_a=p["ab_w_out"][0][:A_WIDTH].astype(BF16), w_out_b=p["ab_w_out"][0][A_WIDTH:].astype(BF16),
        gla_w2=jnp.concatenate([p["gla_w_gate2"][0], jnp.zeros((LANES - GLA_GATE_RANK, GLA_QK), F32)], axis=0),
        gla_bg=row(p["gla_b_gate"][0]), gla_norm=row(p["gla_norm"][0]),
        rwkv=(row(mu[:B_WIDTH]), row(mu[B_WIDTH:2 * B_WIDTH]), row(mu[2 * B_WIDTH:3 * B_WIDTH]),
              row(mu[3 * B_WIDTH:3 * B_WIDTH + LANES]), row(mu[3 * B_WIDTH + LANES:]),
              row(p["rwkv_w0"][0]), jnp.concatenate([p["rwkv_w2"][0], z64], axis=0),
              row(p["rwkv_a0"][0]), jnp.concatenate([z64, p["rwkv_a2"][0]], axis=0),
              p["rwkv_g2"][0], row(p["rwkv_k_k"][0]), row(p["rwkv_k_a"][0]), row(p["rwkv_r_k"][0]),
              row(p["rwkv_gn_g"][0]), row(p["rwkv_gn_b"][0])),
        ret_w_in=p["ret_w_in"][0].astype(BF16), ret_w_out=p["ret_w_out"][0].astype(BF16),
        ret_gn_g=row(p["ret_gn_g"][0]), ret_gn_b=row(p["ret_gn_b"][0]),
        ffn_w1=p["ffn_w1"].astype(BF16), ffn_w3=p["ffn_w3"].astype(BF16), ffn_w2=p["ffn_w2"].astype(BF16),
        router=jnp.concatenate([p["moe_router"][0], jnp.zeros((D_MODEL, LANES - N_EXPERTS), F32)], axis=1),
        moe_w1=p["moe_w1"][0].astype(BF16), moe_w3=p["moe_w3"][0].astype(BF16), moe_w2=p["moe_w2"][0].astype(BF16),
    )
    return w


def _trunk(x, pos0, st_gla, st_rwkv, st_shift, st_ret, w):
    B, T, D = x.shape
    n = B * T
    x0 = x.reshape(n, D)
    c_ab = min(CHUNK, T)
    c_ret = min(256, T)

    u = _norm_matmul(x0, w["norm_mix"][0:1], w["w_in"], tn=UW // 3)
    oa, s_gla = _gla(u, st_gla, w["gla_w2"], w["gla_bg"], w["gla_norm"], B, T, c_ab)
    h0 = st_rwkv.transpose(0, 3, 1, 2).reshape(B, RWKV_N, B_WIDTH)
    yb, h_fin = _rwkv(u, st_shift, h0, w["rwkv"], B, T, c_ab)
    s_rwkv = h_fin.reshape(B, RWKV_N, RWKV_HEADS, RWKV_N).transpose(0, 2, 3, 1)
    u_last = u.reshape(B, T, UW)[:, T - 1:, :]
    s_shift = jnp.concatenate([u_last[..., 3 * A_WIDTH:6 * A_WIDTH], u_last[..., UW - 2 * LANES:]], axis=-1)
    x1 = _proj_res([oa, yb], [w["w_out_a"], w["w_out_b"]], x0)
    x2 = _ffn(x1, w["norm_ffn"][0:1], w["ffn_w1"], w["ffn_w3"], w["ffn_w2"])

    ur = _norm_matmul(x2, w["norm_mix"][1:2], w["ret_w_in"], tn=1536)
    half = RET_DK // 2
    inv = 1.0 / (ROPE_BASE ** (jnp.arange(0, RET_DK, 2, dtype=F32) / RET_DK))
    ang = (pos0 + jnp.arange(T)).astype(F32)[:, None] * inv[None, :]
    o, s_ret = _ret(ur, jnp.cos(ang), jnp.sin(ang), st_ret, w["ret_gn_g"], w["ret_gn_b"], B, T, c_ret)
    x3 = _proj_res([o], [w["ret_w_out"]], x2)
    gates = _router(x3, w["norm_ffn"][1:2], w["router"])
    y = _ffn(x3, w["norm_ffn"][1:2], w["moe_w1"], w["moe_w3"], w["moe_w2"], gates=gates, g_final=w["norm_final"])
    return y.reshape(B, T, D), s_gla[None], s_rwkv[None], s_shift[None], s_ret[None]


def kernel(x_prompt, x_sample, state_gla, state_rwkv, state_shift, state_ret, norm_mix, norm_ffn, norm_final, ab_w_in, ab_w_out, gla_w_gate2, gla_b_gate, gla_norm, rwkv_mu, rwkv_w0, rwkv_w2, rwkv_a0, rwkv_a2, rwkv_g2, rwkv_k_k, rwkv_k_a, rwkv_r_k, rwkv_gn_g, rwkv_gn_b, ret_w_in, ret_gn_g, ret_gn_b, ret_w_out, ffn_w1, ffn_w3, ffn_w2, moe_router, moe_w1, moe_w3, moe_w2):
    p = dict(norm_mix=norm_mix, norm_ffn=norm_ffn, norm_final=norm_final, ab_w_in=ab_w_in, ab_w_out=ab_w_out,
             gla_w_gate2=gla_w_gate2, gla_b_gate=gla_b_gate, gla_norm=gla_norm, rwkv_mu=rwkv_mu, rwkv_w0=rwkv_w0,
             rwkv_w2=rwkv_w2, rwkv_a0=rwkv_a0, rwkv_a2=rwkv_a2, rwkv_g2=rwkv_g2, rwkv_k_k=rwkv_k_k,
             rwkv_k_a=rwkv_k_a, rwkv_r_k=rwkv_r_k, rwkv_gn_g=rwkv_gn_g, rwkv_gn_b=rwkv_gn_b, ret_w_in=ret_w_in,
             ret_gn_g=ret_gn_g, ret_gn_b=ret_gn_b, ret_w_out=ret_w_out, ffn_w1=ffn_w1, ffn_w3=ffn_w3, ffn_w2=ffn_w2,
             moe_router=moe_router, moe_w1=moe_w1, moe_w3=moe_w3, moe_w2=moe_w2)
    w = _prep_weights(p)
    bp, tp = x_prompt.shape[0], x_prompt.shape[1]
    dt = x_prompt.dtype
    z_gla = jnp.zeros((bp, GLA_HEADS, GLA_DK, GLA_DV), dt)
    z_rwkv = jnp.zeros((bp, RWKV_HEADS, RWKV_N, RWKV_N), dt)
    z_shift = jnp.zeros((bp, 1, B_COLS), dt)
    z_ret = jnp.zeros((bp, RET_HEADS, RET_DK, RET_DV), dt)
    past = 2048
    yp = _trunk(x_prompt, 0, z_gla, z_rwkv, z_shift, z_ret, w)
    ys = _trunk(x_sample, past, state_gla[0], state_rwkv[0], state_shift[0], state_ret[0], w)
    return (yp[0], ys[0], yp[1], yp[2], yp[3], yp[4], ys[1], ys[2], ys[3], ys[4])
```

```python
import functools
import math

import jax
import jax.numpy as jnp
from jax import lax
from jax.experimental import pallas as pl
from jax.experimental.pallas import tpu as pltpu

F32 = jnp.float32
BF16 = jnp.bfloat16

D_MODEL = 1024
EPS = 1e-6
GLA_HEADS, GLA_DK, GLA_DV = 4, 64, 128
GLA_QK = GLA_HEADS * GLA_DK
A_WIDTH = GLA_HEADS * GLA_DV
GLA_GATE_RANK = 16
GLA_TAU = 16.0
RWKV_HEADS, RWKV_N = 8, 64
B_WIDTH = RWKV_HEADS * RWKV_N
RWKV_GN_EPS = 64e-5
B_COLS = 3 * B_WIDTH + 64 + 64 + 128
A_COLS = 2 * GLA_QK + A_WIDTH + GLA_GATE_RANK + A_WIDTH
RET_HEADS = 4
RET_DK = D_MODEL // RET_HEADS
RET_DV = 2 * D_MODEL // RET_HEADS
RET_QK = RET_HEADS * RET_DK
RET_V = RET_HEADS * RET_DV
RET_GN_EPS = 1e-5
ROPE_BASE = 10000.0
D_FF = 2816
N_EXPERTS = 8
CHUNK = 64
SUB = 16
LANES = 128
GROUP = 4
MIXER_BATCH = 2
MOE_TILE = 512
GATHER_ROWS = 512
UW = 3456
VMEM_LIMIT = 56 * 1024 * 1024

NN = (((1,), (0,)), ((), ()))
NT = (((1,), (1,)), ((), ()))
TN = (((0,), (0,)), ((), ()))


def _dot(a, b, dims=NN):
    return lax.dot_general(a.astype(BF16), b.astype(BF16), dims, preferred_element_type=F32)


def _split(x):
    hi = x.astype(BF16)
    lo = (x - hi.astype(F32)).astype(BF16)
    return hi, lo


def _dot3(a, b, dims=NN):
    ah, al = _split(a)
    bh, bl = _split(b)
    dg = lambda x, y: lax.dot_general(x, y, dims, preferred_element_type=F32)
    return dg(ah, bh) + dg(ah, bl) + dg(al, bh)


def _dot_exact_lhs(m, x, dims=NN):
    xh, xl = _split(x)
    mb = m.astype(BF16)
    dg = lambda y: lax.dot_general(mb, y, dims, preferred_element_type=F32)
    return dg(xh) + dg(xl)


def _dot_exact_rhs(x, m, dims=NN):
    xh, xl = _split(x)
    mb = m.astype(BF16)
    dg = lambda y: lax.dot_general(y, mb, dims, preferred_element_type=F32)
    return dg(xh) + dg(xl)


def _iota(shape, axis):
    return lax.broadcasted_iota(jnp.int32, shape, axis)


def _softplus(z):
    return jnp.maximum(z, 0.0) + jnp.log1p(jnp.exp(-jnp.abs(z)))


def _rms(x, g):
    return x * lax.rsqrt(jnp.mean(x * x, axis=-1, keepdims=True) + EPS) * g


def _lockstep(chains):
    chains = list(chains)
    while chains:
        alive = []
        for ch in chains:
            try:
                next(ch)
                alive.append(ch)
            except StopIteration:
                pass
        chains = alive
        yield


def _params(*sem):
    return pltpu.CompilerParams(dimension_semantics=sem, vmem_limit_bytes=VMEM_LIMIT)


def _norm_matmul_kernel(x_ref, g_ref, w_ref, o_ref, h_ref):
    @pl.when(pl.program_id(1) == 0)
    def _():
        h_ref[...] = _rms(x_ref[...], g_ref[...]).astype(BF16)

    o_ref[...] = jnp.dot(h_ref[...], w_ref[...], preferred_element_type=F32)


def _norm_matmul(x, g, w, tn):
    n, d = x.shape
    nout = w.shape[1]
    tm = min(n, 1024)
    return pl.pallas_call(
        _norm_matmul_kernel,
        grid=(n // tm, nout // tn),
        in_specs=[
            pl.BlockSpec((tm, d), lambda i, j: (i, 0)),
            pl.BlockSpec((1, d), lambda i, j: (0, 0)),
            pl.BlockSpec((d, tn), lambda i, j: (0, j)),
        ],
        out_specs=pl.BlockSpec((tm, tn), lambda i, j: (i, j)),
        out_shape=jax.ShapeDtypeStruct((n, nout), F32),
        scratch_shapes=[pltpu.VMEM((tm, d), BF16)],
        compiler_params=_params("parallel", "arbitrary"),
        name="norm_matmul",
    )(x, g, w)


def _proj_res_kernel(*refs, n_in):
    a_refs = refs[:n_in]
    w_refs = refs[n_in:2 * n_in]
    res_ref, o_ref = refs[2 * n_in], refs[2 * n_in + 1]
    acc = res_ref[...]
    for a_ref, w_ref in zip(a_refs, w_refs):
        acc = acc + jnp.dot(a_ref[...].astype(BF16), w_ref[...], preferred_element_type=F32)
    o_ref[...] = acc


def _proj_res(a_list, w_list, res):
    n, d = res.shape
    tm = min(n, 512)
    n_in = len(a_list)
    in_specs = [pl.BlockSpec((tm, a.shape[1]), lambda i: (i, 0)) for a in a_list]
    in_specs += [pl.BlockSpec(w.shape, lambda i: (0, 0)) for w in w_list]
    in_specs += [pl.BlockSpec((tm, d), lambda i: (i, 0))]
    return pl.pallas_call(
        functools.partial(_proj_res_kernel, n_in=n_in),
        grid=(n // tm,),
        in_specs=in_specs,
        out_specs=pl.BlockSpec((tm, d), lambda i: (i, 0)),
        out_shape=jax.ShapeDtypeStruct((n, d), F32),
        compiler_params=_params("parallel"),
        name="proj_res",
    )(*a_list, *w_list, res)


def _gla_kernel(q_ref, k_ref, v_ref, ga_ref, og_ref, s0_ref, w2_ref, bg_ref, gn_ref,
                o_ref, sout_ref, s_scr, *, C, NB):
    keep_s = (_iota((GLA_QK, A_WIDTH), 0) >> 6) == (_iota((GLA_QK, A_WIDTH), 1) >> 7)

    @pl.when(pl.program_id(1) == 0)
    def _():
        for bb in range(NB):
            s_scr[bb] = jnp.where(keep_s, jnp.concatenate([s0_ref[bb]] * GLA_HEADS, axis=1), 0.0)

    chains = [_gla_one(q_ref.at[bb], k_ref.at[bb], v_ref.at[bb], ga_ref.at[bb], og_ref.at[bb],
                       w2_ref, bg_ref, gn_ref, o_ref.at[bb], sout_ref.at[bb], s_scr.at[bb], keep_s, C=C)
              for bb in range(NB)]
    for _ in _lockstep(chains):
        pass


def _gla_one(q_ref, k_ref, v_ref, ga_ref, og_ref, w2_ref, bg_ref, gn_ref,
             o_ref, sout_ref, s_scr, keep_s, *, C):
    x = _dot3(ga_ref[...], w2_ref[...]) + bg_ref[...]
    yield
    logg = (jnp.minimum(x, 0.0) - jnp.log1p(jnp.exp(-jnp.abs(x)))) * (1.0 / GLA_TAU)
    tril = _iota((C, C), 0) >= _iota((C, C), 1)
    b = _dot_exact_lhs(tril, logg)
    g_col = _dot_exact_rhs(logg, jnp.ones((C, LANES), F32), TN)
    yield
    q = q_ref[...] * (GLA_DK ** -0.5)
    k = k_ref[...]
    v = v_ref[...]
    s = s_scr[...]
    o_inter = _dot(q * jnp.exp(b), s)
    bl = b[C - 1:C]
    upd = _dot(k * jnp.exp(bl - b), v, TN)

    head_k = _iota((1, GLA_QK), 1) >> 6
    head_v = _iota((1, A_WIDTH), 1) >> 7
    ones_kv = keep_s.astype(BF16)
    ridx = _iota((SUB, 1), 0)

    def bd(x, head_of_lane):
        xb = x.astype(BF16)
        zero = jnp.zeros_like(xb)
        return jnp.concatenate([jnp.where(head_of_lane == h, xb, zero) for h in range(GLA_HEADS)], axis=0)

    starts = range(0, C, SUB)
    atts = {}
    for i0 in starts[1:]:
        m = b[i0 - 1:i0]
        qi, bi = q[i0:i0 + SUB], b[i0:i0 + SUB]
        atts[i0] = _dot(qi * jnp.exp(bi - m), bd(k[:i0] * jnp.exp(m - b[:i0]), head_k), NT)
    yield
    scs = {}
    for i0 in starts:
        qi, bi, ki = q[i0:i0 + SUB], b[i0:i0 + SUB], k[i0:i0 + SUB]
        cols = []
        for j in range(SUB):
            e = jnp.exp(jnp.minimum(bi - bi[j:j + 1], 0.0))
            cols.append(jnp.where(ridx >= j, qi * e * ki[j:j + 1], 0.0))
        scs[i0] = _dot(jnp.concatenate(cols, axis=0), ones_kv)
    offs = {i0: _dot(atts[i0], bd(v[:i0], head_v)) for i0 in starts[1:]}
    yield
    blocks = []
    for i0 in starts:
        vi = v[i0:i0 + SUB]
        oi = o_inter[i0:i0 + SUB]
        if i0 > 0:
            oi = oi + offs[i0]
        for j in range(SUB):
            oi = oi + scs[i0][j * SUB:(j + 1) * SUB] * vi[j:j + 1]
        blocks.append(oi)
    o = jnp.concatenate(blocks, axis=0)

    e_col = jnp.concatenate([jnp.exp(g_col)] * (A_WIDTH // LANES), axis=1)
    s_new = e_col * s + jnp.where(keep_s, upd, 0.0)
    s_scr[...] = s_new
    sout_ref[...] = (s_new[:, 0:GLA_DV] + s_new[:, GLA_DV:2 * GLA_DV]
                     + s_new[:, 2 * GLA_DV:3 * GLA_DV] + s_new[:, 3 * GLA_DV:4 * GLA_DV])

    outs = [_rms(o[:, h * GLA_DV:(h + 1) * GLA_DV], gn_ref[...]) for h in range(GLA_HEADS)]
    og = og_ref[...]
    o_ref[...] = jnp.concatenate(outs, axis=1) * (og * jax.nn.sigmoid(og))


def _gla(u, s0, w2p, bg, gn, B, T, C):
    nc = T // C
    NB = MIXER_BATCH
    s0 = s0.reshape(B, GLA_QK, GLA_DV)
    u = u.reshape(B, T, UW)
    o, s_fin = pl.pallas_call(
        functools.partial(_gla_kernel, C=C, NB=NB),
        grid=(B // NB, nc),
        in_specs=[
            pl.BlockSpec((NB, C, GLA_QK), lambda b, c: (b, c, 0)),
            pl.BlockSpec((NB, C, GLA_QK), lambda b, c: (b, c, 1)),
            pl.BlockSpec((NB, C, A_WIDTH), lambda b, c: (b, c, 1)),
            pl.BlockSpec((NB, C, LANES), lambda b, c: (b, c, 24)),
            pl.BlockSpec((NB, C, A_WIDTH), lambda b, c: (b, c, 2)),
            pl.BlockSpec((NB, GLA_QK, GLA_DV), lambda b, c: (b, 0, 0)),
            pl.BlockSpec((LANES, GLA_QK), lambda b, c: (0, 0)),
            pl.BlockSpec((1, GLA_QK), lambda b, c: (0, 0)),
            pl.BlockSpec((1, GLA_DV), lambda b, c: (0, 0)),
        ],
        out_specs=[
            pl.BlockSpec((NB, C, A_WIDTH), lambda b, c: (b, c, 0)),
            pl.BlockSpec((NB, GLA_QK, GLA_DV), lambda b, c: (b, 0, 0)),
        ],
        out_shape=[
            jax.ShapeDtypeStruct((B, T, A_WIDTH), F32),
            jax.ShapeDtypeStruct((B, GLA_QK, GLA_DV), F32),
        ],
        scratch_shapes=[pltpu.VMEM((NB, GLA_QK, A_WIDTH), F32)],
        compiler_params=_params("parallel", "arbitrary"),
        name="gla",
    )(u, u, u, u, u, s0, w2p, bg, gn)
    return o.reshape(B * T, A_WIDTH), s_fin.reshape(B, GLA_HEADS, GLA_DK, GLA_DV)


N_RWKV_PER_BATCH = (11, 2, 6)
N_RWKV_SHARED = 15


def _rwkv_kernel(*refs, C, NB):
    n_in, n_out, n_scr = N_RWKV_PER_BATCH
    ins, shared = refs[:n_in], refs[n_in:n_in + N_RWKV_SHARED]
    outs = refs[n_in + N_RWKV_SHARED:n_in + N_RWKV_SHARED + n_out]
    scr = refs[n_in + N_RWKV_SHARED + n_out:]
    GW = GROUP * RWKV_N
    keep_h = (_iota((GW, GW), 0) >> 6) == (_iota((GW, GW), 1) >> 6)

    @pl.when(pl.program_id(1) == 0)
    def _():
        h0_ref, h_scr = ins[10], scr[5]
        for bb in range(NB):
            for prev, sh in zip(scr[:5], ins[5:10]):
                prev[bb] = sh[bb]
            h0 = h0_ref[bb]
            for g in range(RWKV_HEADS // GROUP):
                t = jnp.concatenate([h0[:, g * GW:(g + 1) * GW]] * GROUP, axis=0)
                h_scr[bb, g] = jnp.where(keep_h, t, 0.0)

    at = lambda rs, bb: [r.at[bb] for r in rs]
    chains = [_rwkv_one(*at(ins[:5], bb), *shared, *at(outs, bb), *at(scr, bb), keep_h, C=C) for bb in range(NB)]
    for _ in _lockstep(chains):
        pass


def _rwkv_one(r_ref, k_ref, v_ref, wa_ref, gl_ref,
              mu_r, mu_k, mu_v, mu_wa, mu_gl, w0_ref, w2_ref, a0_ref, a2_ref, g2_ref,
              kk_ref, ka_ref, rk_ref, gng_ref, gnb_ref,
              y_ref, hout_ref,
              pr, pk, pv, pwa, pgl, h_scr, keep_h, *, C):
    GW = GROUP * RWKV_N
    CH = GROUP * C
    logc = C.bit_length() - 1
    n_groups = RWKV_HEADS // GROUP

    def head_mask(rows, cols, row_shift, col_of_lane):
        keep = (_iota((rows, cols), 0) >> row_shift) == col_of_lane(_iota((rows, cols), 1))
        return keep.astype(F32).astype(BF16)

    keep_rows = head_mask(CH, GW, logc, lambda l: l >> 6)
    keep_rows2 = head_mask(CH, 2 * GW, logc, lambda l: (l & (GW - 1)) >> 6)
    keep_sq = head_mask(CH, CH, logc, lambda l: l >> logc)

    def bd(x, keep):
        return jnp.concatenate([x.astype(BF16)] * GROUP, axis=0) * keep

    rid = _iota((C, 1), 0)

    def shift(x_ref, prev, mu):
        x = x_ref[...]
        xp = jnp.where(rid == 0, prev[...], pltpu.roll(x, 1, axis=0))
        prev[...] = x[C - 1:C]
        return x + mu[...] * (xp - x)

    rs = shift(r_ref, pr, mu_r)
    ks = shift(k_ref, pk, mu_k)
    vs = shift(v_ref, pv, mu_v)
    was = shift(wa_ref, pwa, mu_wa)
    gls = shift(gl_ref, pgl, mu_gl)

    dec = -_softplus(-(w0_ref[...] + _dot3(jnp.tanh(was), w2_ref[...]))) - 0.5
    logw = -jnp.exp(dec)
    a = jax.nn.sigmoid(a0_ref[...] + _dot(was, a2_ref[...]))
    gate = _dot(jax.nn.sigmoid(gls), g2_ref[...])
    yield
    head_ones = ((_iota((B_WIDTH, B_WIDTH), 0) >> 6) == (_iota((B_WIDTH, B_WIDTH), 1) >> 6)).astype(BF16)
    kkr = ks * kk_ref[...]
    kk = kkr / jnp.maximum(jnp.sqrt(_dot(kkr * kkr, head_ones)), 1e-12)
    km = ks * (1.0 + (a - 1.0) * ka_ref[...])
    beta = kk * a
    tril = _iota((C, C), 0) >= _iota((C, C), 1)
    gc = _dot_exact_lhs(tril, logw)
    g_end = gc[C - 1:C]
    kap = kk * jnp.exp(gc - logw)
    rt = rs * jnp.exp(gc)
    e_neg = jnp.exp(-gc)
    b_inv = beta * e_neg
    k_inv = km * e_neg
    e_end = jnp.exp(g_end - gc)
    k_hat = km * e_end
    b_hat = beta * e_end

    col_s = _iota((C, CH), 1) & (C - 1)
    row_t = _iota((C, CH), 0)
    strict = row_t > col_s
    incl = row_t >= col_s
    same_sub = (row_t >> 4) == (col_s >> 4)
    eye_cat = (row_t == col_s).astype(F32)
    ones_c = jnp.ones((C, LANES), F32)

    yield
    ys = [None] * n_groups

    def group(g):
        L = slice(g * GW, (g + 1) * GW)
        p = jnp.concatenate([kap[:, L], rt[:, L]], axis=0)
        zb = _dot(p, bd(b_inv[:, L], keep_rows), NT)
        zk = _dot(p, bd(k_inv[:, L], keep_rows), NT)
        yield
        a_b = jnp.where(strict, zb[:C], 0.0)
        b_b = jnp.where(incl, zb[C:], 0.0)
        a_k = jnp.where(strict, zk[:C], 0.0)
        b_k = jnp.where(incl, zk[C:], 0.0)
        bd_v = bd(vs[:, L], keep_rows)
        akv = _dot(a_k, bd_v)

        d_blk = jnp.where(same_sub, a_b, 0.0)
        x1 = -d_blk
        p1 = eye_cat + x1
        x2 = _dot(x1, bd(x1, keep_sq))
        yield
        rhs = jnp.concatenate([kap[:, L], akv], axis=1)
        r2 = _dot(jnp.concatenate([x2, p1], axis=0), bd(x2, keep_sq))
        yield
        x4, p2 = r2[:C], p1 + r2[C:]
        r4 = _dot(jnp.concatenate([x4, p2], axis=0), bd(x4, keep_sq))
        yield
        x8, p3 = r4[:C], p2 + r4[C:]
        t_d = p3 + _dot(p3, bd(x8, keep_sq))
        yield

        n1 = _dot(t_d, bd(a_b - d_blk, keep_sq))
        s1 = _dot(t_d, bd(rhs, keep_rows2))
        yield
        n2 = _dot(n1, bd(n1, keep_sq))
        yield
        s2 = s1 + _dot(n2, bd(s1, keep_rows2))
        yield
        sol = s2 - _dot(n1, bd(s2, keep_rows2))
        yield
        w_mat, u0 = sol[:, :GW], sol[:, GW:]

        hbd = h_scr[g]
        hb = hbd.astype(BF16)
        u = _dot(w_mat, hb) + u0
        yield
        y = _dot(rt[:, L], hb) + _dot(b_k, bd_v) - _dot(b_b, bd(u, keep_rows))
        m = _dot(k_hat[:, L], vs[:, L], TN) - _dot(b_hat[:, L], u, TN)
        g_col = _dot_exact_rhs(logw[:, L], ones_c, TN)[:, :1]
        yield
        h_new = jnp.exp(g_col) * hbd + jnp.where(keep_h, m, 0.0)
        h_scr[g] = h_new
        hout_ref[:, L] = (h_new[0:RWKV_N] + h_new[RWKV_N:2 * RWKV_N]
                          + h_new[2 * RWKV_N:3 * RWKV_N] + h_new[3 * RWKV_N:4 * RWKV_N])
        ys[g] = y

    yield from _lockstep([group(g) for g in range(n_groups)])
    y = jnp.concatenate(ys, axis=1)

    inv_n = 1.0 / RWKV_N
    mu = _dot(y, head_ones) * inv_n
    d = y - mu
    var = _dot(d * d, head_ones) * inv_n
    yn = d * lax.rsqrt(var + RWKV_GN_EPS) * gng_ref[...] + gnb_ref[...]
    bonus = _dot(rs * km * rk_ref[...], head_ones) * vs
    y_ref[...] = (yn + bonus) * gate


def _rwkv(u, shift0, h0, wts, B, T, C):
    nc = T // C
    NB = MIXER_BATCH
    wide = lambda blk: pl.BlockSpec((NB, C, B_WIDTH), lambda b, c: (b, c, blk))
    narrow = lambda blk: pl.BlockSpec((NB, C, LANES), lambda b, c: (b, c, blk))
    sh = lambda w, blk: pl.BlockSpec((NB, 1, w), lambda b, c: (b, 0, blk))
    const = lambda shape: pl.BlockSpec(shape, lambda b, c: (0,) * len(shape))
    assert len(wts) == N_RWKV_SHARED
    in_specs = [wide(3), wide(4), wide(5), narrow(25), narrow(26),
                sh(B_WIDTH, 0), sh(B_WIDTH, 1), sh(B_WIDTH, 2), sh(LANES, 12), sh(LANES, 13),
                pl.BlockSpec((NB, RWKV_N, B_WIDTH), lambda b, c: (b, 0, 0))]
    in_specs += [const(w.shape) for w in wts]
    GW = GROUP * RWKV_N
    u = u.reshape(B, T, UW)
    y, h_fin = pl.pallas_call(
        functools.partial(_rwkv_kernel, C=C, NB=NB),
        grid=(B // NB, nc),
        in_specs=in_specs,
        out_specs=[
            pl.BlockSpec((NB, C, B_WIDTH), lambda b, c: (b, c, 0)),
            pl.BlockSpec((NB, RWKV_N, B_WIDTH), lambda b, c: (b, 0, 0)),
        ],
        out_shape=[
            jax.ShapeDtypeStruct((B, T, B_WIDTH), F32),
            jax.ShapeDtypeStruct((B, RWKV_N, B_WIDTH), F32),
        ],
        scratch_shapes=[
            pltpu.VMEM((NB, 1, B_WIDTH), F32), pltpu.VMEM((NB, 1, B_WIDTH), F32), pltpu.VMEM((NB, 1, B_WIDTH), F32),
            pltpu.VMEM((NB, 1, LANES), F32), pltpu.VMEM((NB, 1, LANES), F32),
            pltpu.VMEM((NB, RWKV_HEADS // GROUP, GW, GW), F32),
        ],
        compiler_params=_params("parallel", "arbitrary"),
        name="rwkv7",
    )(u, u, u, u, u, shift0, shift0, shift0, shift0, shift0, h0, *wts)
    return y.reshape(B * T, B_WIDTH), h_fin


def _ret_kernel(q_ref, k_ref, v_ref, g_ref, cos_ref, sin_ref, r0_ref, gng_ref, gnb_ref,
                o_ref, rout_ref, r_scr, *, C):
    c = pl.program_id(1)

    @pl.when(c == 0)
    def _():
        r_scr[...] = r0_ref[0]

    cos = cos_ref[...]
    sin = sin_ref[...]
    half = RET_DK // 2

    def rope(x):
        x1, x2 = x[:, :half], x[:, half:]
        return jnp.concatenate([x1 * cos - x2 * sin, x1 * sin + x2 * cos], axis=1)

    ri = _iota((C, C), 0)
    ci = _iota((C, C), 1)
    causal = ri >= ci
    diff = (ri - ci).astype(F32)
    pos1 = (_iota((C, 1), 0) + 1).astype(F32)
    outs = []
    for h in range(RET_HEADS):
        lg = math.log1p(-(2.0 ** (-5.0 - h)))
        q = rope(q_ref[:, h * RET_DK:(h + 1) * RET_DK])
        k = rope(k_ref[:, h * RET_DK:(h + 1) * RET_DK]) * (RET_DK ** -0.5)
        v = v_ref[:, h * RET_DV:(h + 1) * RET_DV]
        r = r_scr[h]
        decay = jnp.where(causal, jnp.exp(diff * lg), 0.0)
        att = _dot(q, k, NT) * decay
        o = _dot(att, v) + _dot(q, r) * jnp.exp(pos1 * lg)
        r_scr[h] = math.exp(C * lg) * r + _dot(k * jnp.exp((C - pos1) * lg), v, TN)
        mu = jnp.mean(o, axis=-1, keepdims=True)
        d = o - mu
        var = jnp.mean(d * d, axis=-1, keepdims=True)
        outs.append(d * lax.rsqrt(var + RET_GN_EPS))
    g = g_ref[...]
    o = jnp.concatenate(outs, axis=1) * gng_ref[...] + gnb_ref[...]
    o_ref[...] = (g * jax.nn.sigmoid(g)) * o
    rout_ref[0] = r_scr[...]


def _ret(u, cos, sin, r0, gn_g, gn_b, B, T, C):
    nc = T // C
    row = lambda b, c: b * nc + c
    return pl.pallas_call(
        functools.partial(_ret_kernel, C=C),
        grid=(B, nc),
        in_specs=[
            pl.BlockSpec((C, RET_QK), lambda b, c: (row(b, c), 0)),
            pl.BlockSpec((C, RET_QK), lambda b, c: (row(b, c), 1)),
            pl.BlockSpec((C, RET_V), lambda b, c: (row(b, c), 1)),
            pl.BlockSpec((C, RET_V), lambda b, c: (row(b, c), 2)),
            pl.BlockSpec((C, RET_DK // 2), lambda b, c: (c, 0)),
            pl.BlockSpec((C, RET_DK // 2), lambda b, c: (c, 0)),
            pl.BlockSpec((1, RET_HEADS, RET_DK, RET_DV), lambda b, c: (b, 0, 0, 0)),
            pl.BlockSpec((1, RET_V), lambda b, c: (0, 0)),
            pl.BlockSpec((1, RET_V), lambda b, c: (0, 0)),
        ],
        out_specs=[
            pl.BlockSpec((C, RET_V), lambda b, c: (row(b, c), 0)),
            pl.BlockSpec((1, RET_HEADS, RET_DK, RET_DV), lambda b, c: (b, 0, 0, 0)),
        ],
        out_shape=[
            jax.ShapeDtypeStruct((B * T, RET_V), F32),
            jax.ShapeDtypeStruct((B, RET_HEADS, RET_DK, RET_DV), F32),
        ],
        scratch_shapes=[pltpu.VMEM((RET_HEADS, RET_DK, RET_DV), F32)],
        compiler_params=_params("parallel", "arbitrary"),
        name="retention",
    )(u, u, u, u, cos, sin, r0, gn_g, gn_b)


def _router_kernel(x_ref, g_ref, wr_ref, ids_ref, gsel_ref):
    h = _rms(x_ref[...], g_ref[...])
    logits = _dot3(h, wr_ref[...])
    lane = _iota(logits.shape, 1)
    valid = lane < N_EXPERTS
    logits = jnp.where(valid, logits, -jnp.inf)
    m = jnp.max(logits, axis=-1, keepdims=True)
    e = jnp.where(valid, jnp.exp(logits - m), 0.0)
    p = e / jnp.sum(e, axis=-1, keepdims=True)
    big = jnp.int32(LANES)
    p1 = jnp.max(p, axis=-1, keepdims=True)
    i1 = jnp.min(jnp.where(p == p1, lane, big), axis=-1, keepdims=True)
    rest = jnp.where((lane == i1) | ~valid, -1.0, p)
    p2 = jnp.max(rest, axis=-1, keepdims=True)
    i2 = jnp.min(jnp.where(rest == p2, lane, big), axis=-1, keepdims=True)
    tot = p1 + p2
    ids_ref[...] = jnp.where(lane == 0, i1, i2)
    gsel_ref[...] = jnp.where(lane == 0, p1 / tot, p2 / tot)


def _router(x, g, wr):
    n, d = x.shape
    tm = min(n, 512)
    return pl.pallas_call(
        _router_kernel,
        grid=(n // tm,),
        in_specs=[
            pl.BlockSpec((tm, d), lambda i: (i, 0)),
            pl.BlockSpec((1, d), lambda i: (0, 0)),
            pl.BlockSpec((d, LANES), lambda i: (0, 0)),
        ],
        out_specs=[pl.BlockSpec((tm, LANES), lambda i: (i, 0)), pl.BlockSpec((tm, LANES), lambda i: (i, 0))],
        out_shape=[jax.ShapeDtypeStruct((n, LANES), jnp.int32), jax.ShapeDtypeStruct((n, LANES), F32)],
        compiler_params=_params("parallel"),
        name="router",
    )(x, g, wr)


def _gather_kernel(idx_ref, src_ref, dst_ref, sems, *, TG):
    t = pl.program_id(0)
    slot = t % 2
    base = t * TG

    def row_copy(src_row, dst_row, sem):
        return pltpu.make_async_copy(src_ref.at[pl.ds(src_row, 1)], dst_ref.at[pl.ds(dst_row, 1)], sem)

    def issue(r, carry):
        row_copy(idx_ref[0, 0, r], base + r, sems.at[slot]).start()
        return carry

    lax.fori_loop(0, TG, issue, 0, unroll=8)

    def drain(step_base, sem):
        def wait_one(r, carry):
            row_copy(0, step_base + r, sem).wait()
            return carry

        lax.fori_loop(0, TG, wait_one, 0, unroll=8)

    @pl.when(t > 0)
    def _():
        drain(base - TG, sems.at[1 - slot])

    @pl.when(t == pl.num_programs(0) - 1)
    def _():
        drain(base, sems.at[slot])


def _gather_rows(src, idx):
    r, d = idx.shape[0], src.shape[1]
    tg = min(r, GATHER_ROWS)
    nt = r // tg
    return pl.pallas_call(
        functools.partial(_gather_kernel, TG=tg),
        grid=(nt,),
        in_specs=[pl.BlockSpec((1, 1, tg), lambda t: (t, 0, 0), memory_space=pltpu.SMEM),
                  pl.BlockSpec(memory_space=pl.ANY)],
        out_specs=pl.BlockSpec(memory_space=pl.ANY),
        out_shape=jax.ShapeDtypeStruct((r, d), src.dtype),
        scratch_shapes=[pltpu.SemaphoreType.DMA((2,))],
        compiler_params=pltpu.CompilerParams(dimension_semantics=("arbitrary",)),
        name="gather_rows",
    )(idx.reshape(nt, 1, tg), src)


def _moe_ffn_kernel(te_ref, nu_ref, x_ref, g_ref, gate_ref, w1_ref, w3_ref, w2_ref, o_ref, acc, *, tf):
    t = pl.program_id(0)

    @pl.when(t < nu_ref[0])
    def _():
        h = _rms(x_ref[...], g_ref[...]).astype(BF16)
        for i, f0 in enumerate(range(0, D_FF, tf)):
            a1 = jnp.dot(h, w1_ref[0, :, f0:f0 + tf], preferred_element_type=F32)
            a3 = jnp.dot(h, w3_ref[0, :, f0:f0 + tf], preferred_element_type=F32)
            act = (a1 * jax.nn.sigmoid(a1) * a3).astype(BF16)
            part = jnp.dot(act, w2_ref[0, f0:f0 + tf, :], preferred_element_type=F32)
            if i == 0:
                acc[...] = part
            else:
                acc[...] += part
        o_ref[...] = acc[...] * gate_ref[...]

    @pl.when(t >= nu_ref[0])
    def _():
        o_ref[...] = jnp.zeros_like(o_ref)


def _moe_ffn(xs, g, row_gate, tile_expert, n_used, w1, w3, w2, tm):
    p, d = xs.shape
    dff = w1.shape[2]
    grid_spec = pltpu.PrefetchScalarGridSpec(
        num_scalar_prefetch=2,
        grid=(p // tm,),
        in_specs=[
            pl.BlockSpec((tm, d), lambda t, te, nu: (t, 0)),
            pl.BlockSpec((1, d), lambda t, te, nu: (0, 0)),
            pl.BlockSpec((tm, 1), lambda t, te, nu: (t, 0)),
            pl.BlockSpec((1, d, dff), lambda t, te, nu: (te[t], 0, 0)),
            pl.BlockSpec((1, d, dff), lambda t, te, nu: (te[t], 0, 0)),
            pl.BlockSpec((1, dff, d), lambda t, te, nu: (te[t], 0, 0)),
        ],
        out_specs=pl.BlockSpec((tm, d), lambda t, te, nu: (t, 0)),
        scratch_shapes=[pltpu.VMEM((tm, d), F32)],
    )
    return pl.pallas_call(
        functools.partial(_moe_ffn_kernel, tf=256),
        grid_spec=grid_spec,
        out_shape=jax.ShapeDtypeStruct((p, d), F32),
        compiler_params=_params("arbitrary"),
        name="moe_ffn",
    )(tile_expert, n_used, xs, g, row_gate, w1, w3, w2)


def _combine_kernel(x_ref, y1_ref, y2_ref, g_ref, o_ref):
    o_ref[...] = _rms(x_ref[...] + y1_ref[...] + y2_ref[...], g_ref[...])


def _combine(x, yc, g):
    n, d = x.shape
    tm = min(n, 1024)
    nb = n // tm
    return pl.pallas_call(
        _combine_kernel,
        grid=(nb,),
        in_specs=[
            pl.BlockSpec((tm, d), lambda i: (i, 0)),
            pl.BlockSpec((tm, d), lambda i: (i, 0)),
            pl.BlockSpec((tm, d), lambda i: (i + nb, 0)),
            pl.BlockSpec((1, d), lambda i: (0, 0)),
        ],
        out_specs=pl.BlockSpec((tm, d), lambda i: (i, 0)),
        out_shape=jax.ShapeDtypeStruct((n, d), F32),
        compiler_params=_params("parallel"),
        name="combine",
    )(x, yc, yc, g)


def _moe_plan(ids, gsel, tm):
    n = ids.shape[0]
    e_flat = ids.T.reshape(-1)
    g_flat = gsel.T.reshape(-1)
    tok_flat = jnp.tile(jnp.arange(n, dtype=jnp.int32), 2)
    onehot = (e_flat[:, None] == jnp.arange(N_EXPERTS, dtype=jnp.int32)[None, :]).astype(jnp.int32)
    csum = jnp.cumsum(onehot, axis=0)
    rank = jnp.sum(csum * onehot, axis=1) - 1
    counts = csum[-1]
    padded = ((counts + tm - 1) // tm) * tm
    ends = jnp.cumsum(padded)
    pos = (ends - padded)[e_flat] + rank
    p = 2 * n + N_EXPERTS * tm
    row_token = jnp.zeros((p,), jnp.int32).at[pos].set(tok_flat)
    row_gate = jnp.zeros((p,), F32).at[pos].set(g_flat)
    n_used = (ends[-1] // tm).astype(jnp.int32)
    tile_start = jnp.arange(p // tm, dtype=jnp.int32) * tm
    tile_expert = jnp.searchsorted(ends, jnp.minimum(tile_start, ends[-1] - 1), side="right").astype(jnp.int32)
    return row_token, row_gate.reshape(p, 1), tile_expert, n_used.reshape(1), pos.astype(jnp.int32)


def _ffn_kernel(x_ref, g_ref, w1_ref, w3_ref, w2_ref, o_ref, h_scr, acc):
    f = pl.program_id(1)

    @pl.when(f == 0)
    def _():
        h_scr[...] = _rms(x_ref[...], g_ref[...]).astype(BF16)
        acc[...] = x_ref[...]

    h = h_scr[...]
    a1 = jnp.dot(h, w1_ref[...], preferred_element_type=F32)
    a3 = jnp.dot(h, w3_ref[...], preferred_element_type=F32)
    act = a1 * jax.nn.sigmoid(a1) * a3
    acc[...] += jnp.dot(act.astype(BF16), w2_ref[...], preferred_element_type=F32)

    @pl.when(f == pl.num_programs(1) - 1)
    def _():
        o_ref[...] = acc[...]


def _ffn(x, g, w1, w3, w2):
    n, d = x.shape
    dff = w1.shape[1]
    tm = min(n, 1024)
    tf = 256
    return pl.pallas_call(
        _ffn_kernel,
        grid=(n // tm, dff // tf),
        in_specs=[pl.BlockSpec((tm, d), lambda i, f: (i, 0)),
                  pl.BlockSpec((1, d), lambda i, f: (0, 0)),
                  pl.BlockSpec((d, tf), lambda i, f: (0, f)),
                  pl.BlockSpec((d, tf), lambda i, f: (0, f)),
                  pl.BlockSpec((tf, d), lambda i, f: (f, 0))],
        out_specs=pl.BlockSpec((tm, d), lambda i, f: (i, 0)),
        out_shape=jax.ShapeDtypeStruct((n, d), F32),
        scratch_shapes=[pltpu.VMEM((tm, d), BF16), pltpu.VMEM((tm, d), F32)],
        compiler_params=_params("parallel", "arbitrary"),
        name="ffn",
    )(x, g, w1, w3, w2)


def _prep_weights(p):
    w_in = p["ab_w_in"][0]
    wa, wb = w_in[:, :A_COLS], w_in[:, A_COLS:]
    pad = jnp.zeros((D_MODEL, LANES - GLA_GATE_RANK), F32)
    w_in_p = jnp.concatenate(
        [wa[:, :2 * GLA_QK + A_WIDTH], wa[:, 2 * GLA_QK + A_WIDTH + GLA_GATE_RANK:], wb[:, :3 * B_WIDTH],
         wa[:, 2 * GLA_QK + A_WIDTH:2 * GLA_QK + A_WIDTH + GLA_GATE_RANK], pad, wb[:, 3 * B_WIDTH:]], axis=1)
    row = lambda v: v.reshape(1, -1).astype(F32)
    mu = p["rwkv_mu"][0]
    z64 = jnp.zeros((64, B_WIDTH), F32)
    w = dict(
        norm_mix=p["norm_mix"], norm_ffn=p["norm_ffn"], norm_final=row(p["norm_final"]),
        w_in=w_in_p.astype(BF16),
        w_out_a=p["ab_w_out"][0][:A_WIDTH].astype(BF16), w_out_b=p["ab_w_out"][0][A_WIDTH:].astype(BF16),
        gla_w2=jnp.concatenate([p["gla_w_gate2"][0], jnp.zeros((LANES - GLA_GATE_RANK, GLA_QK), F32)], axis=0),
        gla_bg=row(p["gla_b_gate"][0]), gla_norm=row(p["gla_norm"][0]),
        rwkv=(row(mu[:B_WIDTH]), row(mu[B_WIDTH:2 * B_WIDTH]), row(mu[2 * B_WIDTH:3 * B_WIDTH]),
              row(mu[3 * B_WIDTH:3 * B_WIDTH + LANES]), row(mu[3 * B_WIDTH + LANES:]),
              row(p["rwkv_w0"][0]), jnp.concatenate([p["rwkv_w2"][0], z64], axis=0),
              row(p["rwkv_a0"][0]), jnp.concatenate([z64, p["rwkv_a2"][0]], axis=0),
              p["rwkv_g2"][0], row(p["rwkv_k_k"][0]), row(p["rwkv_k_a"][0]), row(p["rwkv_r_k"][0]),
              row(p["rwkv_gn_g"][0]), row(p["rwkv_gn_b"][0])),
        ret_w_in=p["ret_w_in"][0].astype(BF16), ret_w_out=p["ret_w_out"][0].astype(BF16),
        ret_gn_g=row(p["ret_gn_g"][0]), ret_gn_b=row(p["ret_gn_b"][0]),
        ffn_w1=p["ffn_w1"][0].astype(BF16), ffn_w3=p["ffn_w3"][0].astype(BF16), ffn_w2=p["ffn_w2"][0].astype(BF16),
        router=jnp.concatenate([p["moe_router"][0], jnp.zeros((D_MODEL, LANES - N_EXPERTS), F32)], axis=1),
        moe_w1=p["moe_w1"][0].astype(BF16), moe_w3=p["moe_w3"][0].astype(BF16), moe_w2=p["moe_w2"][0].astype(BF16),
    )
    return w


def _trunk(x, pos0, st_gla, st_rwkv, st_shift, st_ret, w):
    B, T, D = x.shape
    n = B * T
    x0 = x.reshape(n, D)
    c_ab = min(CHUNK, T)
    c_ret = min(256, T)

    u = _norm_matmul(x0, w["norm_mix"][0:1], w["w_in"], tn=UW // 3)
    oa, s_gla = _gla(u, st_gla, w["gla_w2"], w["gla_bg"], w["gla_norm"], B, T, c_ab)
    h0 = st_rwkv.transpose(0, 3, 1, 2).reshape(B, RWKV_N, B_WIDTH)
    yb, h_fin = _rwkv(u, st_shift, h0, w["rwkv"], B, T, c_ab)
    s_rwkv = h_fin.reshape(B, RWKV_N, RWKV_HEADS, RWKV_N).transpose(0, 2, 3, 1)
    u_last = u.reshape(B, T, UW)[:, T - 1:, :]
    s_shift = jnp.concatenate([u_last[..., 3 * A_WIDTH:6 * A_WIDTH], u_last[..., UW - 2 * LANES:]], axis=-1)
    x1 = _proj_res([oa, yb], [w["w_out_a"], w["w_out_b"]], x0)
    x2 = _ffn(x1, w["norm_ffn"][0:1], w["ffn_w1"], w["ffn_w3"], w["ffn_w2"])

    ur = _norm_matmul(x2, w["norm_mix"][1:2], w["ret_w_in"], tn=1536)
    half = RET_DK // 2
    inv = 1.0 / (ROPE_BASE ** (jnp.arange(0, RET_DK, 2, dtype=F32) / RET_DK))
    ang = (pos0 + jnp.arange(T)).astype(F32)[:, None] * inv[None, :]
    o, s_ret = _ret(ur, jnp.cos(ang), jnp.sin(ang), st_ret, w["ret_gn_g"], w["ret_gn_b"], B, T, c_ret)
    x3 = _proj_res([o], [w["ret_w_out"]], x2)
    ids, gsel = _router(x3, w["norm_ffn"][1:2], w["router"])
    tm = min(MOE_TILE, n // 4)
    row_token, row_gate, tile_expert, n_used, pos = _moe_plan(ids[:, :2], gsel[:, :2], tm)
    xs = _gather_rows(x3, row_token)
    ys = _moe_ffn(xs, w["norm_ffn"][1:2], row_gate, tile_expert, n_used, w["moe_w1"], w["moe_w3"], w["moe_w2"], tm)
    y = _combine(x3, _gather_rows(ys, pos), w["norm_final"])
    return y.reshape(B, T, D), s_gla[None], s_rwkv[None], s_shift[None], s_ret[None]


def kernel(x_prompt, x_sample, state_gla, state_rwkv, state_shift, state_ret, norm_mix, norm_ffn, norm_final, ab_w_in, ab_w_out, gla_w_gate2, gla_b_gate, gla_norm, rwkv_mu, rwkv_w0, rwkv_w2, rwkv_a0, rwkv_a2, rwkv_g2, rwkv_k_k, rwkv_k_a, rwkv_r_k, rwkv_gn_g, rwkv_gn_b, ret_w_in, ret_gn_g, ret_gn_b, ret_w_out, ffn_w1, ffn_w3, ffn_w2, moe_router, moe_w1, moe_w3, moe_w2):
    p = dict(norm_mix=norm_mix, norm_ffn=norm_ffn, norm_final=norm_final, ab_w_in=ab_w_in, ab_w_out=ab_w_out,
             gla_w_gate2=gla_w_gate2, gla_b_gate=gla_b_gate, gla_norm=gla_norm, rwkv_mu=rwkv_mu, rwkv_w0=rwkv_w0,
             rwkv_w2=rwkv_w2, rwkv_a0=rwkv_a0, rwkv_a2=rwkv_a2, rwkv_g2=rwkv_g2, rwkv_k_k=rwkv_k_k,
             rwkv_k_a=rwkv_k_a, rwkv_r_k=rwkv_r_k, rwkv_gn_g=rwkv_gn_g, rwkv_gn_b=rwkv_gn_b, ret_w_in=ret_w_in,
             ret_gn_g=ret_gn_g, ret_gn_b=ret_gn_b, ret_w_out=ret_w_out, ffn_w1=ffn_w1, ffn_w3=ffn_w3, ffn_w2=ffn_w2,
             moe_router=moe_router, moe_w1=moe_w1, moe_w3=moe_w3, moe_w2=moe_w2)
    w = _prep_weights(p)
    bp, tp = x_prompt.shape[0], x_prompt.shape[1]
    dt = x_prompt.dtype
    z_gla = jnp.zeros((bp, GLA_HEADS, GLA_DK, GLA_DV), dt)
    z_rwkv = jnp.zeros((bp, RWKV_HEADS, RWKV_N, RWKV_N), dt)
    z_shift = jnp.zeros((bp, 1, B_COLS), dt)
    z_ret = jnp.zeros((bp, RET_HEADS, RET_DK, RET_DV), dt)
    past = 2048
    yp = _trunk(x_prompt, 0, z_gla, z_rwkv, z_shift, z_ret, w)
    ys = _trunk(x_sample, past, state_gla[0], state_rwkv[0], state_shift[0], state_ret[0], w)
    return (yp[0], ys[0], yp[1], yp[2], yp[3], yp[4], ys[1], ys[2], ys[3], ys[4])
```

```python
import functools
import math

import jax
import jax.numpy as jnp
from jax import lax
from jax.experimental import pallas as pl
from jax.experimental.pallas import tpu as pltpu

F32 = jnp.float32
BF16 = jnp.bfloat16

D_MODEL = 1024
EPS = 1e-6
GLA_HEADS, GLA_DK, GLA_DV = 4, 64, 128
GLA_QK = GLA_HEADS * GLA_DK
A_WIDTH = GLA_HEADS * GLA_DV
GLA_GATE_RANK = 16
GLA_TAU = 16.0
RWKV_HEADS, RWKV_N = 8, 64
B_WIDTH = RWKV_HEADS * RWKV_N
RWKV_GN_EPS = 64e-5
B_COLS = 3 * B_WIDTH + 64 + 64 + 128
A_COLS = 2 * GLA_QK + A_WIDTH + GLA_GATE_RANK + A_WIDTH
RET_HEADS = 4
RET_DK = D_MODEL // RET_HEADS
RET_DV = 2 * D_MODEL // RET_HEADS
RET_QK = RET_HEADS * RET_DK
RET_V = RET_HEADS * RET_DV
RET_GN_EPS = 1e-5
ROPE_BASE = 10000.0
D_FF = 2816
N_EXPERTS = 8
CHUNK = 64
SUB = 16
LANES = 128
GROUP = 4
MIXER_BATCH = 2
MOE_TILE = 512
DISPATCH_ROWS = 512
UW = 3456
VMEM_LIMIT = 56 * 1024 * 1024

NN = (((1,), (0,)), ((), ()))
NT = (((1,), (1,)), ((), ()))
TN = (((0,), (0,)), ((), ()))


def _dot(a, b, dims=NN):
    return lax.dot_general(a.astype(BF16), b.astype(BF16), dims, preferred_element_type=F32)


def _split(x):
    hi = x.astype(BF16)
    lo = (x - hi.astype(F32)).astype(BF16)
    return hi, lo


def _dot3(a, b, dims=NN):
    ah, al = _split(a)
    bh, bl = _split(b)
    dg = lambda x, y: lax.dot_general(x, y, dims, preferred_element_type=F32)
    return dg(ah, bh) + dg(ah, bl) + dg(al, bh)


def _dot_exact_lhs(m, x, dims=NN):
    xh, xl = _split(x)
    mb = m.astype(BF16)
    dg = lambda y: lax.dot_general(mb, y, dims, preferred_element_type=F32)
    return dg(xh) + dg(xl)


def _dot_exact_rhs(x, m, dims=NN):
    xh, xl = _split(x)
    mb = m.astype(BF16)
    dg = lambda y: lax.dot_general(y, mb, dims, preferred_element_type=F32)
    return dg(xh) + dg(xl)


def _iota(shape, axis):
    return lax.broadcasted_iota(jnp.int32, shape, axis)


def _softplus(z):
    return jnp.maximum(z, 0.0) + jnp.log1p(jnp.exp(-jnp.abs(z)))


def _rms(x, g):
    return x * lax.rsqrt(jnp.mean(x * x, axis=-1, keepdims=True) + EPS) * g


def _lockstep(chains):
    chains = list(chains)
    while chains:
        alive = []
        for ch in chains:
            try:
                next(ch)
                alive.append(ch)
            except StopIteration:
                pass
        chains = alive
        yield


def _params(*sem):
    return pltpu.CompilerParams(dimension_semantics=sem, vmem_limit_bytes=VMEM_LIMIT)


def _norm_matmul_kernel(x_ref, g_ref, w_ref, o_ref, h_ref):
    @pl.when(pl.program_id(1) == 0)
    def _():
        h_ref[...] = _rms(x_ref[...], g_ref[...]).astype(BF16)

    o_ref[...] = jnp.dot(h_ref[...], w_ref[...], preferred_element_type=F32)


def _norm_matmul(x, g, w, tn):
    n, d = x.shape
    nout = w.shape[1]
    tm = min(n, 1024)
    return pl.pallas_call(
        _norm_matmul_kernel,
        grid=(n // tm, nout // tn),
        in_specs=[
            pl.BlockSpec((tm, d), lambda i, j: (i, 0)),
            pl.BlockSpec((1, d), lambda i, j: (0, 0)),
            pl.BlockSpec((d, tn), lambda i, j: (0, j)),
        ],
        out_specs=pl.BlockSpec((tm, tn), lambda i, j: (i, j)),
        out_shape=jax.ShapeDtypeStruct((n, nout), F32),
        scratch_shapes=[pltpu.VMEM((tm, d), BF16)],
        compiler_params=_params("parallel", "arbitrary"),
        name="norm_matmul",
    )(x, g, w)


def _proj_res_kernel(*refs, n_in):
    a_refs = refs[:n_in]
    w_refs = refs[n_in:2 * n_in]
    res_ref, o_ref = refs[2 * n_in], refs[2 * n_in + 1]
    acc = res_ref[...]
    for a_ref, w_ref in zip(a_refs, w_refs):
        acc = acc + jnp.dot(a_ref[...].astype(BF16), w_ref[...], preferred_element_type=F32)
    o_ref[...] = acc


def _proj_res(a_list, w_list, res):
    n, d = res.shape
    tm = min(n, 512)
    n_in = len(a_list)
    in_specs = [pl.BlockSpec((tm, a.shape[1]), lambda i: (i, 0)) for a in a_list]
    in_specs += [pl.BlockSpec(w.shape, lambda i: (0, 0)) for w in w_list]
    in_specs += [pl.BlockSpec((tm, d), lambda i: (i, 0))]
    return pl.pallas_call(
        functools.partial(_proj_res_kernel, n_in=n_in),
        grid=(n // tm,),
        in_specs=in_specs,
        out_specs=pl.BlockSpec((tm, d), lambda i: (i, 0)),
        out_shape=jax.ShapeDtypeStruct((n, d), F32),
        compiler_params=_params("parallel"),
        name="proj_res",
    )(*a_list, *w_list, res)


def _gla_kernel(q_ref, k_ref, v_ref, ga_ref, og_ref, s0_ref, w2_ref, bg_ref, gn_ref,
                o_ref, sout_ref, s_scr, *, C, NB):
    keep_s = (_iota((GLA_QK, A_WIDTH), 0) >> 6) == (_iota((GLA_QK, A_WIDTH), 1) >> 7)

    @pl.when(pl.program_id(1) == 0)
    def _():
        for bb in range(NB):
            s_scr[bb] = jnp.where(keep_s, jnp.concatenate([s0_ref[bb]] * GLA_HEADS, axis=1), 0.0)

    chains = [_gla_one(q_ref.at[bb], k_ref.at[bb], v_ref.at[bb], ga_ref.at[bb], og_ref.at[bb],
                       w2_ref, bg_ref, gn_ref, o_ref.at[bb], sout_ref.at[bb], s_scr.at[bb], keep_s, C=C)
              for bb in range(NB)]
    for _ in _lockstep(chains):
        pass


def _gla_one(q_ref, k_ref, v_ref, ga_ref, og_ref, w2_ref, bg_ref, gn_ref,
             o_ref, sout_ref, s_scr, keep_s, *, C):
    x = _dot3(ga_ref[...], w2_ref[...]) + bg_ref[...]
    yield
    logg = (jnp.minimum(x, 0.0) - jnp.log1p(jnp.exp(-jnp.abs(x)))) * (1.0 / GLA_TAU)
    tril = _iota((C, C), 0) >= _iota((C, C), 1)
    b = _dot_exact_lhs(tril, logg)
    g_col = _dot_exact_rhs(logg, jnp.ones((C, LANES), F32), TN)
    yield
    q = q_ref[...] * (GLA_DK ** -0.5)
    k = k_ref[...]
    v = v_ref[...]
    s = s_scr[...]
    o_inter = _dot(q * jnp.exp(b), s)
    bl = b[C - 1:C]
    upd = _dot(k * jnp.exp(bl - b), v, TN)

    head_k = _iota((1, GLA_QK), 1) >> 6
    head_v = _iota((1, A_WIDTH), 1) >> 7
    ones_kv = keep_s.astype(BF16)
    ridx = _iota((SUB, 1), 0)

    def bd(x, head_of_lane):
        xb = x.astype(BF16)
        zero = jnp.zeros_like(xb)
        return jnp.concatenate([jnp.where(head_of_lane == h, xb, zero) for h in range(GLA_HEADS)], axis=0)

    starts = range(0, C, SUB)
    atts = {}
    for i0 in starts[1:]:
        m = b[i0 - 1:i0]
        qi, bi = q[i0:i0 + SUB], b[i0:i0 + SUB]
        atts[i0] = _dot(qi * jnp.exp(bi - m), bd(k[:i0] * jnp.exp(m - b[:i0]), head_k), NT)
    yield
    scs = {}
    for i0 in starts:
        qi, bi, ki = q[i0:i0 + SUB], b[i0:i0 + SUB], k[i0:i0 + SUB]
        cols = []
        for j in range(SUB):
            e = jnp.exp(jnp.minimum(bi - bi[j:j + 1], 0.0))
            cols.append(jnp.where(ridx >= j, qi * e * ki[j:j + 1], 0.0))
        scs[i0] = _dot(jnp.concatenate(cols, axis=0), ones_kv)
    offs = {i0: _dot(atts[i0], bd(v[:i0], head_v)) for i0 in starts[1:]}
    yield
    blocks = []
    for i0 in starts:
        vi = v[i0:i0 + SUB]
        oi = o_inter[i0:i0 + SUB]
        if i0 > 0:
            oi = oi + offs[i0]
        for j in range(SUB):
            oi = oi + scs[i0][j * SUB:(j + 1) * SUB] * vi[j:j + 1]
        blocks.append(oi)
    o = jnp.concatenate(blocks, axis=0)

    e_col = jnp.concatenate([jnp.exp(g_col)] * (A_WIDTH // LANES), axis=1)
    s_new = e_col * s + jnp.where(keep_s, upd, 0.0)
    s_scr[...] = s_new
    sout_ref[...] = (s_new[:, 0:GLA_DV] + s_new[:, GLA_DV:2 * GLA_DV]
                     + s_new[:, 2 * GLA_DV:3 * GLA_DV] + s_new[:, 3 * GLA_DV:4 * GLA_DV])

    outs = [_rms(o[:, h * GLA_DV:(h + 1) * GLA_DV], gn_ref[...]) for h in range(GLA_HEADS)]
    og = og_ref[...]
    o_ref[...] = jnp.concatenate(outs, axis=1) * (og * jax.nn.sigmoid(og))


def _gla(u, s0, w2p, bg, gn, B, T, C):
    nc = T // C
    NB = MIXER_BATCH
    s0 = s0.reshape(B, GLA_QK, GLA_DV)
    u = u.reshape(B, T, UW)
    o, s_fin = pl.pallas_call(
        functools.partial(_gla_kernel, C=C, NB=NB),
        grid=(B // NB, nc),
        in_specs=[
            pl.BlockSpec((NB, C, GLA_QK), lambda b, c: (b, c, 0)),
            pl.BlockSpec((NB, C, GLA_QK), lambda b, c: (b, c, 1)),
            pl.BlockSpec((NB, C, A_WIDTH), lambda b, c: (b, c, 1)),
            pl.BlockSpec((NB, C, LANES), lambda b, c: (b, c, 24)),
            pl.BlockSpec((NB, C, A_WIDTH), lambda b, c: (b, c, 2)),
            pl.BlockSpec((NB, GLA_QK, GLA_DV), lambda b, c: (b, 0, 0)),
            pl.BlockSpec((LANES, GLA_QK), lambda b, c: (0, 0)),
            pl.BlockSpec((1, GLA_QK), lambda b, c: (0, 0)),
            pl.BlockSpec((1, GLA_DV), lambda b, c: (0, 0)),
        ],
        out_specs=[
            pl.BlockSpec((NB, C, A_WIDTH), lambda b, c: (b, c, 0)),
            pl.BlockSpec((NB, GLA_QK, GLA_DV), lambda b, c: (b, 0, 0)),
        ],
        out_shape=[
            jax.ShapeDtypeStruct((B, T, A_WIDTH), F32),
            jax.ShapeDtypeStruct((B, GLA_QK, GLA_DV), F32),
        ],
        scratch_shapes=[pltpu.VMEM((NB, GLA_QK, A_WIDTH), F32)],
        compiler_params=_params("parallel", "arbitrary"),
        name="gla",
    )(u, u, u, u, u, s0, w2p, bg, gn)
    return o.reshape(B * T, A_WIDTH), s_fin.reshape(B, GLA_HEADS, GLA_DK, GLA_DV)


N_RWKV_PER_BATCH = (11, 2, 6)
N_RWKV_SHARED = 15


def _rwkv_kernel(*refs, C, NB):
    n_in, n_out, n_scr = N_RWKV_PER_BATCH
    ins, shared = refs[:n_in], refs[n_in:n_in + N_RWKV_SHARED]
    outs = refs[n_in + N_RWKV_SHARED:n_in + N_RWKV_SHARED + n_out]
    scr = refs[n_in + N_RWKV_SHARED + n_out:]
    GW = GROUP * RWKV_N
    keep_h = (_iota((GW, GW), 0) >> 6) == (_iota((GW, GW), 1) >> 6)

    @pl.when(pl.program_id(1) == 0)
    def _():
        h0_ref, h_scr = ins[10], scr[5]
        for bb in range(NB):
            for prev, sh in zip(scr[:5], ins[5:10]):
                prev[bb] = sh[bb]
            h0 = h0_ref[bb]
            for g in range(RWKV_HEADS // GROUP):
                t = jnp.concatenate([h0[:, g * GW:(g + 1) * GW]] * GROUP, axis=0)
                h_scr[bb, g] = jnp.where(keep_h, t, 0.0)

    at = lambda rs, bb: [r.at[bb] for r in rs]
    chains = [_rwkv_one(*at(ins[:5], bb), *shared, *at(outs, bb), *at(scr, bb), keep_h, C=C) for bb in range(NB)]
    for _ in _lockstep(chains):
        pass


def _rwkv_one(r_ref, k_ref, v_ref, wa_ref, gl_ref,
              mu_r, mu_k, mu_v, mu_wa, mu_gl, w0_ref, w2_ref, a0_ref, a2_ref, g2_ref,
              kk_ref, ka_ref, rk_ref, gng_ref, gnb_ref,
              y_ref, hout_ref,
              pr, pk, pv, pwa, pgl, h_scr, keep_h, *, C):
    GW = GROUP * RWKV_N
    CH = GROUP * C
    logc = C.bit_length() - 1
    n_groups = RWKV_HEADS // GROUP

    def head_mask(rows, cols, row_shift, col_of_lane):
        keep = (_iota((rows, cols), 0) >> row_shift) == col_of_lane(_iota((rows, cols), 1))
        return keep.astype(F32).astype(BF16)

    keep_rows = head_mask(CH, GW, logc, lambda l: l >> 6)
    keep_rows2 = head_mask(CH, 2 * GW, logc, lambda l: (l & (GW - 1)) >> 6)
    keep_sq = head_mask(CH, CH, logc, lambda l: l >> logc)

    def bd(x, keep):
        return jnp.concatenate([x.astype(BF16)] * GROUP, axis=0) * keep

    rid = _iota((C, 1), 0)

    def shift(x_ref, prev, mu):
        x = x_ref[...]
        xp = jnp.where(rid == 0, prev[...], pltpu.roll(x, 1, axis=0))
        prev[...] = x[C - 1:C]
        return x + mu[...] * (xp - x)

    rs = shift(r_ref, pr, mu_r)
    ks = shift(k_ref, pk, mu_k)
    vs = shift(v_ref, pv, mu_v)
    was = shift(wa_ref, pwa, mu_wa)
    gls = shift(gl_ref, pgl, mu_gl)

    dec = -_softplus(-(w0_ref[...] + _dot3(jnp.tanh(was), w2_ref[...]))) - 0.5
    logw = -jnp.exp(dec)
    a = jax.nn.sigmoid(a0_ref[...] + _dot(was, a2_ref[...]))
    gate = _dot(jax.nn.sigmoid(gls), g2_ref[...])
    yield
    head_ones = ((_iota((B_WIDTH, B_WIDTH), 0) >> 6) == (_iota((B_WIDTH, B_WIDTH), 1) >> 6)).astype(BF16)
    kkr = ks * kk_ref[...]
    kk = kkr / jnp.maximum(jnp.sqrt(_dot(kkr * kkr, head_ones)), 1e-12)
    km = ks * (1.0 + (a - 1.0) * ka_ref[...])
    beta = kk * a
    tril = _iota((C, C), 0) >= _iota((C, C), 1)
    gc = _dot_exact_lhs(tril, logw)
    g_end = gc[C - 1:C]
    kap = kk * jnp.exp(gc - logw)
    rt = rs * jnp.exp(gc)
    e_neg = jnp.exp(-gc)
    b_inv = beta * e_neg
    k_inv = km * e_neg
    e_end = jnp.exp(g_end - gc)
    k_hat = km * e_end
    b_hat = beta * e_end

    col_s = _iota((C, CH), 1) & (C - 1)
    row_t = _iota((C, CH), 0)
    strict = row_t > col_s
    incl = row_t >= col_s
    same_sub = (row_t >> 4) == (col_s >> 4)
    eye_cat = (row_t == col_s).astype(F32)
    ones_c = jnp.ones((C, LANES), F32)

    yield
    ys = [None] * n_groups

    def group(g):
        L = slice(g * GW, (g + 1) * GW)
        p = jnp.concatenate([kap[:, L], rt[:, L]], axis=0)
        zb = _dot(p, bd(b_inv[:, L], keep_rows), NT)
        zk = _dot(p, bd(k_inv[:, L], keep_rows), NT)
        yield
        a_b = jnp.where(strict, zb[:C], 0.0)
        b_b = jnp.where(incl, zb[C:], 0.0)
        a_k = jnp.where(strict, zk[:C], 0.0)
        b_k = jnp.where(incl, zk[C:], 0.0)
        bd_v = bd(vs[:, L], keep_rows)
        akv = _dot(a_k, bd_v)

        d_blk = jnp.where(same_sub, a_b, 0.0)
        x1 = -d_blk
        p1 = eye_cat + x1
        x2 = _dot(x1, bd(x1, keep_sq))
        yield
        rhs = jnp.concatenate([kap[:, L], akv], axis=1)
        r2 = _dot(jnp.concatenate([x2, p1], axis=0), bd(x2, keep_sq))
        yield
        x4, p2 = r2[:C], p1 + r2[C:]
        r4 = _dot(jnp.concatenate([x4, p2], axis=0), bd(x4, keep_sq))
        yield
        x8, p3 = r4[:C], p2 + r4[C:]
        t_d = p3 + _dot(p3, bd(x8, keep_sq))
        yield

        n1 = _dot(t_d, bd(a_b - d_blk, keep_sq))
        s1 = _dot(t_d, bd(rhs, keep_rows2))
        yield
        n2 = _dot(n1, bd(n1, keep_sq))
        yield
        s2 = s1 + _dot(n2, bd(s1, keep_rows2))
        yield
        sol = s2 - _dot(n1, bd(s2, keep_rows2))
        yield
        w_mat, u0 = sol[:, :GW], sol[:, GW:]

        hbd = h_scr[g]
        hb = hbd.astype(BF16)
        u = _dot(w_mat, hb) + u0
        yield
        y = _dot(rt[:, L], hb) + _dot(b_k, bd_v) - _dot(b_b, bd(u, keep_rows))
        m = _dot(k_hat[:, L], vs[:, L], TN) - _dot(b_hat[:, L], u, TN)
        g_col = _dot_exact_rhs(logw[:, L], ones_c, TN)[:, :1]
        yield
        h_new = jnp.exp(g_col) * hbd + jnp.where(keep_h, m, 0.0)
        h_scr[g] = h_new
        hout_ref[:, L] = (h_new[0:RWKV_N] + h_new[RWKV_N:2 * RWKV_N]
                          + h_new[2 * RWKV_N:3 * RWKV_N] + h_new[3 * RWKV_N:4 * RWKV_N])
        ys[g] = y

    yield from _lockstep([group(g) for g in range(n_groups)])
    y = jnp.concatenate(ys, axis=1)

    inv_n = 1.0 / RWKV_N
    mu = _dot(y, head_ones) * inv_n
    d = y - mu
    var = _dot(d * d, head_ones) * inv_n
    yn = d * lax.rsqrt(var + RWKV_GN_EPS) * gng_ref[...] + gnb_ref[...]
    bonus = _dot(rs * km * rk_ref[...], head_ones) * vs
    y_ref[...] = (yn + bonus) * gate


def _rwkv(u, shift0, h0, wts, B, T, C):
    nc = T // C
    NB = MIXER_BATCH
    wide = lambda blk: pl.BlockSpec((NB, C, B_WIDTH), lambda b, c: (b, c, blk))
    narrow = lambda blk: pl.BlockSpec((NB, C, LANES), lambda b, c: (b, c, blk))
    sh = lambda w, blk: pl.BlockSpec((NB, 1, w), lambda b, c: (b, 0, blk))
    const = lambda shape: pl.BlockSpec(shape, lambda b, c: (0,) * len(shape))
    assert len(wts) == N_RWKV_SHARED
    in_specs = [wide(3), wide(4), wide(5), narrow(25), narrow(26),
                sh(B_WIDTH, 0), sh(B_WIDTH, 1), sh(B_WIDTH, 2), sh(LANES, 12), sh(LANES, 13),
                pl.BlockSpec((NB, RWKV_N, B_WIDTH), lambda b, c: (b, 0, 0))]
    in_specs += [const(w.shape) for w in wts]
    GW = GROUP * RWKV_N
    u = u.reshape(B, T, UW)
    y, h_fin = pl.pallas_call(
        functools.partial(_rwkv_kernel, C=C, NB=NB),
        grid=(B // NB, nc),
        in_specs=in_specs,
        out_specs=[
            pl.BlockSpec((NB, C, B_WIDTH), lambda b, c: (b, c, 0)),
            pl.BlockSpec((NB, RWKV_N, B_WIDTH), lambda b, c: (b, 0, 0)),
        ],
        out_shape=[
            jax.ShapeDtypeStruct((B, T, B_WIDTH), F32),
            jax.ShapeDtypeStruct((B, RWKV_N, B_WIDTH), F32),
        ],
        scratch_shapes=[
            pltpu.VMEM((NB, 1, B_WIDTH), F32), pltpu.VMEM((NB, 1, B_WIDTH), F32), pltpu.VMEM((NB, 1, B_WIDTH), F32),
            pltpu.VMEM((NB, 1, LANES), F32), pltpu.VMEM((NB, 1, LANES), F32),
            pltpu.VMEM((NB, RWKV_HEADS // GROUP, GW, GW), F32),
        ],
        compiler_params=_params("parallel", "arbitrary"),
        name="rwkv7",
    )(u, u, u, u, u, shift0, shift0, shift0, shift0, shift0, h0, *wts)
    return y.reshape(B * T, B_WIDTH), h_fin


def _ret_kernel(q_ref, k_ref, v_ref, g_ref, cos_ref, sin_ref, r0_ref, gng_ref, gnb_ref,
                o_ref, rout_ref, r_scr, *, C):
    c = pl.program_id(1)

    @pl.when(c == 0)
    def _():
        r_scr[...] = r0_ref[0]

    cos = cos_ref[...]
    sin = sin_ref[...]
    half = RET_DK // 2

    def rope(x):
        x1, x2 = x[:, :half], x[:, half:]
        return jnp.concatenate([x1 * cos - x2 * sin, x1 * sin + x2 * cos], axis=1)

    ri = _iota((C, C), 0)
    ci = _iota((C, C), 1)
    causal = ri >= ci
    diff = (ri - ci).astype(F32)
    pos1 = (_iota((C, 1), 0) + 1).astype(F32)
    outs = []
    for h in range(RET_HEADS):
        lg = math.log1p(-(2.0 ** (-5.0 - h)))
        q = rope(q_ref[:, h * RET_DK:(h + 1) * RET_DK])
        k = rope(k_ref[:, h * RET_DK:(h + 1) * RET_DK]) * (RET_DK ** -0.5)
        v = v_ref[:, h * RET_DV:(h + 1) * RET_DV]
        r = r_scr[h]
        decay = jnp.where(causal, jnp.exp(diff * lg), 0.0)
        att = _dot(q, k, NT) * decay
        o = _dot(att, v) + _dot(q, r) * jnp.exp(pos1 * lg)
        r_scr[h] = math.exp(C * lg) * r + _dot(k * jnp.exp((C - pos1) * lg), v, TN)
        mu = jnp.mean(o, axis=-1, keepdims=True)
        d = o - mu
        var = jnp.mean(d * d, axis=-1, keepdims=True)
        outs.append(d * lax.rsqrt(var + RET_GN_EPS))
    g = g_ref[...]
    o = jnp.concatenate(outs, axis=1) * gng_ref[...] + gnb_ref[...]
    o_ref[...] = (g * jax.nn.sigmoid(g)) * o
    rout_ref[0] = r_scr[...]


def _ret(u, cos, sin, r0, gn_g, gn_b, B, T, C):
    nc = T // C
    row = lambda b, c: b * nc + c
    return pl.pallas_call(
        functools.partial(_ret_kernel, C=C),
        grid=(B, nc),
        in_specs=[
            pl.BlockSpec((C, RET_QK), lambda b, c: (row(b, c), 0)),
            pl.BlockSpec((C, RET_QK), lambda b, c: (row(b, c), 1)),
            pl.BlockSpec((C, RET_V), lambda b, c: (row(b, c), 1)),
            pl.BlockSpec((C, RET_V), lambda b, c: (row(b, c), 2)),
            pl.BlockSpec((C, RET_DK // 2), lambda b, c: (c, 0)),
            pl.BlockSpec((C, RET_DK // 2), lambda b, c: (c, 0)),
            pl.BlockSpec((1, RET_HEADS, RET_DK, RET_DV), lambda b, c: (b, 0, 0, 0)),
            pl.BlockSpec((1, RET_V), lambda b, c: (0, 0)),
            pl.BlockSpec((1, RET_V), lambda b, c: (0, 0)),
        ],
        out_specs=[
            pl.BlockSpec((C, RET_V), lambda b, c: (row(b, c), 0)),
            pl.BlockSpec((1, RET_HEADS, RET_DK, RET_DV), lambda b, c: (b, 0, 0, 0)),
        ],
        out_shape=[
            jax.ShapeDtypeStruct((B * T, RET_V), F32),
            jax.ShapeDtypeStruct((B, RET_HEADS, RET_DK, RET_DV), F32),
        ],
        scratch_shapes=[pltpu.VMEM((RET_HEADS, RET_DK, RET_DV), F32)],
        compiler_params=_params("parallel", "arbitrary"),
        name="retention",
    )(u, u, u, u, cos, sin, r0, gn_g, gn_b)


def _router_kernel(x_ref, g_ref, wr_ref, ids_ref, gsel_ref):
    h = _rms(x_ref[...], g_ref[...])
    logits = _dot3(h, wr_ref[...])
    lane = _iota(logits.shape, 1)
    valid = lane < N_EXPERTS
    logits = jnp.where(valid, logits, -jnp.inf)
    m = jnp.max(logits, axis=-1, keepdims=True)
    e = jnp.where(valid, jnp.exp(logits - m), 0.0)
    p = e / jnp.sum(e, axis=-1, keepdims=True)
    big = jnp.int32(LANES)
    p1 = jnp.max(p, axis=-1, keepdims=True)
    i1 = jnp.min(jnp.where(p == p1, lane, big), axis=-1, keepdims=True)
    rest = jnp.where((lane == i1) | ~valid, -1.0, p)
    p2 = jnp.max(rest, axis=-1, keepdims=True)
    i2 = jnp.min(jnp.where(rest == p2, lane, big), axis=-1, keepdims=True)
    tot = p1 + p2
    ids_ref[...] = jnp.where(lane == 0, i1, i2)
    gsel_ref[...] = jnp.where(lane == 0, p1 / tot, p2 / tot)


def _router(x, g, wr):
    n, d = x.shape
    tm = min(n, 512)
    return pl.pallas_call(
        _router_kernel,
        grid=(n // tm,),
        in_specs=[
            pl.BlockSpec((tm, d), lambda i: (i, 0)),
            pl.BlockSpec((1, d), lambda i: (0, 0)),
            pl.BlockSpec((d, LANES), lambda i: (0, 0)),
        ],
        out_specs=[pl.BlockSpec((tm, LANES), lambda i: (i, 0)), pl.BlockSpec((tm, LANES), lambda i: (i, 0))],
        out_shape=[jax.ShapeDtypeStruct((n, LANES), jnp.int32), jax.ShapeDtypeStruct((n, LANES), F32)],
        compiler_params=_params("parallel"),
        name="router",
    )(x, g, wr)


def _dispatch_kernel(ends_ref, pos1_ref, pos2_ref, x_ref, xs_ref, zbuf, zsem, sem, *, TQ, TM):
    t = pl.program_id(0)

    @pl.when(t == 0)
    def _():
        zbuf[...] = jnp.zeros_like(zbuf)
        tails = [pltpu.make_async_copy(zbuf, xs_ref.at[pl.ds(pl.multiple_of(ends_ref[e] - TM, TM), TM)], zsem)
                 for e in range(N_EXPERTS)]
        for cp in tails:
            cp.start()
        for cp in tails:
            cp.wait()

        def zero_tile(i, carry):
            cp = pltpu.make_async_copy(zbuf, xs_ref.at[pl.ds(pl.multiple_of(i * TM, TM), TM)], zsem)
            cp.start()
            cp.wait()
            return carry

        lax.fori_loop(ends_ref[N_EXPERTS - 1] // TM, xs_ref.shape[0] // TM, zero_tile, 0)

    def row_copy(r, pos_ref):
        return pltpu.make_async_copy(x_ref.at[pl.ds(r, 1)], xs_ref.at[pl.ds(pos_ref[0, 0, r], 1)], sem)

    def issue(r, carry):
        row_copy(r, pos1_ref).start()
        row_copy(r, pos2_ref).start()
        return carry

    lax.fori_loop(0, TQ, issue, 0, unroll=8)

    def drain(r, carry):
        row_copy(r, pos1_ref).wait()
        row_copy(r, pos2_ref).wait()
        return carry

    lax.fori_loop(0, TQ, drain, 0, unroll=8)


def _dispatch(x, pos, ends, p, tm):
    n, d = x.shape
    tq = min(n, DISPATCH_ROWS)
    nt = n // tq
    pos3 = pos.reshape(2 * nt, 1, tq)
    grid_spec = pltpu.PrefetchScalarGridSpec(
        num_scalar_prefetch=1,
        grid=(nt,),
        in_specs=[pl.BlockSpec((1, 1, tq), lambda t, ends: (t, 0, 0), memory_space=pltpu.SMEM),
                  pl.BlockSpec((1, 1, tq), lambda t, ends: (t + nt, 0, 0), memory_space=pltpu.SMEM),
                  pl.BlockSpec((tq, d), lambda t, ends: (t, 0))],
        out_specs=pl.BlockSpec(memory_space=pl.ANY),
        scratch_shapes=[pltpu.VMEM((tm, d), F32), pltpu.SemaphoreType.DMA(()), pltpu.SemaphoreType.DMA(())],
    )
    return pl.pallas_call(
        functools.partial(_dispatch_kernel, TQ=tq, TM=tm),
        grid_spec=grid_spec,
        out_shape=jax.ShapeDtypeStruct((p, d), F32),
        compiler_params=_params("arbitrary"),
        name="dispatch",
    )(ends, pos3, pos3, x)


def _moe_ffn_kernel(te_ref, nu_ref, x_ref, g_ref, w1_ref, w3_ref, w2_ref, o_ref, acc, *, tf):
    t = pl.program_id(0)

    @pl.when(t < nu_ref[0])
    def _():
        h = _rms(x_ref[...], g_ref[...]).astype(BF16)
        for i, f0 in enumerate(range(0, D_FF, tf)):
            a1 = jnp.dot(h, w1_ref[0, :, f0:f0 + tf], preferred_element_type=F32)
            a3 = jnp.dot(h, w3_ref[0, :, f0:f0 + tf], preferred_element_type=F32)
            act = (a1 * jax.nn.sigmoid(a1) * a3).astype(BF16)
            part = jnp.dot(act, w2_ref[0, f0:f0 + tf, :], preferred_element_type=F32)
            if i == 0:
                acc[...] = part
            elif f0 + tf < D_FF:
                acc[...] += part
            else:
                o_ref[...] = acc[...] + part

    @pl.when(t >= nu_ref[0])
    def _():
        o_ref[...] = jnp.zeros_like(o_ref)


def _moe_ffn(xs, g, tile_expert, n_used, w1, w3, w2, tm):
    p, d = xs.shape
    dff = w1.shape[2]
    grid_spec = pltpu.PrefetchScalarGridSpec(
        num_scalar_prefetch=2,
        grid=(p // tm,),
        in_specs=[
            pl.BlockSpec((tm, d), lambda t, te, nu: (jnp.minimum(t, nu[0] - 1), 0)),
            pl.BlockSpec((1, d), lambda t, te, nu: (0, 0)),
            pl.BlockSpec((1, d, dff), lambda t, te, nu: (te[t], 0, 0)),
            pl.BlockSpec((1, d, dff), lambda t, te, nu: (te[t], 0, 0)),
            pl.BlockSpec((1, dff, d), lambda t, te, nu: (te[t], 0, 0)),
        ],
        out_specs=pl.BlockSpec((tm, d), lambda t, te, nu: (t, 0)),
        scratch_shapes=[pltpu.VMEM((tm, d), F32)],
    )
    return pl.pallas_call(
        functools.partial(_moe_ffn_kernel, tf=256),
        grid_spec=grid_spec,
        out_shape=jax.ShapeDtypeStruct((p, d), F32),
        compiler_params=_params("arbitrary"),
        name="moe_ffn",
    )(tile_expert, n_used, xs, g, w1, w3, w2)


def _combine_kernel(pos1_ref, pos2_ref, pos1n_ref, pos2n_ref, x_ref, gsel_ref, gf_ref, ys_ref, o_ref,
                    buf, sems, *, TQ):
    t = pl.program_id(0)
    slot = t % 2

    def row_copy(r, pos_ref, s, k):
        return pltpu.make_async_copy(ys_ref.at[pl.ds(pos_ref[0, 0, r], 1)], buf.at[s, k, pl.ds(r, 1)], sems.at[s])

    def issue(p1_ref, p2_ref, s):
        def body(r, carry):
            row_copy(r, p1_ref, s, 0).start()
            row_copy(r, p2_ref, s, 1).start()
            return carry

        lax.fori_loop(0, TQ, body, 0, unroll=8)

    @pl.when(t == 0)
    def _():
        issue(pos1_ref, pos2_ref, 0)

    @pl.when(t + 1 < pl.num_programs(0))
    def _():
        issue(pos1n_ref, pos2n_ref, 1 - slot)

    def drain(r, carry):
        row_copy(r, pos1_ref, slot, 0).wait()
        row_copy(r, pos2_ref, slot, 1).wait()
        return carry

    lax.fori_loop(0, TQ, drain, 0, unroll=8)
    gs = gsel_ref[...]
    y = x_ref[...] + gs[:, 0:1] * buf[slot, 0] + gs[:, 1:2] * buf[slot, 1]
    o_ref[...] = _rms(y, gf_ref[...])


def _combine(x, ys, pos, gsel, g_final):
    n, d = x.shape
    tq = min(n, DISPATCH_ROWS)
    nt = n // tq
    pos3 = pos.reshape(2 * nt, 1, tq)
    smem = lambda f: pl.BlockSpec((1, 1, tq), f, memory_space=pltpu.SMEM)
    nxt = lambda t: jnp.minimum(t + 1, nt - 1)
    return pl.pallas_call(
        functools.partial(_combine_kernel, TQ=tq),
        grid=(nt,),
        in_specs=[smem(lambda t: (t, 0, 0)), smem(lambda t: (t + nt, 0, 0)),
                  smem(lambda t: (nxt(t), 0, 0)), smem(lambda t: (nxt(t) + nt, 0, 0)),
                  pl.BlockSpec((tq, d), lambda t: (t, 0)),
                  pl.BlockSpec((tq, LANES), lambda t: (t, 0)),
                  pl.BlockSpec((1, d), lambda t: (0, 0)),
                  pl.BlockSpec(memory_space=pl.ANY)],
        out_specs=pl.BlockSpec((tq, d), lambda t: (t, 0)),
        out_shape=jax.ShapeDtypeStruct((n, d), F32),
        scratch_shapes=[pltpu.VMEM((2, 2, tq, d), F32), pltpu.SemaphoreType.DMA((2,))],
        compiler_params=_params("arbitrary"),
        name="combine",
    )(pos3, pos3, pos3, pos3, x, gsel, g_final, ys)


def _moe_plan(ids, tm):
    n = ids.shape[0]
    e_flat = ids.T.reshape(-1)
    experts = jnp.arange(N_EXPERTS, dtype=jnp.int32)
    onehot = (e_flat[:, None] == experts[None, :]).astype(jnp.int32)
    csum = jnp.cumsum(onehot, axis=0)
    counts = csum[-1]
    padded = jnp.maximum((counts + tm - 1) // tm, 1) * tm
    ends = jnp.cumsum(padded).astype(jnp.int32)
    pos = jnp.sum((csum - 1 + (ends - padded)[None, :]) * onehot, axis=1).astype(jnp.int32)
    p = 2 * n + N_EXPERTS * tm
    n_used = ends[-1] // tm
    tile_start = jnp.minimum(jnp.arange(p // tm, dtype=jnp.int32) * tm, ends[-1] - 1)
    tile_expert = jnp.sum((tile_start[:, None] >= ends[None, :]).astype(jnp.int32), axis=1)
    return pos, ends, tile_expert, n_used.reshape(1), p


def _ffn_kernel(x_ref, g_ref, w1_ref, w3_ref, w2_ref, o_ref, h_scr, acc):
    f = pl.program_id(1)

    @pl.when(f == 0)
    def _():
        h_scr[...] = _rms(x_ref[...], g_ref[...]).astype(BF16)
        acc[...] = x_ref[...]

    h = h_scr[...]
    a1 = jnp.dot(h, w1_ref[...], preferred_element_type=F32)
    a3 = jnp.dot(h, w3_ref[...], preferred_element_type=F32)
    act = a1 * jax.nn.sigmoid(a1) * a3
    acc[...] += jnp.dot(act.astype(BF16), w2_ref[...], preferred_element_type=F32)

    @pl.when(f == pl.num_programs(1) - 1)
    def _():
        o_ref[...] = acc[...]


def _ffn(x, g, w1, w3, w2):
    n, d = x.shape
    dff = w1.shape[1]
    tm = min(n, 1024)
    tf = 256
    return pl.pallas_call(
        _ffn_kernel,
        grid=(n // tm, dff // tf),
        in_specs=[pl.BlockSpec((tm, d), lambda i, f: (i, 0)),
                  pl.BlockSpec((1, d), lambda i, f: (0, 0)),
                  pl.BlockSpec((d, tf), lambda i, f: (0, f)),
                  pl.BlockSpec((d, tf), lambda i, f: (0, f)),
                  pl.BlockSpec((tf, d), lambda i, f: (f, 0))],
        out_specs=pl.BlockSpec((tm, d), lambda i, f: (i, 0)),
        out_shape=jax.ShapeDtypeStruct((n, d), F32),
        scratch_shapes=[pltpu.VMEM((tm, d), BF16), pltpu.VMEM((tm, d), F32)],
        compiler_params=_params("parallel", "arbitrary"),
        name="ffn",
    )(x, g, w1, w3, w2)


def _prep_weights(p):
    w_in = p["ab_w_in"][0]
    wa, wb = w_in[:, :A_COLS], w_in[:, A_COLS:]
    pad = jnp.zeros((D_MODEL, LANES - GLA_GATE_RANK), F32)
    w_in_p = jnp.concatenate(
        [wa[:, :2 * GLA_QK + A_WIDTH], wa[:, 2 * GLA_QK + A_WIDTH + GLA_GATE_RANK:], wb[:, :3 * B_WIDTH],
         wa[:, 2 * GLA_QK + A_WIDTH:2 * GLA_QK + A_WIDTH + GLA_GATE_RANK], pad, wb[:, 3 * B_WIDTH:]], axis=1)
    row = lambda v: v.reshape(1, -1).astype(F32)
    mu = p["rwkv_mu"][0]
    z64 = jnp.zeros((64, B_WIDTH), F32)
    w = dict(
        norm_mix=p["norm_mix"], norm_ffn=p["norm_ffn"], norm_final=row(p["norm_final"]),
        w_in=w_in_p.astype(BF16),
        w_out_a=p["ab_w_out"][0][:A_WIDTH].astype(BF16), w_out_b=p["ab_w_out"][0][A_WIDTH:].astype(BF16),
        gla_w2=jnp.concatenate([p["gla_w_gate2"][0], jnp.zeros((LANES - GLA_GATE_RANK, GLA_QK), F32)], axis=0),
        gla_bg=row(p["gla_b_gate"][0]), gla_norm=row(p["gla_norm"][0]),
        rwkv=(row(mu[:B_WIDTH]), row(mu[B_WIDTH:2 * B_WIDTH]), row(mu[2 * B_WIDTH:3 * B_WIDTH]),
              row(mu[3 * B_WIDTH:3 * B_WIDTH + LANES]), row(mu[3 * B_WIDTH + LANES:]),
              row(p["rwkv_w0"][0]), jnp.concatenate([p["rwkv_w2"][0], z64], axis=0),
              row(p["rwkv_a0"][0]), jnp.concatenate([z64, p["rwkv_a2"][0]], axis=0),
              p["rwkv_g2"][0], row(p["rwkv_k_k"][0]), row(p["rwkv_k_a"][0]), row(p["rwkv_r_k"][0]),
              row(p["rwkv_gn_g"][0]), row(p["rwkv_gn_b"][0])),
        ret_w_in=p["ret_w_in"][0].astype(BF16), ret_w_out=p["ret_w_out"][0].astype(BF16),
        ret_gn_g=row(p["ret_gn_g"][0]), ret_gn_b=row(p["ret_gn_b"][0]),
        ffn_w1=p["ffn_w1"][0].astype(BF16), ffn_w3=p["ffn_w3"][0].astype(BF16), ffn_w2=p["ffn_w2"][0].astype(BF16),
        router=jnp.concatenate([p["moe_router"][0], jnp.zeros((D_MODEL, LANES - N_EXPERTS), F32)], axis=1),
        moe_w1=p["moe_w1"][0].astype(BF16), moe_w3=p["moe_w3"][0].astype(BF16), moe_w2=p["moe_w2"][0].astype(BF16),
    )
    return w


def _trunk(x, pos0, st_gla, st_rwkv, st_shift, st_ret, w):
    B, T, D = x.shape
    n = B * T
    x0 = x.reshape(n, D)
    c_ab = min(CHUNK, T)
    c_ret = min(256, T)

    u = _norm_matmul(x0, w["norm_mix"][0:1], w["w_in"], tn=UW // 3)
    oa, s_gla = _gla(u, st_gla, w["gla_w2"], w["gla_bg"], w["gla_norm"], B, T, c_ab)
    h0 = st_rwkv.transpose(0, 3, 1, 2).reshape(B, RWKV_N, B_WIDTH)
    yb, h_fin = _rwkv(u, st_shift, h0, w["rwkv"], B, T, c_ab)
    s_rwkv = h_fin.reshape(B, RWKV_N, RWKV_HEADS, RWKV_N).transpose(0, 2, 3, 1)
    u_last = u.reshape(B, T, UW)[:, T - 1:, :]
    s_shift = jnp.concatenate([u_last[..., 3 * A_WIDTH:6 * A_WIDTH], u_last[..., UW - 2 * LANES:]], axis=-1)
    x1 = _proj_res([oa, yb], [w["w_out_a"], w["w_out_b"]], x0)
    x2 = _ffn(x1, w["norm_ffn"][0:1], w["ffn_w1"], w["ffn_w3"], w["ffn_w2"])

    ur = _norm_matmul(x2, w["norm_mix"][1:2], w["ret_w_in"], tn=1536)
    half = RET_DK // 2
    inv = 1.0 / (ROPE_BASE ** (jnp.arange(0, RET_DK, 2, dtype=F32) / RET_DK))
    ang = (pos0 + jnp.arange(T)).astype(F32)[:, None] * inv[None, :]
    o, s_ret = _ret(ur, jnp.cos(ang), jnp.sin(ang), st_ret, w["ret_gn_g"], w["ret_gn_b"], B, T, c_ret)
    x3 = _proj_res([o], [w["ret_w_out"]], x2)
    ids, gsel = _router(x3, w["norm_ffn"][1:2], w["router"])
    tm = min(MOE_TILE, n // 4)
    pos, ends, tile_expert, n_used, p = _moe_plan(ids[:, :2], tm)
    xs = _dispatch(x3, pos, ends, p, tm)
    ys = _moe_ffn(xs, w["norm_ffn"][1:2], tile_expert, n_used, w["moe_w1"], w["moe_w3"], w["moe_w2"], tm)
    y = _combine(x3, ys, pos, gsel, w["norm_final"])
    return y.reshape(B, T, D), s_gla[None], s_rwkv[None], s_shift[None], s_ret[None]


def kernel(x_prompt, x_sample, state_gla, state_rwkv, state_shift, state_ret, norm_mix, norm_ffn, norm_final, ab_w_in, ab_w_out, gla_w_gate2, gla_b_gate, gla_norm, rwkv_mu, rwkv_w0, rwkv_w2, rwkv_a0, rwkv_a2, rwkv_g2, rwkv_k_k, rwkv_k_a, rwkv_r_k, rwkv_gn_g, rwkv_gn_b, ret_w_in, ret_gn_g, ret_gn_b, ret_w_out, ffn_w1, ffn_w3, ffn_w2, moe_router, moe_w1, moe_w3, moe_w2):
    p = dict(norm_mix=norm_mix, norm_ffn=norm_ffn, norm_final=norm_final, ab_w_in=ab_w_in, ab_w_out=ab_w_out,
             gla_w_gate2=gla_w_gate2, gla_b_gate=gla_b_gate, gla_norm=gla_norm, rwkv_mu=rwkv_mu, rwkv_w0=rwkv_w0,
             rwkv_w2=rwkv_w2, rwkv_a0=rwkv_a0, rwkv_a2=rwkv_a2, rwkv_g2=rwkv_g2, rwkv_k_k=rwkv_k_k,
             rwkv_k_a=rwkv_k_a, rwkv_r_k=rwkv_r_k, rwkv_gn_g=rwkv_gn_g, rwkv_gn_b=rwkv_gn_b, ret_w_in=ret_w_in,
             ret_gn_g=ret_gn_g, ret_gn_b=ret_gn_b, ret_w_out=ret_w_out, ffn_w1=ffn_w1, ffn_w3=ffn_w3, ffn_w2=ffn_w2,
             moe_router=moe_router, moe_w1=moe_w1, moe_w3=moe_w3, moe_w2=moe_w2)
    w = _prep_weights(p)
    bp, tp = x_prompt.shape[0], x_prompt.shape[1]
    dt = x_prompt.dtype
    z_gla = jnp.zeros((bp, GLA_HEADS, GLA_DK, GLA_DV), dt)
    z_rwkv = jnp.zeros((bp, RWKV_HEADS, RWKV_N, RWKV_N), dt)
    z_shift = jnp.zeros((bp, 1, B_COLS), dt)
    z_ret = jnp.zeros((bp, RET_HEADS, RET_DK, RET_DV), dt)
    past = 2048
    yp = _trunk(x_prompt, 0, z_gla, z_rwkv, z_shift, z_ret, w)
    ys = _trunk(x_sample, past, state_gla[0], state_rwkv[0], state_shift[0], state_ret[0], w)
    return (yp[0], ys[0], yp[1], yp[2], yp[3], yp[4], ys[1], ys[2], ys[3], ys[4])
```

```python
import functools
import math

import jax
import jax.numpy as jnp
from jax import lax
from jax.experimental import pallas as pl
from jax.experimental.pallas import tpu as pltpu

F32 = jnp.float32
BF16 = jnp.bfloat16

D_MODEL = 1024
EPS = 1e-6
GLA_HEADS, GLA_DK, GLA_DV = 4, 64, 128
GLA_QK = GLA_HEADS * GLA_DK
A_WIDTH = GLA_HEADS * GLA_DV
GLA_GATE_RANK = 16
GLA_TAU = 16.0
RWKV_HEADS, RWKV_N = 8, 64
B_WIDTH = RWKV_HEADS * RWKV_N
RWKV_GN_EPS = 64e-5
B_COLS = 3 * B_WIDTH + 64 + 64 + 128
A_COLS = 2 * GLA_QK + A_WIDTH + GLA_GATE_RANK + A_WIDTH
RET_HEADS = 4
RET_DK = D_MODEL // RET_HEADS
RET_DV = 2 * D_MODEL // RET_HEADS
RET_QK = RET_HEADS * RET_DK
RET_V = RET_HEADS * RET_DV
RET_GN_EPS = 1e-5
ROPE_BASE = 10000.0
D_FF = 2816
N_EXPERTS = 8
CHUNK = 64
SUB = 16
LANES = 128
GROUP = 4
MIXER_BATCH = 4
MOE_TILE = 512
DISPATCH_ROWS = 512
UW = 3456
VMEM_LIMIT = 56 * 1024 * 1024

NN = (((1,), (0,)), ((), ()))
NT = (((1,), (1,)), ((), ()))
TN = (((0,), (0,)), ((), ()))


def _dot(a, b, dims=NN):
    return lax.dot_general(a.astype(BF16), b.astype(BF16), dims, preferred_element_type=F32)


def _split(x):
    hi = x.astype(BF16)
    lo = (x - hi.astype(F32)).astype(BF16)
    return hi, lo


def _dot3(a, b, dims=NN):
    ah, al = _split(a)
    bh, bl = _split(b)
    dg = lambda x, y: lax.dot_general(x, y, dims, preferred_element_type=F32)
    return dg(ah, bh) + dg(ah, bl) + dg(al, bh)


def _dot_exact_lhs(m, x, dims=NN):
    xh, xl = _split(x)
    mb = m.astype(BF16)
    dg = lambda y: lax.dot_general(mb, y, dims, preferred_element_type=F32)
    return dg(xh) + dg(xl)


def _dot_exact_rhs(x, m, dims=NN):
    xh, xl = _split(x)
    mb = m.astype(BF16)
    dg = lambda y: lax.dot_general(y, mb, dims, preferred_element_type=F32)
    return dg(xh) + dg(xl)


def _iota(shape, axis):
    return lax.broadcasted_iota(jnp.int32, shape, axis)


def _softplus(z):
    return jnp.maximum(z, 0.0) + jnp.log1p(jnp.exp(-jnp.abs(z)))


def _rms(x, g):
    return x * lax.rsqrt(jnp.mean(x * x, axis=-1, keepdims=True) + EPS) * g


def _lockstep(chains):
    chains = list(chains)
    while chains:
        alive = []
        for ch in chains:
            try:
                next(ch)
                alive.append(ch)
            except StopIteration:
                pass
        chains = alive
        yield


def _params(*sem):
    return pltpu.CompilerParams(dimension_semantics=sem, vmem_limit_bytes=VMEM_LIMIT)


def _norm_matmul_kernel(x_ref, g_ref, w_ref, o_ref, h_ref):
    @pl.when(pl.program_id(1) == 0)
    def _():
        h_ref[...] = _rms(x_ref[...], g_ref[...]).astype(BF16)

    o_ref[...] = jnp.dot(h_ref[...], w_ref[...], preferred_element_type=F32).astype(o_ref.dtype)


def _norm_matmul(x, g, w, tn, out_dtype):
    n, d = x.shape
    nout = w.shape[1]
    tm = min(n, 1024)
    return pl.pallas_call(
        _norm_matmul_kernel,
        grid=(n // tm, nout // tn),
        in_specs=[
            pl.BlockSpec((tm, d), lambda i, j: (i, 0)),
            pl.BlockSpec((1, d), lambda i, j: (0, 0)),
            pl.BlockSpec((d, tn), lambda i, j: (0, j)),
        ],
        out_specs=pl.BlockSpec((tm, tn), lambda i, j: (i, j)),
        out_shape=jax.ShapeDtypeStruct((n, nout), out_dtype),
        scratch_shapes=[pltpu.VMEM((tm, d), BF16)],
        compiler_params=_params("parallel", "arbitrary"),
        name="norm_matmul",
    )(x, g, w)


def _proj_res_kernel(*refs, n_in):
    a_refs = refs[:n_in]
    w_refs = refs[n_in:2 * n_in]
    res_ref, o_ref = refs[2 * n_in], refs[2 * n_in + 1]
    acc = res_ref[...]
    for a_ref, w_ref in zip(a_refs, w_refs):
        acc = acc + jnp.dot(a_ref[...].astype(BF16), w_ref[...], preferred_element_type=F32)
    o_ref[...] = acc


def _proj_res(a_list, w_list, res):
    n, d = res.shape
    tm = min(n, 512)
    n_in = len(a_list)
    in_specs = [pl.BlockSpec((tm, a.shape[1]), lambda i: (i, 0)) for a in a_list]
    in_specs += [pl.BlockSpec(w.shape, lambda i: (0, 0)) for w in w_list]
    in_specs += [pl.BlockSpec((tm, d), lambda i: (i, 0))]
    return pl.pallas_call(
        functools.partial(_proj_res_kernel, n_in=n_in),
        grid=(n // tm,),
        in_specs=in_specs,
        out_specs=pl.BlockSpec((tm, d), lambda i: (i, 0)),
        out_shape=jax.ShapeDtypeStruct((n, d), F32),
        compiler_params=_params("parallel"),
        name="proj_res",
    )(*a_list, *w_list, res)


def _gla_kernel(q_ref, k_ref, v_ref, ga_ref, og_ref, s0_ref, w2_ref, bg_ref, gn_ref,
                o_ref, sout_ref, s_scr, *, C, NB):
    keep_s = (_iota((GLA_QK, A_WIDTH), 0) >> 6) == (_iota((GLA_QK, A_WIDTH), 1) >> 7)

    @pl.when(pl.program_id(1) == 0)
    def _():
        for bb in range(NB):
            s_scr[bb] = jnp.where(keep_s, jnp.concatenate([s0_ref[bb]] * GLA_HEADS, axis=1), 0.0)

    chains = [_gla_one(q_ref.at[bb], k_ref.at[bb], v_ref.at[bb], ga_ref.at[bb], og_ref.at[bb],
                       w2_ref, bg_ref, gn_ref, o_ref.at[bb], sout_ref.at[bb], s_scr.at[bb], keep_s, C=C)
              for bb in range(NB)]
    for _ in _lockstep(chains):
        pass


def _gla_one(q_ref, k_ref, v_ref, ga_ref, og_ref, w2_ref, bg_ref, gn_ref,
             o_ref, sout_ref, s_scr, keep_s, *, C):
    x = _dot3(ga_ref[...], w2_ref[...]) + bg_ref[...]
    yield
    logg = (jnp.minimum(x, 0.0) - jnp.log1p(jnp.exp(-jnp.abs(x)))) * (1.0 / GLA_TAU)
    tril = _iota((C, C), 0) >= _iota((C, C), 1)
    b = _dot_exact_lhs(tril, logg)
    g_col = _dot_exact_rhs(logg, jnp.ones((C, LANES), F32), TN)
    yield
    q = q_ref[...] * (GLA_DK ** -0.5)
    k = k_ref[...]
    v = v_ref[...]
    s = s_scr[...]
    o_inter = _dot(q * jnp.exp(b), s)
    bl = b[C - 1:C]
    upd = _dot(k * jnp.exp(bl - b), v, TN)

    head_k = _iota((1, GLA_QK), 1) >> 6
    head_v = _iota((1, A_WIDTH), 1) >> 7
    ones_kv = keep_s.astype(BF16)
    ridx = _iota((SUB, 1), 0)

    def bd(x, head_of_lane):
        xb = x.astype(BF16)
        zero = jnp.zeros_like(xb)
        return jnp.concatenate([jnp.where(head_of_lane == h, xb, zero) for h in range(GLA_HEADS)], axis=0)

    starts = range(0, C, SUB)
    atts = {}
    for i0 in starts[1:]:
        m = b[i0 - 1:i0]
        qi, bi = q[i0:i0 + SUB], b[i0:i0 + SUB]
        atts[i0] = _dot(qi * jnp.exp(bi - m), bd(k[:i0] * jnp.exp(m - b[:i0]), head_k), NT)
    yield
    scs = {}
    for i0 in starts:
        qi, bi, ki = q[i0:i0 + SUB], b[i0:i0 + SUB], k[i0:i0 + SUB]
        cols = []
        for j in range(SUB):
            e = jnp.exp(jnp.minimum(bi - bi[j:j + 1], 0.0))
            cols.append(jnp.where(ridx >= j, qi * e * ki[j:j + 1], 0.0))
        scs[i0] = _dot(jnp.concatenate(cols, axis=0), ones_kv)
    offs = {i0: _dot(atts[i0], bd(v[:i0], head_v)) for i0 in starts[1:]}
    yield
    blocks = []
    for i0 in starts:
        vi = v[i0:i0 + SUB]
        oi = o_inter[i0:i0 + SUB]
        if i0 > 0:
            oi = oi + offs[i0]
        for j in range(SUB):
            oi = oi + scs[i0][j * SUB:(j + 1) * SUB] * vi[j:j + 1]
        blocks.append(oi)
    o = jnp.concatenate(blocks, axis=0)

    e_col = jnp.concatenate([jnp.exp(g_col)] * (A_WIDTH // LANES), axis=1)
    s_new = e_col * s + jnp.where(keep_s, upd, 0.0)
    s_scr[...] = s_new
    sout_ref[...] = (s_new[:, 0:GLA_DV] + s_new[:, GLA_DV:2 * GLA_DV]
                     + s_new[:, 2 * GLA_DV:3 * GLA_DV] + s_new[:, 3 * GLA_DV:4 * GLA_DV])

    outs = [_rms(o[:, h * GLA_DV:(h + 1) * GLA_DV], gn_ref[...]) for h in range(GLA_HEADS)]
    og = og_ref[...]
    o_ref[...] = (jnp.concatenate(outs, axis=1) * (og * jax.nn.sigmoid(og))).astype(o_ref.dtype)


def _gla(u, s0, w2p, bg, gn, B, T, C):
    nc = T // C
    NB = MIXER_BATCH
    s0 = s0.reshape(B, GLA_QK, GLA_DV)
    u = u.reshape(B, T, UW)
    o, s_fin = pl.pallas_call(
        functools.partial(_gla_kernel, C=C, NB=NB),
        grid=(B // NB, nc),
        in_specs=[
            pl.BlockSpec((NB, C, GLA_QK), lambda b, c: (b, c, 0)),
            pl.BlockSpec((NB, C, GLA_QK), lambda b, c: (b, c, 1)),
            pl.BlockSpec((NB, C, A_WIDTH), lambda b, c: (b, c, 1)),
            pl.BlockSpec((NB, C, LANES), lambda b, c: (b, c, 24)),
            pl.BlockSpec((NB, C, A_WIDTH), lambda b, c: (b, c, 2)),
            pl.BlockSpec((NB, GLA_QK, GLA_DV), lambda b, c: (b, 0, 0)),
            pl.BlockSpec((LANES, GLA_QK), lambda b, c: (0, 0)),
            pl.BlockSpec((1, GLA_QK), lambda b, c: (0, 0)),
            pl.BlockSpec((1, GLA_DV), lambda b, c: (0, 0)),
        ],
        out_specs=[
            pl.BlockSpec((NB, C, A_WIDTH), lambda b, c: (b, c, 0)),
            pl.BlockSpec((NB, GLA_QK, GLA_DV), lambda b, c: (b, 0, 0)),
        ],
        out_shape=[
            jax.ShapeDtypeStruct((B, T, A_WIDTH), BF16),
            jax.ShapeDtypeStruct((B, GLA_QK, GLA_DV), F32),
        ],
        scratch_shapes=[pltpu.VMEM((NB, GLA_QK, A_WIDTH), F32)],
        compiler_params=_params("parallel", "arbitrary"),
        name="gla",
    )(u, u, u, u, u, s0, w2p, bg, gn)
    return o.reshape(B * T, A_WIDTH), s_fin.reshape(B, GLA_HEADS, GLA_DK, GLA_DV)


N_RWKV_PER_BATCH = (11, 2, 6)
N_RWKV_SHARED = 15


def _rwkv_kernel(*refs, C, NB):
    n_in, n_out, n_scr = N_RWKV_PER_BATCH
    ins, shared = refs[:n_in], refs[n_in:n_in + N_RWKV_SHARED]
    outs = refs[n_in + N_RWKV_SHARED:n_in + N_RWKV_SHARED + n_out]
    scr = refs[n_in + N_RWKV_SHARED + n_out:]
    GW = GROUP * RWKV_N
    keep_h = (_iota((GW, GW), 0) >> 6) == (_iota((GW, GW), 1) >> 6)

    @pl.when(pl.program_id(1) == 0)
    def _():
        h0_ref, h_scr = ins[10], scr[5]
        for bb in range(NB):
            for prev, sh in zip(scr[:5], ins[5:10]):
                prev[bb] = sh[bb]
            h0 = h0_ref[bb]
            for g in range(RWKV_HEADS // GROUP):
                t = jnp.concatenate([h0[:, g * GW:(g + 1) * GW]] * GROUP, axis=0)
                h_scr[bb, g] = jnp.where(keep_h, t, 0.0)

    at = lambda rs, bb: [r.at[bb] for r in rs]
    chains = [_rwkv_one(*at(ins[:5], bb), *shared, *at(outs, bb), *at(scr, bb), keep_h, C=C) for bb in range(NB)]
    for _ in _lockstep(chains):
        pass


def _rwkv_one(r_ref, k_ref, v_ref, wa_ref, gl_ref,
              mu_r, mu_k, mu_v, mu_wa, mu_gl, w0_ref, w2_ref, a0_ref, a2_ref, g2_ref,
              kk_ref, ka_ref, rk_ref, gng_ref, gnb_ref,
              y_ref, hout_ref,
              pr, pk, pv, pwa, pgl, h_scr, keep_h, *, C):
    GW = GROUP * RWKV_N
    CH = GROUP * C
    logc = C.bit_length() - 1
    n_groups = RWKV_HEADS // GROUP

    def head_mask(rows, cols, row_shift, col_of_lane):
        keep = (_iota((rows, cols), 0) >> row_shift) == col_of_lane(_iota((rows, cols), 1))
        return keep.astype(F32).astype(BF16)

    keep_rows = head_mask(CH, GW, logc, lambda l: l >> 6)
    keep_rows2 = head_mask(CH, 2 * GW, logc, lambda l: (l & (GW - 1)) >> 6)
    keep_sq = head_mask(CH, CH, logc, lambda l: l >> logc)

    def bd(x, keep):
        return jnp.concatenate([x.astype(BF16)] * GROUP, axis=0) * keep

    rid = _iota((C, 1), 0)

    def shift(x_ref, prev, mu):
        x = x_ref[...]
        xp = jnp.where(rid == 0, prev[...], pltpu.roll(x, 1, axis=0))
        prev[...] = x[C - 1:C]
        return x + mu[...] * (xp - x)

    rs = shift(r_ref, pr, mu_r)
    ks = shift(k_ref, pk, mu_k)
    vs = shift(v_ref, pv, mu_v)
    was = shift(wa_ref, pwa, mu_wa)
    gls = shift(gl_ref, pgl, mu_gl)

    dec = -_softplus(-(w0_ref[...] + _dot3(jnp.tanh(was), w2_ref[...]))) - 0.5
    logw = -jnp.exp(dec)
    a = jax.nn.sigmoid(a0_ref[...] + _dot(was, a2_ref[...]))
    gate = _dot(jax.nn.sigmoid(gls), g2_ref[...])
    yield
    ones_h = keep_h.astype(F32).astype(BF16)

    def head_sum(x):
        return jnp.concatenate([_dot(x[:, g * GW:(g + 1) * GW], ones_h) for g in range(n_groups)], axis=1)

    kkr = ks * kk_ref[...]
    kk = kkr / jnp.maximum(jnp.sqrt(head_sum(kkr * kkr)), 1e-12)
    km = ks * (1.0 + (a - 1.0) * ka_ref[...])
    beta = kk * a
    tril = _iota((C, C), 0) >= _iota((C, C), 1)
    gc = _dot_exact_lhs(tril, logw)
    g_end = gc[C - 1:C]
    kap = kk * jnp.exp(gc - logw)
    rt = rs * jnp.exp(gc)
    e_neg = jnp.exp(-gc)
    b_inv = beta * e_neg
    k_inv = km * e_neg
    e_end = jnp.exp(g_end - gc)
    k_hat = km * e_end
    b_hat = beta * e_end

    col_s = _iota((C, CH), 1) & (C - 1)
    row_t = _iota((C, CH), 0)
    strict = row_t > col_s
    incl = row_t >= col_s
    same_sub = (row_t >> 4) == (col_s >> 4)
    eye_cat = (row_t == col_s).astype(F32)
    ones_c = jnp.ones((C, LANES), F32)

    yield
    ys = [None] * n_groups

    def group(g):
        L = slice(g * GW, (g + 1) * GW)
        p = jnp.concatenate([kap[:, L], rt[:, L]], axis=0)
        zb = _dot(p, bd(b_inv[:, L], keep_rows), NT)
        zk = _dot(p, bd(k_inv[:, L], keep_rows), NT)
        yield
        a_b = jnp.where(strict, zb[:C], 0.0)
        b_b = jnp.where(incl, zb[C:], 0.0)
        a_k = jnp.where(strict, zk[:C], 0.0)
        b_k = jnp.where(incl, zk[C:], 0.0)
        bd_v = bd(vs[:, L], keep_rows)
        akv = _dot(a_k, bd_v)

        d_blk = jnp.where(same_sub, a_b, 0.0)
        x1 = -d_blk
        p1 = eye_cat + x1
        x2 = _dot(x1, bd(x1, keep_sq))
        yield
        rhs = jnp.concatenate([kap[:, L], akv], axis=1)
        r2 = _dot(jnp.concatenate([x2, p1], axis=0), bd(x2, keep_sq))
        yield
        x4, p2 = r2[:C], p1 + r2[C:]
        r4 = _dot(jnp.concatenate([x4, p2], axis=0), bd(x4, keep_sq))
        yield
        x8, p3 = r4[:C], p2 + r4[C:]
        t_d = p3 + _dot(p3, bd(x8, keep_sq))
        yield

        n1 = _dot(t_d, bd(a_b - d_blk, keep_sq))
        s1 = _dot(t_d, bd(rhs, keep_rows2))
        yield
        n2 = _dot(n1, bd(n1, keep_sq))
        yield
        s2 = s1 + _dot(n2, bd(s1, keep_rows2))
        yield
        sol = s2 - _dot(n1, bd(s2, keep_rows2))
        yield
        w_mat, u0 = sol[:, :GW], sol[:, GW:]

        hbd = h_scr[g]
        hb = hbd.astype(BF16)
        u = _dot(w_mat, hb) + u0
        yield
        y = _dot(rt[:, L], hb) + _dot(b_k, bd_v) - _dot(b_b, bd(u, keep_rows))
        m = _dot(k_hat[:, L], vs[:, L], TN) - _dot(b_hat[:, L], u, TN)
        g_col = _dot_exact_rhs(logw[:, L], ones_c, TN)[:, :1]
        yield
        h_new = jnp.exp(g_col) * hbd + jnp.where(keep_h, m, 0.0)
        h_scr[g] = h_new
        hout_ref[:, L] = (h_new[0:RWKV_N] + h_new[RWKV_N:2 * RWKV_N]
                          + h_new[2 * RWKV_N:3 * RWKV_N] + h_new[3 * RWKV_N:4 * RWKV_N])
        ys[g] = y

    yield from _lockstep([group(g) for g in range(n_groups)])
    y = jnp.concatenate(ys, axis=1)

    inv_n = 1.0 / RWKV_N
    mu = head_sum(y) * inv_n
    d = y - mu
    var = head_sum(d * d) * inv_n
    yn = d * lax.rsqrt(var + RWKV_GN_EPS) * gng_ref[...] + gnb_ref[...]
    bonus = head_sum(rs * km * rk_ref[...]) * vs
    y_ref[...] = ((yn + bonus) * gate).astype(y_ref.dtype)


def _rwkv(u, shift0, h0, wts, B, T, C):
    nc = T // C
    NB = MIXER_BATCH
    wide = lambda blk: pl.BlockSpec((NB, C, B_WIDTH), lambda b, c: (b, c, blk))
    narrow = lambda blk: pl.BlockSpec((NB, C, LANES), lambda b, c: (b, c, blk))
    sh = lambda w, blk: pl.BlockSpec((NB, 1, w), lambda b, c: (b, 0, blk))
    const = lambda shape: pl.BlockSpec(shape, lambda b, c: (0,) * len(shape))
    assert len(wts) == N_RWKV_SHARED
    in_specs = [wide(3), wide(4), wide(5), narrow(25), narrow(26),
                sh(B_WIDTH, 0), sh(B_WIDTH, 1), sh(B_WIDTH, 2), sh(LANES, 12), sh(LANES, 13),
                pl.BlockSpec((NB, RWKV_N, B_WIDTH), lambda b, c: (b, 0, 0))]
    in_specs += [const(w.shape) for w in wts]
    GW = GROUP * RWKV_N
    u = u.reshape(B, T, UW)
    y, h_fin = pl.pallas_call(
        functools.partial(_rwkv_kernel, C=C, NB=NB),
        grid=(B // NB, nc),
        in_specs=in_specs,
        out_specs=[
            pl.BlockSpec((NB, C, B_WIDTH), lambda b, c: (b, c, 0)),
            pl.BlockSpec((NB, RWKV_N, B_WIDTH), lambda b, c: (b, 0, 0)),
        ],
        out_shape=[
            jax.ShapeDtypeStruct((B, T, B_WIDTH), BF16),
            jax.ShapeDtypeStruct((B, RWKV_N, B_WIDTH), F32),
        ],
        scratch_shapes=[
            pltpu.VMEM((NB, 1, B_WIDTH), F32), pltpu.VMEM((NB, 1, B_WIDTH), F32), pltpu.VMEM((NB, 1, B_WIDTH), F32),
            pltpu.VMEM((NB, 1, LANES), F32), pltpu.VMEM((NB, 1, LANES), F32),
            pltpu.VMEM((NB, RWKV_HEADS // GROUP, GW, GW), F32),
        ],
        compiler_params=_params("parallel", "arbitrary"),
        name="rwkv7",
    )(u, u, u, u, u, shift0, shift0, shift0, shift0, shift0, h0, *wts)
    return y.reshape(B * T, B_WIDTH), h_fin


def _ret_kernel(q_ref, k_ref, v_ref, g_ref, cos_ref, sin_ref, r0_ref, gng_ref, gnb_ref,
                o_ref, rout_ref, r_scr, *, C):
    c = pl.program_id(1)

    @pl.when(c == 0)
    def _():
        r_scr[...] = r0_ref[0]

    cos = cos_ref[...]
    sin = sin_ref[...]
    half = RET_DK // 2

    def rope(x):
        x1, x2 = x[:, :half], x[:, half:]
        return jnp.concatenate([x1 * cos - x2 * sin, x1 * sin + x2 * cos], axis=1)

    ri = _iota((C, C), 0)
    ci = _iota((C, C), 1)
    causal = ri >= ci
    diff = (ri - ci).astype(F32)
    pos1 = (_iota((C, 1), 0) + 1).astype(F32)
    outs = []
    for h in range(RET_HEADS):
        lg = math.log1p(-(2.0 ** (-5.0 - h)))
        q = rope(q_ref[:, h * RET_DK:(h + 1) * RET_DK].astype(F32))
        k = rope(k_ref[:, h * RET_DK:(h + 1) * RET_DK].astype(F32)) * (RET_DK ** -0.5)
        v = v_ref[:, h * RET_DV:(h + 1) * RET_DV]
        r = r_scr[h]
        decay = jnp.where(causal, jnp.exp(diff * lg), 0.0)
        att = _dot(q, k, NT) * decay
        o = _dot(att, v) + _dot(q, r) * jnp.exp(pos1 * lg)
        r_scr[h] = math.exp(C * lg) * r + _dot(k * jnp.exp((C - pos1) * lg), v, TN)
        mu = jnp.mean(o, axis=-1, keepdims=True)
        d = o - mu
        var = jnp.mean(d * d, axis=-1, keepdims=True)
        outs.append(d * lax.rsqrt(var + RET_GN_EPS))
    g = g_ref[...].astype(F32)
    o = jnp.concatenate(outs, axis=1) * gng_ref[...] + gnb_ref[...]
    o_ref[...] = ((g * jax.nn.sigmoid(g)) * o).astype(o_ref.dtype)
    rout_ref[0] = r_scr[...]


def _ret(u, cos, sin, r0, gn_g, gn_b, B, T, C):
    nc = T // C
    row = lambda b, c: b * nc + c
    return pl.pallas_call(
        functools.partial(_ret_kernel, C=C),
        grid=(B, nc),
        in_specs=[
            pl.BlockSpec((C, RET_QK), lambda b, c: (row(b, c), 0)),
            pl.BlockSpec((C, RET_QK), lambda b, c: (row(b, c), 1)),
            pl.BlockSpec((C, RET_V), lambda b, c: (row(b, c), 1)),
            pl.BlockSpec((C, RET_V), lambda b, c: (row(b, c), 2)),
            pl.BlockSpec((C, RET_DK // 2), lambda b, c: (c, 0)),
            pl.BlockSpec((C, RET_DK // 2), lambda b, c: (c, 0)),
            pl.BlockSpec((1, RET_HEADS, RET_DK, RET_DV), lambda b, c: (b, 0, 0, 0)),
            pl.BlockSpec((1, RET_V), lambda b, c: (0, 0)),
            pl.BlockSpec((1, RET_V), lambda b, c: (0, 0)),
        ],
        out_specs=[
            pl.BlockSpec((C, RET_V), lambda b, c: (row(b, c), 0)),
            pl.BlockSpec((1, RET_HEADS, RET_DK, RET_DV), lambda b, c: (b, 0, 0, 0)),
        ],
        out_shape=[
            jax.ShapeDtypeStruct((B * T, RET_V), BF16),
            jax.ShapeDtypeStruct((B, RET_HEADS, RET_DK, RET_DV), F32),
        ],
        scratch_shapes=[pltpu.VMEM((RET_HEADS, RET_DK, RET_DV), F32)],
        compiler_params=_params("parallel", "arbitrary"),
        name="retention",
    )(u, u, u, u, cos, sin, r0, gn_g, gn_b)


def _router_kernel(x_ref, g_ref, wr_ref, ids_ref, gsel_ref):
    h = _rms(x_ref[...], g_ref[...])
    logits = _dot3(h, wr_ref[...])
    lane = _iota(logits.shape, 1)
    valid = lane < N_EXPERTS
    logits = jnp.where(valid, logits, -jnp.inf)
    m = jnp.max(logits, axis=-1, keepdims=True)
    e = jnp.where(valid, jnp.exp(logits - m), 0.0)
    p = e / jnp.sum(e, axis=-1, keepdims=True)
    big = jnp.int32(LANES)
    p1 = jnp.max(p, axis=-1, keepdims=True)
    i1 = jnp.min(jnp.where(p == p1, lane, big), axis=-1, keepdims=True)
    rest = jnp.where((lane == i1) | ~valid, -1.0, p)
    p2 = jnp.max(rest, axis=-1, keepdims=True)
    i2 = jnp.min(jnp.where(rest == p2, lane, big), axis=-1, keepdims=True)
    tot = p1 + p2
    ids_ref[...] = jnp.where(lane == 0, i1, i2)
    gsel_ref[...] = jnp.where(lane == 0, p1 / tot, p2 / tot)


def _router(x, g, wr):
    n, d = x.shape
    tm = min(n, 512)
    return pl.pallas_call(
        _router_kernel,
        grid=(n // tm,),
        in_specs=[
            pl.BlockSpec((tm, d), lambda i: (i, 0)),
            pl.BlockSpec((1, d), lambda i: (0, 0)),
            pl.BlockSpec((d, LANES), lambda i: (0, 0)),
        ],
        out_specs=[pl.BlockSpec((tm, LANES), lambda i: (i, 0)), pl.BlockSpec((tm, LANES), lambda i: (i, 0))],
        out_shape=[jax.ShapeDtypeStruct((n, LANES), jnp.int32), jax.ShapeDtypeStruct((n, LANES), F32)],
        compiler_params=_params("parallel"),
        name="router",
    )(x, g, wr)


def _dispatch_kernel(ends_ref, pos1_ref, pos2_ref, x_ref, xs_ref, zbuf, zsem, sem, *, TQ, TM):
    t = pl.program_id(0)

    @pl.when(t == 0)
    def _():
        zbuf[...] = jnp.zeros_like(zbuf)
        tails = [pltpu.make_async_copy(zbuf, xs_ref.at[pl.ds(pl.multiple_of(ends_ref[e] - TM, TM), TM)], zsem)
                 for e in range(N_EXPERTS)]
        for cp in tails:
            cp.start()
        for cp in tails:
            cp.wait()

        def zero_tile(i, carry):
            cp = pltpu.make_async_copy(zbuf, xs_ref.at[pl.ds(pl.multiple_of(i * TM, TM), TM)], zsem)
            cp.start()
            cp.wait()
            return carry

        lax.fori_loop(ends_ref[N_EXPERTS - 1] // TM, xs_ref.shape[0] // TM, zero_tile, 0)

    def row_copy(r, pos_ref):
        return pltpu.make_async_copy(x_ref.at[pl.ds(r, 1)], xs_ref.at[pl.ds(pos_ref[0, 0, r], 1)], sem)

    def issue(r, carry):
        row_copy(r, pos1_ref).start(priority=0)
        row_copy(r, pos2_ref).start(priority=1)
        return carry

    lax.fori_loop(0, TQ, issue, 0, unroll=8)

    def drain(r, carry):
        row_copy(r, pos1_ref).wait()
        row_copy(r, pos2_ref).wait()
        return carry

    lax.fori_loop(0, TQ, drain, 0, unroll=8)


def _dispatch(x, pos, ends, p, tm):
    n, d = x.shape
    tq = min(n, DISPATCH_ROWS)
    nt = n // tq
    pos3 = pos.reshape(2 * nt, 1, tq)
    grid_spec = pltpu.PrefetchScalarGridSpec(
        num_scalar_prefetch=1,
        grid=(nt,),
        in_specs=[pl.BlockSpec((1, 1, tq), lambda t, ends: (t, 0, 0), memory_space=pltpu.SMEM),
                  pl.BlockSpec((1, 1, tq), lambda t, ends: (t + nt, 0, 0), memory_space=pltpu.SMEM),
                  pl.BlockSpec((tq, d), lambda t, ends: (t, 0))],
        out_specs=pl.BlockSpec(memory_space=pl.ANY),
        scratch_shapes=[pltpu.VMEM((tm, d), F32), pltpu.SemaphoreType.DMA(()), pltpu.SemaphoreType.DMA(())],
    )
    return pl.pallas_call(
        functools.partial(_dispatch_kernel, TQ=tq, TM=tm),
        grid_spec=grid_spec,
        out_shape=jax.ShapeDtypeStruct((p, d), F32),
        compiler_params=_params("arbitrary"),
        name="dispatch",
    )(ends, pos3, pos3, x)


def _moe_ffn_kernel(te_ref, nu_ref, x_ref, g_ref, w1_ref, w3_ref, w2_ref, o_ref, acc, *, tf):
    t = pl.program_id(0)

    @pl.when(t < nu_ref[0])
    def _():
        h = _rms(x_ref[...], g_ref[...]).astype(BF16)
        for i, f0 in enumerate(range(0, D_FF, tf)):
            a1 = jnp.dot(h, w1_ref[0, :, f0:f0 + tf], preferred_element_type=F32)
            a3 = jnp.dot(h, w3_ref[0, :, f0:f0 + tf], preferred_element_type=F32)
            act = (a1 * jax.nn.sigmoid(a1) * a3).astype(BF16)
            part = jnp.dot(act, w2_ref[0, f0:f0 + tf, :], preferred_element_type=F32)
            if i == 0:
                acc[...] = part
            elif f0 + tf < D_FF:
                acc[...] += part
            else:
                o_ref[...] = acc[...] + part

    @pl.when(t >= nu_ref[0])
    def _():
        o_ref[...] = jnp.zeros_like(o_ref)


def _moe_ffn(xs, g, tile_expert, n_used, w1, w3, w2, tm):
    p, d = xs.shape
    dff = w1.shape[2]
    grid_spec = pltpu.PrefetchScalarGridSpec(
        num_scalar_prefetch=2,
        grid=(p // tm,),
        in_specs=[
            pl.BlockSpec((tm, d), lambda t, te, nu: (jnp.minimum(t, nu[0] - 1), 0)),
            pl.BlockSpec((1, d), lambda t, te, nu: (0, 0)),
            pl.BlockSpec((1, d, dff), lambda t, te, nu: (te[t], 0, 0)),
            pl.BlockSpec((1, d, dff), lambda t, te, nu: (te[t], 0, 0)),
            pl.BlockSpec((1, dff, d), lambda t, te, nu: (te[t], 0, 0)),
        ],
        out_specs=pl.BlockSpec((tm, d), lambda t, te, nu: (t, 0)),
        scratch_shapes=[pltpu.VMEM((tm, d), F32)],
    )
    return pl.pallas_call(
        functools.partial(_moe_ffn_kernel, tf=256),
        grid_spec=grid_spec,
        out_shape=jax.ShapeDtypeStruct((p, d), F32),
        compiler_params=_params("arbitrary"),
        name="moe_ffn",
    )(tile_expert, n_used, xs, g, w1, w3, w2)


def _combine_kernel(pos1_ref, pos2_ref, pos1n_ref, pos2n_ref, x_ref, gsel_ref, gf_ref, ys_ref, o_ref,
                    buf, sems, *, TQ):
    t = pl.program_id(0)
    slot = t % 2

    def row_copy(r, pos_ref, s, k):
        return pltpu.make_async_copy(ys_ref.at[pl.ds(pos_ref[0, 0, r], 1)], buf.at[s, k, pl.ds(r, 1)], sems.at[s])

    def issue(p1_ref, p2_ref, s):
        def body(r, carry):
            row_copy(r, p1_ref, s, 0).start(priority=0)
            row_copy(r, p2_ref, s, 1).start(priority=1)
            return carry

        lax.fori_loop(0, TQ, body, 0, unroll=8)

    @pl.when(t == 0)
    def _():
        issue(pos1_ref, pos2_ref, 0)

    @pl.when(t + 1 < pl.num_programs(0))
    def _():
        issue(pos1n_ref, pos2n_ref, 1 - slot)

    def drain(r, carry):
        row_copy(r, pos1_ref, slot, 0).wait()
        row_copy(r, pos2_ref, slot, 1).wait()
        return carry

    lax.fori_loop(0, TQ, drain, 0, unroll=8)
    gs = gsel_ref[...]
    y = x_ref[...] + gs[:, 0:1] * buf[slot, 0] + gs[:, 1:2] * buf[slot, 1]
    o_ref[...] = _rms(y, gf_ref[...])


def _combine(x, ys, pos, gsel, g_final):
    n, d = x.shape
    tq = min(n, DISPATCH_ROWS)
    nt = n // tq
    pos3 = pos.reshape(2 * nt, 1, tq)
    smem = lambda f: pl.BlockSpec((1, 1, tq), f, memory_space=pltpu.SMEM)
    nxt = lambda t: jnp.minimum(t + 1, nt - 1)
    return pl.pallas_call(
        functools.partial(_combine_kernel, TQ=tq),
        grid=(nt,),
        in_specs=[smem(lambda t: (t, 0, 0)), smem(lambda t: (t + nt, 0, 0)),
                  smem(lambda t: (nxt(t), 0, 0)), smem(lambda t: (nxt(t) + nt, 0, 0)),
                  pl.BlockSpec((tq, d), lambda t: (t, 0)),
                  pl.BlockSpec((tq, LANES), lambda t: (t, 0)),
                  pl.BlockSpec((1, d), lambda t: (0, 0)),
                  pl.BlockSpec(memory_space=pl.ANY)],
        out_specs=pl.BlockSpec((tq, d), lambda t: (t, 0)),
        out_shape=jax.ShapeDtypeStruct((n, d), F32),
        scratch_shapes=[pltpu.VMEM((2, 2, tq, d), F32), pltpu.SemaphoreType.DMA((2,))],
        compiler_params=_params("arbitrary"),
        name="combine",
    )(pos3, pos3, pos3, pos3, x, gsel, g_final, ys)


def _moe_plan(ids, tm):
    n = ids.shape[0]
    e_flat = ids.T.reshape(-1)
    experts = jnp.arange(N_EXPERTS, dtype=jnp.int32)
    onehot = (e_flat[:, None] == experts[None, :]).astype(jnp.int32)
    csum = jnp.cumsum(onehot, axis=0)
    counts = csum[-1]
    padded = jnp.maximum((counts + tm - 1) // tm, 1) * tm
    ends = jnp.cumsum(padded).astype(jnp.int32)
    pos = jnp.sum((csum - 1 + (ends - padded)[None, :]) * onehot, axis=1).astype(jnp.int32)
    p = 2 * n + N_EXPERTS * tm
    n_used = ends[-1] // tm
    tile_start = jnp.minimum(jnp.arange(p // tm, dtype=jnp.int32) * tm, ends[-1] - 1)
    tile_expert = jnp.sum((tile_start[:, None] >= ends[None, :]).astype(jnp.int32), axis=1)
    return pos, ends, tile_expert, n_used.reshape(1), p


def _ffn_kernel(x_ref, g_ref, w1_ref, w3_ref, w2_ref, o_ref, h_scr, acc):
    f = pl.program_id(1)

    @pl.when(f == 0)
    def _():
        h_scr[...] = _rms(x_ref[...], g_ref[...]).astype(BF16)
        acc[...] = x_ref[...]

    h = h_scr[...]
    a1 = jnp.dot(h, w1_ref[...], preferred_element_type=F32)
    a3 = jnp.dot(h, w3_ref[...], preferred_element_type=F32)
    act = a1 * jax.nn.sigmoid(a1) * a3
    acc[...] += jnp.dot(act.astype(BF16), w2_ref[...], preferred_element_type=F32)

    @pl.when(f == pl.num_programs(1) - 1)
    def _():
        o_ref[...] = acc[...]


def _ffn(x, g, w1, w3, w2):
    n, d = x.shape
    dff = w1.shape[1]
    tm = min(n, 1024)
    tf = 256
    return pl.pallas_call(
        _ffn_kernel,
        grid=(n // tm, dff // tf),
        in_specs=[pl.BlockSpec((tm, d), lambda i, f: (i, 0)),
                  pl.BlockSpec((1, d), lambda i, f: (0, 0)),
                  pl.BlockSpec((d, tf), lambda i, f: (0, f)),
                  pl.BlockSpec((d, tf), lambda i, f: (0, f)),
                  pl.BlockSpec((tf, d), lambda i, f: (f, 0))],
        out_specs=pl.BlockSpec((tm, d), lambda i, f: (i, 0)),
        out_shape=jax.ShapeDtypeStruct((n, d), F32),
        scratch_shapes=[pltpu.VMEM((tm, d), BF16), pltpu.VMEM((tm, d), F32)],
        compiler_params=_params("parallel", "arbitrary"),
        name="ffn",
    )(x, g, w1, w3, w2)


def _prep_weights(p):
    w_in = p["ab_w_in"][0]
    wa, wb = w_in[:, :A_COLS], w_in[:, A_COLS:]
    pad = jnp.zeros((D_MODEL, LANES - GLA_GATE_RANK), F32)
    w_in_p = jnp.concatenate(
        [wa[:, :2 * GLA_QK + A_WIDTH], wa[:, 2 * GLA_QK + A_WIDTH + GLA_GATE_RANK:], wb[:, :3 * B_WIDTH],
         wa[:, 2 * GLA_QK + A_WIDTH:2 * GLA_QK + A_WIDTH + GLA_GATE_RANK], pad, wb[:, 3 * B_WIDTH:]], axis=1)
    row = lambda v: v.reshape(1, -1).astype(F32)
    mu = p["rwkv_mu"][0]
    z64 = jnp.zeros((64, B_WIDTH), F32)
    w = dict(
        norm_mix=p["norm_mix"], norm_ffn=p["norm_ffn"], norm_final=row(p["norm_final"]),
        w_in=w_in_p.astype(BF16),
        w_out_a=p["ab_w_out"][0][:A_WIDTH].astype(BF16), w_out_b=p["ab_w_out"][0][A_WIDTH:].astype(BF16),
        gla_w2=jnp.concatenate([p["gla_w_gate2"][0], jnp.zeros((LANES - GLA_GATE_RANK, GLA_QK), F32)], axis=0),
        gla_bg=row(p["gla_b_gate"][0]), gla_norm=row(p["gla_norm"][0]),
        rwkv=(row(mu[:B_WIDTH]), row(mu[B_WIDTH:2 * B_WIDTH]), row(mu[2 * B_WIDTH:3 * B_WIDTH]),
              row(mu[3 * B_WIDTH:3 * B_WIDTH + LANES]), row(mu[3 * B_WIDTH + LANES:]),
              row(p["rwkv_w0"][0]), jnp.concatenate([p["rwkv_w2"][0], z64], axis=0),
              row(p["rwkv_a0"][0]), jnp.concatenate([z64, p["rwkv_a2"][0]], axis=0),
              p["rwkv_g2"][0], row(p["rwkv_k_k"][0]), row(p["rwkv_k_a"][0]), row(p["rwkv_r_k"][0]),
              row(p["rwkv_gn_g"][0]), row(p["rwkv_gn_b"][0])),
        ret_w_in=p["ret_w_in"][0].astype(BF16), ret_w_out=p["ret_w_out"][0].astype(BF16),
        ret_gn_g=row(p["ret_gn_g"][0]), ret_gn_b=row(p["ret_gn_b"][0]),
        ffn_w1=p["ffn_w1"][0].astype(BF16), ffn_w3=p["ffn_w3"][0].astype(BF16), ffn_w2=p["ffn_w2"][0].astype(BF16),
        router=jnp.concatenate([p["moe_router"][0], jnp.zeros((D_MODEL, LANES - N_EXPERTS), F32)], axis=1),
        moe_w1=p["moe_w1"][0].astype(BF16), moe_w3=p["moe_w3"][0].astype(BF16), moe_w2=p["moe_w2"][0].astype(BF16),
    )
    return w


def _trunk(x, pos0, st_gla, st_rwkv, st_shift, st_ret, w):
    B, T, D = x.shape
    n = B * T
    x0 = x.reshape(n, D)
    c_ab = min(CHUNK, T)
    c_ret = min(256, T)

    u = _norm_matmul(x0, w["norm_mix"][0:1], w["w_in"], tn=UW // 3, out_dtype=F32)
    oa, s_gla = _gla(u, st_gla, w["gla_w2"], w["gla_bg"], w["gla_norm"], B, T, c_ab)
    h0 = st_rwkv.transpose(0, 3, 1, 2).reshape(B, RWKV_N, B_WIDTH)
    yb, h_fin = _rwkv(u, st_shift, h0, w["rwkv"], B, T, c_ab)
    s_rwkv = h_fin.reshape(B, RWKV_N, RWKV_HEADS, RWKV_N).transpose(0, 2, 3, 1)
    u_last = u.reshape(B, T, UW)[:, T - 1:, :]
    s_shift = jnp.concatenate([u_last[..., 3 * A_WIDTH:6 * A_WIDTH], u_last[..., UW - 2 * LANES:]], axis=-1)
    x1 = _proj_res([oa, yb], [w["w_out_a"], w["w_out_b"]], x0)
    x2 = _ffn(x1, w["norm_ffn"][0:1], w["ffn_w1"], w["ffn_w3"], w["ffn_w2"])

    ur = _norm_matmul(x2, w["norm_mix"][1:2], w["ret_w_in"], tn=1536, out_dtype=BF16)
    half = RET_DK // 2
    inv = 1.0 / (ROPE_BASE ** (jnp.arange(0, RET_DK, 2, dtype=F32) / RET_DK))
    ang = (pos0 + jnp.arange(T)).astype(F32)[:, None] * inv[None, :]
    o, s_ret = _ret(ur, jnp.cos(ang), jnp.sin(ang), st_ret, w["ret_gn_g"], w["ret_gn_b"], B, T, c_ret)
    x3 = _proj_res([o], [w["ret_w_out"]], x2)
    ids, gsel = _router(x3, w["norm_ffn"][1:2], w["router"])
    tm = min(MOE_TILE, n // 4)
    pos, ends, tile_expert, n_used, p = _moe_plan(ids[:, :2], tm)
    xs = _dispatch(x3, pos, ends, p, tm)
    ys = _moe_ffn(xs, w["norm_ffn"][1:2], tile_expert, n_used, w["moe_w1"], w["moe_w3"], w["moe_w2"], tm)
    y = _combine(x3, ys, pos, gsel, w["norm_final"])
    return y.reshape(B, T, D), s_gla[None], s_rwkv[None], s_shift[None], s_ret[None]


def kernel(x_prompt, x_sample, state_gla, state_rwkv, state_shift, state_ret, norm_mix, norm_ffn, norm_final, ab_w_in, ab_w_out, gla_w_gate2, gla_b_gate, gla_norm, rwkv_mu, rwkv_w0, rwkv_w2, rwkv_a0, rwkv_a2, rwkv_g2, rwkv_k_k, rwkv_k_a, rwkv_r_k, rwkv_gn_g, rwkv_gn_b, ret_w_in, ret_gn_g, ret_gn_b, ret_w_out, ffn_w1, ffn_w3, ffn_w2, moe_router, moe_w1, moe_w3, moe_w2):
    p = dict(norm_mix=norm_mix, norm_ffn=norm_ffn, norm_final=norm_final, ab_w_in=ab_w_in, ab_w_out=ab_w_out,
             gla_w_gate2=gla_w_gate2, gla_b_gate=gla_b_gate, gla_norm=gla_norm, rwkv_mu=rwkv_mu, rwkv_w0=rwkv_w0,
             rwkv_w2=rwkv_w2, rwkv_a0=rwkv_a0, rwkv_a2=rwkv_a2, rwkv_g2=rwkv_g2, rwkv_k_k=rwkv_k_k,
             rwkv_k_a=rwkv_k_a, rwkv_r_k=rwkv_r_k, rwkv_gn_g=rwkv_gn_g, rwkv_gn_b=rwkv_gn_b, ret_w_in=ret_w_in,
             ret_gn_g=ret_gn_g, ret_gn_b=ret_gn_b, ret_w_out=ret_w_out, ffn_w1=ffn_w1, ffn_w3=ffn_w3, ffn_w2=ffn_w2,
             moe_router=moe_router, moe_w1=moe_w1, moe_w3=moe_w3, moe_w2=moe_w2)
    w = _prep_weights(p)
    bp, tp = x_prompt.shape[0], x_prompt.shape[1]
    dt = x_prompt.dtype
    z_gla = jnp.zeros((bp, GLA_HEADS, GLA_DK, GLA_DV), dt)
    z_rwkv = jnp.zeros((bp, RWKV_HEADS, RWKV_N, RWKV_N), dt)
    z_shift = jnp.zeros((bp, 1, B_COLS), dt)
    z_ret = jnp.zeros((bp, RET_HEADS, RET_DK, RET_DV), dt)
    past = 2048
    yp = _trunk(x_prompt, 0, z_gla, z_rwkv, z_shift, z_ret, w)
    ys = _trunk(x_sample, past, state_gla[0], state_rwkv[0], state_shift[0], state_ret[0], w)
    return (yp[0], ys[0], yp[1], yp[2], yp[3], yp[4], ys[1], ys[2], ys[3], ys[4])
```

```python
import functools
import math

import jax
import jax.numpy as jnp
from jax import lax
from jax.experimental import pallas as pl
from jax.experimental.pallas import tpu as pltpu

F32 = jnp.float32
BF16 = jnp.bfloat16

D_MODEL = 1024
EPS = 1e-6
GLA_HEADS, GLA_DK, GLA_DV = 4, 64, 128
GLA_QK = GLA_HEADS * GLA_DK
A_WIDTH = GLA_HEADS * GLA_DV
GLA_GATE_RANK = 16
GLA_TAU = 16.0
RWKV_HEADS, RWKV_N = 8, 64
B_WIDTH = RWKV_HEADS * RWKV_N
RWKV_GN_EPS = 64e-5
B_COLS = 3 * B_WIDTH + 64 + 64 + 128
A_COLS = 2 * GLA_QK + A_WIDTH + GLA_GATE_RANK + A_WIDTH
RET_HEADS = 4
RET_DK = D_MODEL // RET_HEADS
RET_DV = 2 * D_MODEL // RET_HEADS
RET_QK = RET_HEADS * RET_DK
RET_V = RET_HEADS * RET_DV
RET_GN_EPS = 1e-5
ROPE_BASE = 10000.0
D_FF = 2816
N_EXPERTS = 8
CHUNK = 64
SUB = 16
LANES = 128
GROUP = 4
MIXER_BATCH = 4
MOE_TILE = 512
DISPATCH_ROWS = 512
UW = 3456
VMEM_LIMIT = 56 * 1024 * 1024

NN = (((1,), (0,)), ((), ()))
NT = (((1,), (1,)), ((), ()))
TN = (((0,), (0,)), ((), ()))


def _dot(a, b, dims=NN):
    return lax.dot_general(a.astype(BF16), b.astype(BF16), dims, preferred_element_type=F32)


def _split(x):
    hi = x.astype(BF16)
    lo = (x - hi.astype(F32)).astype(BF16)
    return hi, lo


def _dot3(a, b, dims=NN):
    ah, al = _split(a)
    bh, bl = _split(b)
    dg = lambda x, y: lax.dot_general(x, y, dims, preferred_element_type=F32)
    return dg(ah, bh) + dg(ah, bl) + dg(al, bh)


def _dot_exact_lhs(m, x, dims=NN):
    xh, xl = _split(x)
    mb = m.astype(BF16)
    dg = lambda y: lax.dot_general(mb, y, dims, preferred_element_type=F32)
    return dg(xh) + dg(xl)


def _dot_exact_rhs(x, m, dims=NN):
    xh, xl = _split(x)
    mb = m.astype(BF16)
    dg = lambda y: lax.dot_general(y, mb, dims, preferred_element_type=F32)
    return dg(xh) + dg(xl)


def _iota(shape, axis):
    return lax.broadcasted_iota(jnp.int32, shape, axis)


def _softplus(z):
    return jnp.maximum(z, 0.0) + jnp.log1p(jnp.exp(-jnp.abs(z)))


def _rms(x, g):
    return x * lax.rsqrt(jnp.mean(x * x, axis=-1, keepdims=True) + EPS) * g


def _lockstep(chains):
    chains = list(chains)
    while chains:
        alive = []
        for ch in chains:
            try:
                next(ch)
                alive.append(ch)
            except StopIteration:
                pass
        chains = alive
        yield


def _params(*sem):
    return pltpu.CompilerParams(dimension_semantics=sem, vmem_limit_bytes=VMEM_LIMIT)


def _norm_matmul_kernel(x_ref, g_ref, w_ref, o_ref, h_ref):
    @pl.when(pl.program_id(1) == 0)
    def _():
        h_ref[...] = _rms(x_ref[...], g_ref[...]).astype(BF16)

    o_ref[...] = jnp.dot(h_ref[...], w_ref[...], preferred_element_type=F32).astype(o_ref.dtype)


def _norm_matmul(x, g, w, tm, tn, out_dtype):
    n, d = x.shape
    nout = w.shape[1]
    tm = min(n, tm)
    return pl.pallas_call(
        _norm_matmul_kernel,
        grid=(n // tm, nout // tn),
        in_specs=[
            pl.BlockSpec((tm, d), lambda i, j: (i, 0)),
            pl.BlockSpec((1, d), lambda i, j: (0, 0)),
            pl.BlockSpec((d, tn), lambda i, j: (0, j)),
        ],
        out_specs=pl.BlockSpec((tm, tn), lambda i, j: (i, j)),
        out_shape=jax.ShapeDtypeStruct((n, nout), out_dtype),
        scratch_shapes=[pltpu.VMEM((tm, d), BF16)],
        compiler_params=_params("parallel", "arbitrary"),
        name="norm_matmul",
    )(x, g, w)


def _proj_res_kernel(*refs, n_in):
    a_refs = refs[:n_in]
    w_refs = refs[n_in:2 * n_in]
    res_ref, o_ref = refs[2 * n_in], refs[2 * n_in + 1]
    acc = res_ref[...]
    for a_ref, w_ref in zip(a_refs, w_refs):
        acc = acc + jnp.dot(a_ref[...].astype(BF16), w_ref[...], preferred_element_type=F32)
    o_ref[...] = acc


def _proj_res(a_list, w_list, res):
    n, d = res.shape
    tm = min(n, 512)
    n_in = len(a_list)
    in_specs = [pl.BlockSpec((tm, a.shape[1]), lambda i: (i, 0)) for a in a_list]
    in_specs += [pl.BlockSpec(w.shape, lambda i: (0, 0)) for w in w_list]
    in_specs += [pl.BlockSpec((tm, d), lambda i: (i, 0))]
    return pl.pallas_call(
        functools.partial(_proj_res_kernel, n_in=n_in),
        grid=(n // tm,),
        in_specs=in_specs,
        out_specs=pl.BlockSpec((tm, d), lambda i: (i, 0)),
        out_shape=jax.ShapeDtypeStruct((n, d), F32),
        compiler_params=_params("parallel"),
        name="proj_res",
    )(*a_list, *w_list, res)


def _gla_kernel(q_ref, k_ref, v_ref, ga_ref, og_ref, s0_ref, w2_ref, bg_ref, gn_ref,
                o_ref, sout_ref, s_scr, *, C, NB):
    keep_s = (_iota((GLA_QK, A_WIDTH), 0) >> 6) == (_iota((GLA_QK, A_WIDTH), 1) >> 7)

    @pl.when(pl.program_id(1) == 0)
    def _():
        for bb in range(NB):
            s_scr[bb] = jnp.where(keep_s, jnp.concatenate([s0_ref[bb]] * GLA_HEADS, axis=1), 0.0)

    chains = [_gla_one(q_ref.at[bb], k_ref.at[bb], v_ref.at[bb], ga_ref.at[bb], og_ref.at[bb],
                       w2_ref, bg_ref, gn_ref, o_ref.at[bb], sout_ref.at[bb], s_scr.at[bb], keep_s, C=C)
              for bb in range(NB)]
    for _ in _lockstep(chains):
        pass


def _gla_one(q_ref, k_ref, v_ref, ga_ref, og_ref, w2_ref, bg_ref, gn_ref,
             o_ref, sout_ref, s_scr, keep_s, *, C):
    x = _dot3(ga_ref[...].astype(F32), w2_ref[...]) + bg_ref[...]
    yield
    logg = (jnp.minimum(x, 0.0) - jnp.log1p(jnp.exp(-jnp.abs(x)))) * (1.0 / GLA_TAU)
    tril = _iota((C, C), 0) >= _iota((C, C), 1)
    b = _dot_exact_lhs(tril, logg)
    g_col = _dot_exact_rhs(logg, jnp.ones((C, LANES), F32), TN)
    yield
    q = q_ref[...].astype(F32) * (GLA_DK ** -0.5)
    k = k_ref[...].astype(F32)
    v = v_ref[...].astype(F32)
    s = s_scr[...]
    o_inter = _dot(q * jnp.exp(b), s)
    bl = b[C - 1:C]
    upd = _dot(k * jnp.exp(bl - b), v, TN)

    head_k = _iota((1, GLA_QK), 1) >> 6
    head_v = _iota((1, A_WIDTH), 1) >> 7
    ones_kv = keep_s.astype(BF16)
    ridx = _iota((SUB, 1), 0)

    def bd(x, head_of_lane):
        xb = x.astype(BF16)
        zero = jnp.zeros_like(xb)
        return jnp.concatenate([jnp.where(head_of_lane == h, xb, zero) for h in range(GLA_HEADS)], axis=0)

    starts = range(0, C, SUB)
    atts = {}
    for i0 in starts[1:]:
        m = b[i0 - 1:i0]
        qi, bi = q[i0:i0 + SUB], b[i0:i0 + SUB]
        atts[i0] = _dot(qi * jnp.exp(bi - m), bd(k[:i0] * jnp.exp(m - b[:i0]), head_k), NT)
    yield
    scs = {}
    for i0 in starts:
        qi, bi, ki = q[i0:i0 + SUB], b[i0:i0 + SUB], k[i0:i0 + SUB]
        cols = []
        for j in range(SUB):
            e = jnp.exp(jnp.minimum(bi - bi[j:j + 1], 0.0))
            cols.append(jnp.where(ridx >= j, qi * e * ki[j:j + 1], 0.0))
        scs[i0] = _dot(jnp.concatenate(cols, axis=0), ones_kv)
    offs = {i0: _dot(atts[i0], bd(v[:i0], head_v)) for i0 in starts[1:]}
    yield
    blocks = []
    for i0 in starts:
        vi = v[i0:i0 + SUB]
        oi = o_inter[i0:i0 + SUB]
        if i0 > 0:
            oi = oi + offs[i0]
        for j in range(SUB):
            oi = oi + scs[i0][j * SUB:(j + 1) * SUB] * vi[j:j + 1]
        blocks.append(oi)
    o = jnp.concatenate(blocks, axis=0)

    e_col = jnp.concatenate([jnp.exp(g_col)] * (A_WIDTH // LANES), axis=1)
    s_new = e_col * s + jnp.where(keep_s, upd, 0.0)
    s_scr[...] = s_new
    sout_ref[...] = (s_new[:, 0:GLA_DV] + s_new[:, GLA_DV:2 * GLA_DV]
                     + s_new[:, 2 * GLA_DV:3 * GLA_DV] + s_new[:, 3 * GLA_DV:4 * GLA_DV])

    outs = [_rms(o[:, h * GLA_DV:(h + 1) * GLA_DV], gn_ref[...]) for h in range(GLA_HEADS)]
    og = og_ref[...].astype(F32)
    o_ref[...] = (jnp.concatenate(outs, axis=1) * (og * jax.nn.sigmoid(og))).astype(o_ref.dtype)


def _gla(u, s0, w2p, bg, gn, B, T, C):
    nc = T // C
    NB = MIXER_BATCH
    s0 = s0.reshape(B, GLA_QK, GLA_DV)
    u = u.reshape(B, T, UW)
    o, s_fin = pl.pallas_call(
        functools.partial(_gla_kernel, C=C, NB=NB),
        grid=(B // NB, nc),
        in_specs=[
            pl.BlockSpec((NB, C, GLA_QK), lambda b, c: (b, c, 0)),
            pl.BlockSpec((NB, C, GLA_QK), lambda b, c: (b, c, 1)),
            pl.BlockSpec((NB, C, A_WIDTH), lambda b, c: (b, c, 1)),
            pl.BlockSpec((NB, C, LANES), lambda b, c: (b, c, 24)),
            pl.BlockSpec((NB, C, A_WIDTH), lambda b, c: (b, c, 2)),
            pl.BlockSpec((NB, GLA_QK, GLA_DV), lambda b, c: (b, 0, 0)),
            pl.BlockSpec((LANES, GLA_QK), lambda b, c: (0, 0)),
            pl.BlockSpec((1, GLA_QK), lambda b, c: (0, 0)),
            pl.BlockSpec((1, GLA_DV), lambda b, c: (0, 0)),
        ],
        out_specs=[
            pl.BlockSpec((NB, C, A_WIDTH), lambda b, c: (b, c, 0)),
            pl.BlockSpec((NB, GLA_QK, GLA_DV), lambda b, c: (b, 0, 0)),
        ],
        out_shape=[
            jax.ShapeDtypeStruct((B, T, A_WIDTH), BF16),
            jax.ShapeDtypeStruct((B, GLA_QK, GLA_DV), F32),
        ],
        scratch_shapes=[pltpu.VMEM((NB, GLA_QK, A_WIDTH), F32)],
        compiler_params=_params("parallel", "arbitrary"),
        name="gla",
    )(u, u, u, u, u, s0, w2p, bg, gn)
    return o.reshape(B * T, A_WIDTH), s_fin.reshape(B, GLA_HEADS, GLA_DK, GLA_DV)


N_RWKV_PER_BATCH = (11, 2, 6)
N_RWKV_SHARED = 15


def _rwkv_kernel(*refs, C, NB):
    n_in, n_out, n_scr = N_RWKV_PER_BATCH
    ins, shared = refs[:n_in], refs[n_in:n_in + N_RWKV_SHARED]
    outs = refs[n_in + N_RWKV_SHARED:n_in + N_RWKV_SHARED + n_out]
    scr = refs[n_in + N_RWKV_SHARED + n_out:]
    GW = GROUP * RWKV_N
    keep_h = (_iota((GW, GW), 0) >> 6) == (_iota((GW, GW), 1) >> 6)

    @pl.when(pl.program_id(1) == 0)
    def _():
        h0_ref, h_scr = ins[10], scr[5]
        for bb in range(NB):
            for prev, sh in zip(scr[:5], ins[5:10]):
                prev[bb] = sh[bb]
            h0 = h0_ref[bb]
            for g in range(RWKV_HEADS // GROUP):
                t = jnp.concatenate([h0[:, g * GW:(g + 1) * GW]] * GROUP, axis=0)
                h_scr[bb, g] = jnp.where(keep_h, t, 0.0)

    at = lambda rs, bb: [r.at[bb] for r in rs]
    chains = [_rwkv_one(*at(ins[:5], bb), *shared, *at(outs, bb), *at(scr, bb), keep_h, C=C) for bb in range(NB)]
    for _ in _lockstep(chains):
        pass


def _rwkv_one(r_ref, k_ref, v_ref, wa_ref, gl_ref,
              mu_r, mu_k, mu_v, mu_wa, mu_gl, w0_ref, w2_ref, a0_ref, a2_ref, g2_ref,
              kk_ref, ka_ref, rk_ref, gng_ref, gnb_ref,
              y_ref, hout_ref,
              pr, pk, pv, pwa, pgl, h_scr, keep_h, *, C):
    GW = GROUP * RWKV_N
    CH = GROUP * C
    logc = C.bit_length() - 1
    n_groups = RWKV_HEADS // GROUP

    def head_mask(rows, cols, row_shift, col_of_lane):
        keep = (_iota((rows, cols), 0) >> row_shift) == col_of_lane(_iota((rows, cols), 1))
        return keep.astype(F32).astype(BF16)

    keep_rows = head_mask(CH, GW, logc, lambda l: l >> 6)
    keep_sq = head_mask(CH, CH, logc, lambda l: l >> logc)

    def bd(x, keep):
        return jnp.concatenate([x.astype(BF16)] * GROUP, axis=0) * keep

    rid = _iota((C, 1), 0)

    def shift(x_ref, prev, mu):
        x = x_ref[...].astype(F32)
        xp = jnp.where(rid == 0, prev[...], pltpu.roll(x, 1, axis=0))
        prev[...] = x[C - 1:C]
        return x + mu[...] * (xp - x)

    rs = shift(r_ref, pr, mu_r)
    ks = shift(k_ref, pk, mu_k)
    vs = shift(v_ref, pv, mu_v)
    was = shift(wa_ref, pwa, mu_wa)
    gls = shift(gl_ref, pgl, mu_gl)

    dec = -_softplus(-(w0_ref[...] + _dot3(jnp.tanh(was), w2_ref[...]))) - 0.5
    logw = -jnp.exp(dec)
    a = jax.nn.sigmoid(a0_ref[...] + _dot(was, a2_ref[...]))
    gate = _dot(jax.nn.sigmoid(gls), g2_ref[...])
    yield
    ones_h = keep_h.astype(F32).astype(BF16)

    def head_sum(x):
        return jnp.concatenate([_dot(x[:, g * GW:(g + 1) * GW], ones_h) for g in range(n_groups)], axis=1)

    kkr = ks * kk_ref[...]
    kk = kkr / jnp.maximum(jnp.sqrt(head_sum(kkr * kkr)), 1e-12)
    km = ks * (1.0 + (a - 1.0) * ka_ref[...])
    beta = kk * a
    tril = _iota((C, C), 0) >= _iota((C, C), 1)
    gc = _dot_exact_lhs(tril, logw)
    g_end = gc[C - 1:C]
    kap = kk * jnp.exp(gc - logw)
    rt = rs * jnp.exp(gc)
    e_neg = jnp.exp(-gc)
    b_inv = beta * e_neg
    k_inv = km * e_neg
    e_end = jnp.exp(g_end - gc)
    k_hat = km * e_end
    b_hat = beta * e_end

    col_s = _iota((C, CH), 1) & (C - 1)
    row_t = _iota((C, CH), 0)
    strict = row_t > col_s
    incl = row_t >= col_s
    same_sub = (row_t >> 4) == (col_s >> 4)
    eye_cat = (row_t == col_s).astype(F32)
    ones_c = jnp.ones((C, LANES), F32)

    yield
    ys = [None] * n_groups

    def group(g):
        L = slice(g * GW, (g + 1) * GW)
        p = jnp.concatenate([kap[:, L], rt[:, L]], axis=0)
        zb = _dot(p, bd(b_inv[:, L], keep_rows), NT)
        zk = _dot(p, bd(k_inv[:, L], keep_rows), NT)
        yield
        a_b = jnp.where(strict, zb[:C], 0.0)
        b_b = jnp.where(incl, zb[C:], 0.0)
        a_k = jnp.where(strict, zk[:C], 0.0)
        b_k = jnp.where(incl, zk[C:], 0.0)
        bd_v = bd(vs[:, L], keep_rows)
        akv = _dot(a_k, bd_v)

        d_blk = jnp.where(same_sub, a_b, 0.0)
        x1 = -d_blk
        p1 = eye_cat + x1
        x2 = _dot(x1, bd(x1, keep_sq))
        hbd = h_scr[g]
        ph = _dot(p, hbd)
        yield
        rhs = ph[:C] + akv
        r2 = _dot(jnp.concatenate([x2, p1], axis=0), bd(x2, keep_sq))
        yield
        x4, p2 = r2[:C], p1 + r2[C:]
        r4 = _dot(jnp.concatenate([x4, p2], axis=0), bd(x4, keep_sq))
        yield
        x8, p3 = r4[:C], p2 + r4[C:]
        t_d = p3 + _dot(p3, bd(x8, keep_sq))
        yield

        n1 = _dot(t_d, bd(a_b - d_blk, keep_sq))
        s1 = _dot(t_d, bd(rhs, keep_rows))
        yield
        n2 = _dot(n1, bd(n1, keep_sq))
        yield
        s2 = s1 + _dot(n2, bd(s1, keep_rows))
        yield
        u = s2 - _dot(n1, bd(s2, keep_rows))
        yield
        y = ph[C:] + _dot(b_k, bd_v) - _dot(b_b, bd(u, keep_rows))
        m = _dot(k_hat[:, L], vs[:, L], TN) - _dot(b_hat[:, L], u, TN)
        g_col = _dot_exact_rhs(logw[:, L], ones_c, TN)[:, :1]
        yield
        h_new = jnp.exp(g_col) * hbd + jnp.where(keep_h, m, 0.0)
        h_scr[g] = h_new
        hout_ref[:, L] = (h_new[0:RWKV_N] + h_new[RWKV_N:2 * RWKV_N]
                          + h_new[2 * RWKV_N:3 * RWKV_N] + h_new[3 * RWKV_N:4 * RWKV_N])
        ys[g] = y

    yield from _lockstep([group(g) for g in range(n_groups)])
    y = jnp.concatenate(ys, axis=1)

    inv_n = 1.0 / RWKV_N
    mu = head_sum(y) * inv_n
    d = y - mu
    var = head_sum(d * d) * inv_n
    yn = d * lax.rsqrt(var + RWKV_GN_EPS) * gng_ref[...] + gnb_ref[...]
    bonus = head_sum(rs * km * rk_ref[...]) * vs
    y_ref[...] = ((yn + bonus) * gate).astype(y_ref.dtype)


def _rwkv(u, shift0, h0, wts, B, T, C):
    nc = T // C
    NB = MIXER_BATCH
    wide = lambda blk: pl.BlockSpec((NB, C, B_WIDTH), lambda b, c: (b, c, blk))
    narrow = lambda blk: pl.BlockSpec((NB, C, LANES), lambda b, c: (b, c, blk))
    sh = lambda w, blk: pl.BlockSpec((NB, 1, w), lambda b, c: (b, 0, blk))
    const = lambda shape: pl.BlockSpec(shape, lambda b, c: (0,) * len(shape))
    assert len(wts) == N_RWKV_SHARED
    in_specs = [wide(3), wide(4), wide(5), narrow(25), narrow(26),
                sh(B_WIDTH, 0), sh(B_WIDTH, 1), sh(B_WIDTH, 2), sh(LANES, 12), sh(LANES, 13),
                pl.BlockSpec((NB, RWKV_N, B_WIDTH), lambda b, c: (b, 0, 0))]
    in_specs += [const(w.shape) for w in wts]
    GW = GROUP * RWKV_N
    u = u.reshape(B, T, UW)
    y, h_fin = pl.pallas_call(
        functools.partial(_rwkv_kernel, C=C, NB=NB),
        grid=(B // NB, nc),
        in_specs=in_specs,
        out_specs=[
            pl.BlockSpec((NB, C, B_WIDTH), lambda b, c: (b, c, 0)),
            pl.BlockSpec((NB, RWKV_N, B_WIDTH), lambda b, c: (b, 0, 0)),
        ],
        out_shape=[
            jax.ShapeDtypeStruct((B, T, B_WIDTH), BF16),
            jax.ShapeDtypeStruct((B, RWKV_N, B_WIDTH), F32),
        ],
        scratch_shapes=[
            pltpu.VMEM((NB, 1, B_WIDTH), F32), pltpu.VMEM((NB, 1, B_WIDTH), F32), pltpu.VMEM((NB, 1, B_WIDTH), F32),
            pltpu.VMEM((NB, 1, LANES), F32), pltpu.VMEM((NB, 1, LANES), F32),
            pltpu.VMEM((NB, RWKV_HEADS // GROUP, GW, GW), F32),
        ],
        compiler_params=_params("parallel", "arbitrary"),
        name="rwkv7",
    )(u, u, u, u, u, shift0, shift0, shift0, shift0, shift0, h0, *wts)
    return y.reshape(B * T, B_WIDTH), h_fin


def _ret_kernel(q_ref, k_ref, v_ref, g_ref, cos_ref, sin_ref, r0_ref, gng_ref, gnb_ref,
                o_ref, rout_ref, r_scr, *, C):
    c = pl.program_id(1)

    @pl.when(c == 0)
    def _():
        r_scr[...] = r0_ref[0]

    cos = cos_ref[...]
    sin = sin_ref[...]
    half = RET_DK // 2

    def rope(x):
        x1, x2 = x[:, :half], x[:, half:]
        return jnp.concatenate([x1 * cos - x2 * sin, x1 * sin + x2 * cos], axis=1)

    ri = _iota((C, C), 0)
    ci = _iota((C, C), 1)
    causal = ri >= ci
    diff = (ri - ci).astype(F32)
    pos1 = (_iota((C, 1), 0) + 1).astype(F32)
    outs = []
    for h in range(RET_HEADS):
        lg = math.log1p(-(2.0 ** (-5.0 - h)))
        q = rope(q_ref[:, h * RET_DK:(h + 1) * RET_DK].astype(F32))
        k = rope(k_ref[:, h * RET_DK:(h + 1) * RET_DK].astype(F32)) * (RET_DK ** -0.5)
        v = v_ref[:, h * RET_DV:(h + 1) * RET_DV]
        r = r_scr[h]
        decay = jnp.where(causal, jnp.exp(diff * lg), 0.0)
        att = _dot(q, k, NT) * decay
        o = _dot(att, v) + _dot(q, r) * jnp.exp(pos1 * lg)
        r_scr[h] = math.exp(C * lg) * r + _dot(k * jnp.exp((C - pos1) * lg), v, TN)
        mu = jnp.mean(o, axis=-1, keepdims=True)
        d = o - mu
        var = jnp.mean(d * d, axis=-1, keepdims=True)
        outs.append(d * lax.rsqrt(var + RET_GN_EPS))
    g = g_ref[...].astype(F32)
    o = jnp.concatenate(outs, axis=1) * gng_ref[...] + gnb_ref[...]
    o_ref[...] = ((g * jax.nn.sigmoid(g)) * o).astype(o_ref.dtype)
    rout_ref[0] = r_scr[...]


def _ret(u, cos, sin, r0, gn_g, gn_b, B, T, C):
    nc = T // C
    row = lambda b, c: b * nc + c
    return pl.pallas_call(
        functools.partial(_ret_kernel, C=C),
        grid=(B, nc),
        in_specs=[
            pl.BlockSpec((C, RET_QK), lambda b, c: (row(b, c), 0)),
            pl.BlockSpec((C, RET_QK), lambda b, c: (row(b, c), 1)),
            pl.BlockSpec((C, RET_V), lambda b, c: (row(b, c), 1)),
            pl.BlockSpec((C, RET_V), lambda b, c: (row(b, c), 2)),
            pl.BlockSpec((C, RET_DK // 2), lambda b, c: (c, 0)),
            pl.BlockSpec((C, RET_DK // 2), lambda b, c: (c, 0)),
            pl.BlockSpec((1, RET_HEADS, RET_DK, RET_DV), lambda b, c: (b, 0, 0, 0)),
            pl.BlockSpec((1, RET_V), lambda b, c: (0, 0)),
            pl.BlockSpec((1, RET_V), lambda b, c: (0, 0)),
        ],
        out_specs=[
            pl.BlockSpec((C, RET_V), lambda b, c: (row(b, c), 0)),
            pl.BlockSpec((1, RET_HEADS, RET_DK, RET_DV), lambda b, c: (b, 0, 0, 0)),
        ],
        out_shape=[
            jax.ShapeDtypeStruct((B * T, RET_V), BF16),
            jax.ShapeDtypeStruct((B, RET_HEADS, RET_DK, RET_DV), F32),
        ],
        scratch_shapes=[pltpu.VMEM((RET_HEADS, RET_DK, RET_DV), F32)],
        compiler_params=_params("parallel", "arbitrary"),
        name="retention",
    )(u, u, u, u, cos, sin, r0, gn_g, gn_b)


def _router_kernel(x_ref, g_ref, wr_ref, ids_ref, gsel_ref):
    h = _rms(x_ref[...], g_ref[...])
    logits = _dot3(h, wr_ref[...])
    lane = _iota(logits.shape, 1)
    valid = lane < N_EXPERTS
    logits = jnp.where(valid, logits, -jnp.inf)
    m = jnp.max(logits, axis=-1, keepdims=True)
    e = jnp.where(valid, jnp.exp(logits - m), 0.0)
    p = e / jnp.sum(e, axis=-1, keepdims=True)
    big = jnp.int32(LANES)
    p1 = jnp.max(p, axis=-1, keepdims=True)
    i1 = jnp.min(jnp.where(p == p1, lane, big), axis=-1, keepdims=True)
    rest = jnp.where((lane == i1) | ~valid, -1.0, p)
    p2 = jnp.max(rest, axis=-1, keepdims=True)
    i2 = jnp.min(jnp.where(rest == p2, lane, big), axis=-1, keepdims=True)
    tot = p1 + p2
    ids_ref[...] = jnp.where(lane == 0, i1, i2)
    gsel_ref[...] = jnp.where(lane == 0, p1 / tot, p2 / tot)


def _router(x, g, wr):
    n, d = x.shape
    tm = min(n, 512)
    return pl.pallas_call(
        _router_kernel,
        grid=(n // tm,),
        in_specs=[
            pl.BlockSpec((tm, d), lambda i: (i, 0)),
            pl.BlockSpec((1, d), lambda i: (0, 0)),
            pl.BlockSpec((d, LANES), lambda i: (0, 0)),
        ],
        out_specs=[pl.BlockSpec((tm, LANES), lambda i: (i, 0)), pl.BlockSpec((tm, LANES), lambda i: (i, 0))],
        out_shape=[jax.ShapeDtypeStruct((n, LANES), jnp.int32), jax.ShapeDtypeStruct((n, LANES), F32)],
        compiler_params=_params("parallel"),
        name="router",
    )(x, g, wr)


def _dispatch_kernel(ends_ref, pos1_ref, pos2_ref, x_ref, xs_ref, zbuf, zsem, sem, *, TQ, TM):
    t = pl.program_id(0)

    @pl.when(t == 0)
    def _():
        zbuf[...] = jnp.zeros_like(zbuf)
        tails = [pltpu.make_async_copy(zbuf, xs_ref.at[pl.ds(pl.multiple_of(ends_ref[e] - TM, TM), TM)], zsem)
                 for e in range(N_EXPERTS)]
        for cp in tails:
            cp.start()
        for cp in tails:
            cp.wait()

        def zero_tile(i, carry):
            cp = pltpu.make_async_copy(zbuf, xs_ref.at[pl.ds(pl.multiple_of(i * TM, TM), TM)], zsem)
            cp.start()
            cp.wait()
            return carry

        lax.fori_loop(ends_ref[N_EXPERTS - 1] // TM, xs_ref.shape[0] // TM, zero_tile, 0)

    def row_copy(r, pos_ref):
        return pltpu.make_async_copy(x_ref.at[pl.ds(r, 1)], xs_ref.at[pl.ds(pos_ref[0, 0, r], 1)], sem)

    def issue(r, carry):
        row_copy(r, pos1_ref).start(priority=0)
        row_copy(r, pos2_ref).start(priority=1)
        return carry

    lax.fori_loop(0, TQ, issue, 0, unroll=8)

    def drain(r, carry):
        row_copy(r, pos1_ref).wait()
        row_copy(r, pos2_ref).wait()
        return carry

    lax.fori_loop(0, TQ, drain, 0, unroll=8)


def _dispatch(x, pos, ends, p, tm):
    n, d = x.shape
    tq = min(n, DISPATCH_ROWS)
    nt = n // tq
    pos3 = pos.reshape(2 * nt, 1, tq)
    grid_spec = pltpu.PrefetchScalarGridSpec(
        num_scalar_prefetch=1,
        grid=(nt,),
        in_specs=[pl.BlockSpec((1, 1, tq), lambda t, ends: (t, 0, 0), memory_space=pltpu.SMEM),
                  pl.BlockSpec((1, 1, tq), lambda t, ends: (t + nt, 0, 0), memory_space=pltpu.SMEM),
                  pl.BlockSpec((tq, d), lambda t, ends: (t, 0))],
        out_specs=pl.BlockSpec(memory_space=pl.ANY),
        scratch_shapes=[pltpu.VMEM((tm, d), F32), pltpu.SemaphoreType.DMA(()), pltpu.SemaphoreType.DMA(())],
    )
    return pl.pallas_call(
        functools.partial(_dispatch_kernel, TQ=tq, TM=tm),
        grid_spec=grid_spec,
        out_shape=jax.ShapeDtypeStruct((p, d), F32),
        compiler_params=_params("arbitrary"),
        name="dispatch",
    )(ends, pos3, pos3, x)


def _moe_ffn_kernel(te_ref, nu_ref, x_ref, g_ref, w1_ref, w3_ref, w2_ref, o_ref, acc, *, tf, residual):
    t = pl.program_id(0)

    @pl.when(t < nu_ref[0])
    def _():
        x = x_ref[...]
        h = _rms(x, g_ref[...]).astype(BF16)
        for i, f0 in enumerate(range(0, D_FF, tf)):
            a1 = jnp.dot(h, w1_ref[0, :, f0:f0 + tf], preferred_element_type=F32)
            a3 = jnp.dot(h, w3_ref[0, :, f0:f0 + tf], preferred_element_type=F32)
            act = (a1 * jax.nn.sigmoid(a1) * a3).astype(BF16)
            part = jnp.dot(act, w2_ref[0, f0:f0 + tf, :], preferred_element_type=F32)
            if i == 0:
                acc[...] = part + x if residual else part
            elif f0 + tf < D_FF:
                acc[...] += part
            else:
                o_ref[...] = acc[...] + part

    @pl.when(t >= nu_ref[0])
    def _():
        o_ref[...] = jnp.zeros_like(o_ref)


def _ffn(x, g, w1, w3, w2):
    nt = x.shape[0] // min(x.shape[0], MOE_TILE)
    return _moe_ffn(x, g, jnp.zeros((nt,), jnp.int32), jnp.full((1,), nt, jnp.int32), w1, w3, w2,
                    min(x.shape[0], MOE_TILE), residual=True)


def _moe_ffn(xs, g, tile_expert, n_used, w1, w3, w2, tm, residual=False):
    p, d = xs.shape
    dff = w1.shape[2]
    grid_spec = pltpu.PrefetchScalarGridSpec(
        num_scalar_prefetch=2,
        grid=(p // tm,),
        in_specs=[
            pl.BlockSpec((tm, d), lambda t, te, nu: (jnp.minimum(t, nu[0] - 1), 0)),
            pl.BlockSpec((1, d), lambda t, te, nu: (0, 0)),
            pl.BlockSpec((1, d, dff), lambda t, te, nu: (te[t], 0, 0)),
            pl.BlockSpec((1, d, dff), lambda t, te, nu: (te[t], 0, 0)),
            pl.BlockSpec((1, dff, d), lambda t, te, nu: (te[t], 0, 0)),
        ],
        out_specs=pl.BlockSpec((tm, d), lambda t, te, nu: (t, 0)),
        scratch_shapes=[pltpu.VMEM((tm, d), F32)],
    )
    return pl.pallas_call(
        functools.partial(_moe_ffn_kernel, tf=256, residual=residual),
        grid_spec=grid_spec,
        out_shape=jax.ShapeDtypeStruct((p, d), F32),
        compiler_params=_params("arbitrary"),
        name="moe_ffn",
    )(tile_expert, n_used, xs, g, w1, w3, w2)


def _combine_kernel(pos1_ref, pos2_ref, pos1n_ref, pos2n_ref, x_ref, gsel_ref, gf_ref, ys_ref, o_ref,
                    buf, sems, *, TQ):
    t = pl.program_id(0)
    slot = t % 2

    def row_copy(r, pos_ref, s, k):
        return pltpu.make_async_copy(ys_ref.at[pl.ds(pos_ref[0, 0, r], 1)], buf.at[s, k, pl.ds(r, 1)], sems.at[s])

    def issue(p1_ref, p2_ref, s):
        def body(r, carry):
            row_copy(r, p1_ref, s, 0).start(priority=0)
            row_copy(r, p2_ref, s, 1).start(priority=1)
            return carry

        lax.fori_loop(0, TQ, body, 0, unroll=8)

    @pl.when(t == 0)
    def _():
        issue(pos1_ref, pos2_ref, 0)

    @pl.when(t + 1 < pl.num_programs(0))
    def _():
        issue(pos1n_ref, pos2n_ref, 1 - slot)

    def drain(r, carry):
        row_copy(r, pos1_ref, slot, 0).wait()
        row_copy(r, pos2_ref, slot, 1).wait()
        return carry

    lax.fori_loop(0, TQ, drain, 0, unroll=8)
    gs = gsel_ref[...]
    y = x_ref[...] + gs[:, 0:1] * buf[slot, 0] + gs[:, 1:2] * buf[slot, 1]
    o_ref[...] = _rms(y, gf_ref[...])


def _combine(x, ys, pos, gsel, g_final):
    n, d = x.shape
    tq = min(n, DISPATCH_ROWS)
    nt = n // tq
    pos3 = pos.reshape(2 * nt, 1, tq)
    smem = lambda f: pl.BlockSpec((1, 1, tq), f, memory_space=pltpu.SMEM)
    nxt = lambda t: jnp.minimum(t + 1, nt - 1)
    return pl.pallas_call(
        functools.partial(_combine_kernel, TQ=tq),
        grid=(nt,),
        in_specs=[smem(lambda t: (t, 0, 0)), smem(lambda t: (t + nt, 0, 0)),
                  smem(lambda t: (nxt(t), 0, 0)), smem(lambda t: (nxt(t) + nt, 0, 0)),
                  pl.BlockSpec((tq, d), lambda t: (t, 0)),
                  pl.BlockSpec((tq, LANES), lambda t: (t, 0)),
                  pl.BlockSpec((1, d), lambda t: (0, 0)),
                  pl.BlockSpec(memory_space=pl.ANY)],
        out_specs=pl.BlockSpec((tq, d), lambda t: (t, 0)),
        out_shape=jax.ShapeDtypeStruct((n, d), F32),
        scratch_shapes=[pltpu.VMEM((2, 2, tq, d), F32), pltpu.SemaphoreType.DMA((2,))],
        compiler_params=_params("arbitrary"),
        name="combine",
    )(pos3, pos3, pos3, pos3, x, gsel, g_final, ys)


def _moe_plan(ids, tm):
    n = ids.shape[0]
    e_flat = ids.T.reshape(-1)
    experts = jnp.arange(N_EXPERTS, dtype=jnp.int32)
    onehot = (e_flat[:, None] == experts[None, :]).astype(jnp.int32)
    csum = jnp.cumsum(onehot, axis=0)
    counts = csum[-1]
    padded = jnp.maximum((counts + tm - 1) // tm, 1) * tm
    ends = jnp.cumsum(padded).astype(jnp.int32)
    pos = jnp.sum((csum - 1 + (ends - padded)[None, :]) * onehot, axis=1).astype(jnp.int32)
    p = 2 * n + N_EXPERTS * tm
    n_used = ends[-1] // tm
    tile_start = jnp.minimum(jnp.arange(p // tm, dtype=jnp.int32) * tm, ends[-1] - 1)
    tile_expert = jnp.sum((tile_start[:, None] >= ends[None, :]).astype(jnp.int32), axis=1)
    return pos, ends, tile_expert, n_used.reshape(1), p


def _prep_weights(p):
    w_in = p["ab_w_in"][0]
    wa, wb = w_in[:, :A_COLS], w_in[:, A_COLS:]
    pad = jnp.zeros((D_MODEL, LANES - GLA_GATE_RANK), F32)
    w_in_p = jnp.concatenate(
        [wa[:, :2 * GLA_QK + A_WIDTH], wa[:, 2 * GLA_QK + A_WIDTH + GLA_GATE_RANK:], wb[:, :3 * B_WIDTH],
         wa[:, 2 * GLA_QK + A_WIDTH:2 * GLA_QK + A_WIDTH + GLA_GATE_RANK], pad, wb[:, 3 * B_WIDTH:]], axis=1)
    row = lambda v: v.reshape(1, -1).astype(F32)
    mu = p["rwkv_mu"][0]
    z64 = jnp.zeros((64, B_WIDTH), F32)
    w = dict(
        norm_mix=p["norm_mix"], norm_ffn=p["norm_ffn"], norm_final=row(p["norm_final"]),
        w_in=w_in_p.astype(BF16),
        w_out_a=p["ab_w_out"][0][:A_WIDTH].astype(BF16), w_out_b=p["ab_w_out"][0][A_WIDTH:].astype(BF16),
        gla_w2=jnp.concatenate([p["gla_w_gate2"][0], jnp.zeros((LANES - GLA_GATE_RANK, GLA_QK), F32)], axis=0),
        gla_bg=row(p["gla_b_gate"][0]), gla_norm=row(p["gla_norm"][0]),
        rwkv=(row(mu[:B_WIDTH]), row(mu[B_WIDTH:2 * B_WIDTH]), row(mu[2 * B_WIDTH:3 * B_WIDTH]),
              row(mu[3 * B_WIDTH:3 * B_WIDTH + LANES]), row(mu[3 * B_WIDTH + LANES:]),
              row(p["rwkv_w0"][0]), jnp.concatenate([p["rwkv_w2"][0], z64], axis=0),
              row(p["rwkv_a0"][0]), jnp.concatenate([z64, p["rwkv_a2"][0]], axis=0),
              p["rwkv_g2"][0], row(p["rwkv_k_k"][0]), row(p["rwkv_k_a"][0]), row(p["rwkv_r_k"][0]),
              row(p["rwkv_gn_g"][0]), row(p["rwkv_gn_b"][0])),
        ret_w_in=p["ret_w_in"][0].astype(BF16), ret_w_out=p["ret_w_out"][0].astype(BF16),
        ret_gn_g=row(p["ret_gn_g"][0]), ret_gn_b=row(p["ret_gn_b"][0]),
        ffn_w1=p["ffn_w1"].astype(BF16), ffn_w3=p["ffn_w3"].astype(BF16), ffn_w2=p["ffn_w2"].astype(BF16),
        router=jnp.concatenate([p["moe_router"][0], jnp.zeros((D_MODEL, LANES - N_EXPERTS), F32)], axis=1),
        moe_w1=p["moe_w1"][0].astype(BF16), moe_w3=p["moe_w3"][0].astype(BF16), moe_w2=p["moe_w2"][0].astype(BF16),
    )
    return w


def _trunk(x, pos0, st_gla, st_rwkv, st_shift, st_ret, w):
    B, T, D = x.shape
    n = B * T
    x0 = x.reshape(n, D)
    c_ab = min(CHUNK, T)
    c_ret = min(256, T)

    u = _norm_matmul(x0, w["norm_mix"][0:1], w["w_in"], tm=512, tn=UW, out_dtype=BF16)
    oa, s_gla = _gla(u, st_gla, w["gla_w2"], w["gla_bg"], w["gla_norm"], B, T, c_ab)
    h0 = st_rwkv.transpose(0, 3, 1, 2).reshape(B, RWKV_N, B_WIDTH)
    yb, h_fin = _rwkv(u, st_shift, h0, w["rwkv"], B, T, c_ab)
    s_rwkv = h_fin.reshape(B, RWKV_N, RWKV_HEADS, RWKV_N).transpose(0, 2, 3, 1)
    u_last = u.reshape(B, T, UW)[:, T - 1:, :]
    s_shift = jnp.concatenate([u_last[..., 3 * A_WIDTH:6 * A_WIDTH], u_last[..., UW - 2 * LANES:]], axis=-1).astype(F32)
    x1 = _proj_res([oa, yb], [w["w_out_a"], w["w_out_b"]], x0)
    x2 = _ffn(x1, w["norm_ffn"][0:1], w["ffn_w1"], w["ffn_w3"], w["ffn_w2"])

    ur = _norm_matmul(x2, w["norm_mix"][1:2], w["ret_w_in"], tm=1024, tn=1536, out_dtype=BF16)
    half = RET_DK // 2
    inv = 1.0 / (ROPE_BASE ** (jnp.arange(0, RET_DK, 2, dtype=F32) / RET_DK))
    ang = (pos0 + jnp.arange(T)).astype(F32)[:, None] * inv[None, :]
    o, s_ret = _ret(ur, jnp.cos(ang), jnp.sin(ang), st_ret, w["ret_gn_g"], w["ret_gn_b"], B, T, c_ret)
    x3 = _proj_res([o], [w["ret_w_out"]], x2)
    ids, gsel = _router(x3, w["norm_ffn"][1:2], w["router"])
    tm = min(MOE_TILE, n // 4)
    pos, ends, tile_expert, n_used, p = _moe_plan(ids[:, :2], tm)
    xs = _dispatch(x3, pos, ends, p, tm)
    ys = _moe_ffn(xs, w["norm_ffn"][1:2], tile_expert, n_used, w["moe_w1"], w["moe_w3"], w["moe_w2"], tm)
    y = _combine(x3, ys, pos, gsel, w["norm_final"])
    return y.reshape(B, T, D), s_gla[None], s_rwkv[None], s_shift[None], s_ret[None]


def kernel(x_prompt, x_sample, state_gla, state_rwkv, state_shift, state_ret, norm_mix, norm_ffn, norm_final, ab_w_in, ab_w_out, gla_w_gate2, gla_b_gate, gla_norm, rwkv_mu, rwkv_w0, rwkv_w2, rwkv_a0, rwkv_a2, rwkv_g2, rwkv_k_k, rwkv_k_a, rwkv_r_k, rwkv_gn_g, rwkv_gn_b, ret_w_in, ret_gn_g, ret_gn_b, ret_w_out, ffn_w1, ffn_w3, ffn_w2, moe_router, moe_w1, moe_w3, moe_w2):
    p = dict(norm_mix=norm_mix, norm_ffn=norm_ffn, norm_final=norm_final, ab_w_in=ab_w_in, ab_w_out=ab_w_out,
             gla_w_gate2=gla_w_gate2, gla_b_gate=gla_b_gate, gla_norm=gla_norm, rwkv_mu=rwkv_mu, rwkv_w0=rwkv_w0,
             rwkv_w2=rwkv_w2, rwkv_a0=rwkv_a0, rwkv_a2=rwkv_a2, rwkv_g2=rwkv_g2, rwkv_k_k=rwkv_k_k,
             rwkv_k_a=rwkv_k_a, rwkv_r_k=rwkv_r_k, rwkv_gn_g=rwkv_gn_g, rwkv_gn_b=rwkv_gn_b, ret_w_in=ret_w_in,
             ret_gn_g=ret_gn_g, ret_gn_b=ret_gn_b, ret_w_out=ret_w_out, ffn_w1=ffn_w1, ffn_w3=ffn_w3, ffn_w2=ffn_w2,
             moe_router=moe_router, moe_w1=moe_w1, moe_w3=moe_w3, moe_w2=moe_w2)
    w = _prep_weights(p)
    bp, tp = x_prompt.shape[0], x_prompt.shape[1]
    dt = x_prompt.dtype
    z_gla = jnp.zeros((bp, GLA_HEADS, GLA_DK, GLA_DV), dt)
    z_rwkv = jnp.zeros((bp, RWKV_HEADS, RWKV_N, RWKV_N), dt)
    z_shift = jnp.zeros((bp, 1, B_COLS), dt)
    z_ret = jnp.zeros((bp, RET_HEADS, RET_DK, RET_DV), dt)
    past = 2048
    yp = _trunk(x_prompt, 0, z_gla, z_rwkv, z_shift, z_ret, w)
    ys = _trunk(x_sample, past, state_gla[0], state_rwkv[0], state_shift[0], state_ret[0], w)
    return (yp[0], ys[0], yp[1], yp[2], yp[3], yp[4], ys[1], ys[2], ys[3], ys[4])
```

```python
import functools
import math

import jax
import jax.numpy as jnp
from jax import lax
from jax.experimental import pallas as pl
from jax.experimental.pallas import tpu as pltpu

F32 = jnp.float32
BF16 = jnp.bfloat16

D_MODEL = 1024
EPS = 1e-6
GLA_HEADS, GLA_DK, GLA_DV = 4, 64, 128
GLA_QK = GLA_HEADS * GLA_DK
A_WIDTH = GLA_HEADS * GLA_DV
GLA_GATE_RANK = 16
GLA_TAU = 16.0
RWKV_HEADS, RWKV_N = 8, 64
B_WIDTH = RWKV_HEADS * RWKV_N
RWKV_GN_EPS = 64e-5
B_COLS = 3 * B_WIDTH + 64 + 64 + 128
A_COLS = 2 * GLA_QK + A_WIDTH + GLA_GATE_RANK + A_WIDTH
RET_HEADS = 4
RET_DK = D_MODEL // RET_HEADS
RET_DV = 2 * D_MODEL // RET_HEADS
RET_QK = RET_HEADS * RET_DK
RET_V = RET_HEADS * RET_DV
RET_GN_EPS = 1e-5
ROPE_BASE = 10000.0
D_FF = 2816
N_EXPERTS = 8
CHUNK = 64
SUB = 16
LANES = 128
GROUP = 4
MIXER_BATCH = 4
MOE_TILE = 512
DISPATCH_ROWS = 512
UW = 3456
VMEM_LIMIT = 56 * 1024 * 1024

NN = (((1,), (0,)), ((), ()))
NT = (((1,), (1,)), ((), ()))
TN = (((0,), (0,)), ((), ()))


def _dot(a, b, dims=NN):
    return lax.dot_general(a.astype(BF16), b.astype(BF16), dims, preferred_element_type=F32)


def _split(x):
    hi = x.astype(BF16)
    lo = (x - hi.astype(F32)).astype(BF16)
    return hi, lo


def _dot3(a, b, dims=NN):
    ah, al = _split(a)
    bh, bl = _split(b)
    dg = lambda x, y: lax.dot_general(x, y, dims, preferred_element_type=F32)
    return dg(ah, bh) + dg(ah, bl) + dg(al, bh)


def _dot_exact_lhs(m, x, dims=NN):
    xh, xl = _split(x)
    mb = m.astype(BF16)
    dg = lambda y: lax.dot_general(mb, y, dims, preferred_element_type=F32)
    return dg(xh) + dg(xl)


def _dot_exact_rhs(x, m, dims=NN):
    xh, xl = _split(x)
    mb = m.astype(BF16)
    dg = lambda y: lax.dot_general(y, mb, dims, preferred_element_type=F32)
    return dg(xh) + dg(xl)


def _iota(shape, axis):
    return lax.broadcasted_iota(jnp.int32, shape, axis)


def _softplus(z):
    return jnp.maximum(z, 0.0) + jnp.log1p(jnp.exp(-jnp.abs(z)))


def _rms(x, g):
    return x * lax.rsqrt(jnp.mean(x * x, axis=-1, keepdims=True) + EPS) * g


def _lockstep(chains):
    chains = list(chains)
    while chains:
        alive = []
        for ch in chains:
            try:
                next(ch)
                alive.append(ch)
            except StopIteration:
                pass
        chains = alive
        yield


def _params(*sem):
    return pltpu.CompilerParams(dimension_semantics=sem, vmem_limit_bytes=VMEM_LIMIT)


def _norm_matmul_kernel(x_ref, g_ref, w_ref, o_ref, h_ref):
    @pl.when(pl.program_id(1) == 0)
    def _():
        h_ref[...] = _rms(x_ref[...], g_ref[...]).astype(BF16)

    o_ref[...] = jnp.dot(h_ref[...], w_ref[...], preferred_element_type=F32).astype(o_ref.dtype)


def _norm_matmul(x, g, w, tm, tn, out_dtype):
    n, d = x.shape
    nout = w.shape[1]
    tm = min(n, tm)
    return pl.pallas_call(
        _norm_matmul_kernel,
        grid=(n // tm, nout // tn),
        in_specs=[
            pl.BlockSpec((tm, d), lambda i, j: (i, 0)),
            pl.BlockSpec((1, d), lambda i, j: (0, 0)),
            pl.BlockSpec((d, tn), lambda i, j: (0, j)),
        ],
        out_specs=pl.BlockSpec((tm, tn), lambda i, j: (i, j)),
        out_shape=jax.ShapeDtypeStruct((n, nout), out_dtype),
        scratch_shapes=[pltpu.VMEM((tm, d), BF16)],
        compiler_params=_params("parallel", "arbitrary"),
        name="norm_matmul",
    )(x, g, w)


def _gla_kernel(q_ref, k_ref, v_ref, ga_ref, og_ref, s0_ref, w2_ref, bg_ref, gn_ref,
                o_ref, sout_ref, s_scr, *, C, NB):
    keep_s = (_iota((GLA_QK, A_WIDTH), 0) >> 6) == (_iota((GLA_QK, A_WIDTH), 1) >> 7)

    @pl.when(pl.program_id(1) == 0)
    def _():
        for bb in range(NB):
            s_scr[bb] = jnp.where(keep_s, jnp.concatenate([s0_ref[bb]] * GLA_HEADS, axis=1), 0.0)

    chains = [_gla_one(q_ref.at[bb], k_ref.at[bb], v_ref.at[bb], ga_ref.at[bb], og_ref.at[bb],
                       w2_ref, bg_ref, gn_ref, o_ref.at[bb], sout_ref.at[bb], s_scr.at[bb], keep_s, C=C)
              for bb in range(NB)]
    for _ in _lockstep(chains):
        pass


def _gla_one(q_ref, k_ref, v_ref, ga_ref, og_ref, w2_ref, bg_ref, gn_ref,
             o_ref, sout_ref, s_scr, keep_s, *, C):
    x = _dot3(ga_ref[...].astype(F32), w2_ref[...]) + bg_ref[...]
    yield
    logg = (jnp.minimum(x, 0.0) - jnp.log1p(jnp.exp(-jnp.abs(x)))) * (1.0 / GLA_TAU)
    tril = _iota((C, C), 0) >= _iota((C, C), 1)
    b = _dot_exact_lhs(tril, logg)
    g_col = _dot_exact_rhs(logg, jnp.ones((C, LANES), F32), TN)
    yield
    q = q_ref[...].astype(F32) * (GLA_DK ** -0.5)
    k = k_ref[...].astype(F32)
    v = v_ref[...].astype(F32)
    s = s_scr[...]
    o_inter = _dot(q * jnp.exp(b), s)
    bl = b[C - 1:C]
    upd = _dot(k * jnp.exp(bl - b), v, TN)

    head_k = _iota((1, GLA_QK), 1) >> 6
    head_v = _iota((1, A_WIDTH), 1) >> 7
    ones_kv = keep_s.astype(BF16)
    ridx = _iota((SUB, 1), 0)

    def bd(x, head_of_lane):
        xb = x.astype(BF16)
        zero = jnp.zeros_like(xb)
        return jnp.concatenate([jnp.where(head_of_lane == h, xb, zero) for h in range(GLA_HEADS)], axis=0)

    starts = range(0, C, SUB)
    atts = {}
    for i0 in starts[1:]:
        m = b[i0 - 1:i0]
        qi, bi = q[i0:i0 + SUB], b[i0:i0 + SUB]
        atts[i0] = _dot(qi * jnp.exp(bi - m), bd(k[:i0] * jnp.exp(m - b[:i0]), head_k), NT)
    yield
    scs = {}
    for i0 in starts:
        qi, bi, ki = q[i0:i0 + SUB], b[i0:i0 + SUB], k[i0:i0 + SUB]
        cols = []
        for j in range(SUB):
            e = jnp.exp(jnp.minimum(bi - bi[j:j + 1], 0.0))
            cols.append(jnp.where(ridx >= j, qi * e * ki[j:j + 1], 0.0))
        scs[i0] = _dot(jnp.concatenate(cols, axis=0), ones_kv)
    offs = {i0: _dot(atts[i0], bd(v[:i0], head_v)) for i0 in starts[1:]}
    yield
    blocks = []
    for i0 in starts:
        vi = v[i0:i0 + SUB]
        oi = o_inter[i0:i0 + SUB]
        if i0 > 0:
            oi = oi + offs[i0]
        for j in range(SUB):
            oi = oi + scs[i0][j * SUB:(j + 1) * SUB] * vi[j:j + 1]
        blocks.append(oi)
    o = jnp.concatenate(blocks, axis=0)

    e_col = jnp.concatenate([jnp.exp(g_col)] * (A_WIDTH // LANES), axis=1)
    s_new = e_col * s + jnp.where(keep_s, upd, 0.0)
    s_scr[...] = s_new
    sout_ref[...] = (s_new[:, 0:GLA_DV] + s_new[:, GLA_DV:2 * GLA_DV]
                     + s_new[:, 2 * GLA_DV:3 * GLA_DV] + s_new[:, 3 * GLA_DV:4 * GLA_DV])

    outs = [_rms(o[:, h * GLA_DV:(h + 1) * GLA_DV], gn_ref[...]) for h in range(GLA_HEADS)]
    og = og_ref[...].astype(F32)
    o_ref[...] = (jnp.concatenate(outs, axis=1) * (og * jax.nn.sigmoid(og))).astype(o_ref.dtype)


def _gla(u, s0, w2p, bg, gn, B, T, C):
    nc = T // C
    NB = MIXER_BATCH
    s0 = s0.reshape(B, GLA_QK, GLA_DV)
    u = u.reshape(B, T, UW)
    o, s_fin = pl.pallas_call(
        functools.partial(_gla_kernel, C=C, NB=NB),
        grid=(B // NB, nc),
        in_specs=[
            pl.BlockSpec((NB, C, GLA_QK), lambda b, c: (b, c, 0)),
            pl.BlockSpec((NB, C, GLA_QK), lambda b, c: (b, c, 1)),
            pl.BlockSpec((NB, C, A_WIDTH), lambda b, c: (b, c, 1)),
            pl.BlockSpec((NB, C, LANES), lambda b, c: (b, c, 24)),
            pl.BlockSpec((NB, C, A_WIDTH), lambda b, c: (b, c, 2)),
            pl.BlockSpec((NB, GLA_QK, GLA_DV), lambda b, c: (b, 0, 0)),
            pl.BlockSpec((LANES, GLA_QK), lambda b, c: (0, 0)),
            pl.BlockSpec((1, GLA_QK), lambda b, c: (0, 0)),
            pl.BlockSpec((1, GLA_DV), lambda b, c: (0, 0)),
        ],
        out_specs=[
            pl.BlockSpec((NB, C, A_WIDTH), lambda b, c: (b, c, 0)),
            pl.BlockSpec((NB, GLA_QK, GLA_DV), lambda b, c: (b, 0, 0)),
        ],
        out_shape=[
            jax.ShapeDtypeStruct((B, T, A_WIDTH), BF16),
            jax.ShapeDtypeStruct((B, GLA_QK, GLA_DV), F32),
        ],
        scratch_shapes=[pltpu.VMEM((NB, GLA_QK, A_WIDTH), F32)],
        compiler_params=_params("parallel", "arbitrary"),
        name="gla",
    )(u, u, u, u, u, s0, w2p, bg, gn)
    return o.reshape(B * T, A_WIDTH), s_fin.reshape(B, GLA_HEADS, GLA_DK, GLA_DV)


N_RWKV_PER_BATCH = (11, 2, 6)
N_RWKV_SHARED = 15


def _rwkv_kernel(*refs, C, NB):
    n_in, n_out, n_scr = N_RWKV_PER_BATCH
    ins, shared = refs[:n_in], refs[n_in:n_in + N_RWKV_SHARED]
    outs = refs[n_in + N_RWKV_SHARED:n_in + N_RWKV_SHARED + n_out]
    scr = refs[n_in + N_RWKV_SHARED + n_out:]
    GW = GROUP * RWKV_N
    keep_h = (_iota((GW, GW), 0) >> 6) == (_iota((GW, GW), 1) >> 6)

    @pl.when(pl.program_id(1) == 0)
    def _():
        h0_ref, h_scr = ins[10], scr[5]
        for bb in range(NB):
            for prev, sh in zip(scr[:5], ins[5:10]):
                prev[bb] = sh[bb]
            h0 = h0_ref[bb]
            for g in range(RWKV_HEADS // GROUP):
                t = jnp.concatenate([h0[:, g * GW:(g + 1) * GW]] * GROUP, axis=0)
                h_scr[bb, g] = jnp.where(keep_h, t, 0.0)

    at = lambda rs, bb: [r.at[bb] for r in rs]
    chains = [_rwkv_one(*at(ins[:5], bb), *shared, *at(outs, bb), *at(scr, bb), keep_h, C=C) for bb in range(NB)]
    for _ in _lockstep(chains):
        pass


def _rwkv_one(r_ref, k_ref, v_ref, wa_ref, gl_ref,
              mu_r, mu_k, mu_v, mu_wa, mu_gl, w0_ref, w2_ref, a0_ref, a2_ref, g2_ref,
              kk_ref, ka_ref, rk_ref, gng_ref, gnb_ref,
              y_ref, hout_ref,
              pr, pk, pv, pwa, pgl, h_scr, keep_h, *, C):
    GW = GROUP * RWKV_N
    CH = GROUP * C
    logc = C.bit_length() - 1
    n_groups = RWKV_HEADS // GROUP

    def head_mask(rows, cols, row_shift, col_of_lane):
        keep = (_iota((rows, cols), 0) >> row_shift) == col_of_lane(_iota((rows, cols), 1))
        return keep.astype(F32).astype(BF16)

    keep_rows = head_mask(CH, GW, logc, lambda l: l >> 6)
    keep_sq = head_mask(CH, CH, logc, lambda l: l >> logc)

    def bd(x, keep):
        return jnp.concatenate([x.astype(BF16)] * GROUP, axis=0) * keep

    rid = _iota((C, 1), 0)

    def shift(x_ref, prev, mu):
        x = x_ref[...].astype(F32)
        xp = jnp.where(rid == 0, prev[...], pltpu.roll(x, 1, axis=0))
        prev[...] = x[C - 1:C]
        return x + mu[...] * (xp - x)

    rs = shift(r_ref, pr, mu_r)
    ks = shift(k_ref, pk, mu_k)
    vs = shift(v_ref, pv, mu_v)
    was = shift(wa_ref, pwa, mu_wa)
    gls = shift(gl_ref, pgl, mu_gl)

    dec = -_softplus(-(w0_ref[...] + _dot3(jnp.tanh(was), w2_ref[...]))) - 0.5
    logw = -jnp.exp(dec)
    a = jax.nn.sigmoid(a0_ref[...] + _dot(was, a2_ref[...]))
    gate = _dot(jax.nn.sigmoid(gls), g2_ref[...])
    yield
    ones_h = keep_h.astype(F32).astype(BF16)

    def head_sum(x):
        return jnp.concatenate([_dot(x[:, g * GW:(g + 1) * GW], ones_h) for g in range(n_groups)], axis=1)

    kkr = ks * kk_ref[...]
    kk = kkr / jnp.maximum(jnp.sqrt(head_sum(kkr * kkr)), 1e-12)
    km = ks * (1.0 + (a - 1.0) * ka_ref[...])
    beta = kk * a
    tril = _iota((C, C), 0) >= _iota((C, C), 1)
    gc = _dot_exact_lhs(tril, logw)
    g_end = gc[C - 1:C]
    kap = kk * jnp.exp(gc - logw)
    rt = rs * jnp.exp(gc)
    e_neg = jnp.exp(-gc)
    b_inv = beta * e_neg
    k_inv = km * e_neg
    e_end = jnp.exp(g_end - gc)
    k_hat = km * e_end
    b_hat = beta * e_end

    col_s = _iota((C, CH), 1) & (C - 1)
    row_t = _iota((C, CH), 0)
    strict = row_t > col_s
    incl = row_t >= col_s
    same_sub = (row_t >> 4) == (col_s >> 4)
    eye_cat = (row_t == col_s).astype(F32)
    ones_c = jnp.ones((C, LANES), F32)

    yield
    ys = [None] * n_groups

    def group(g):
        L = slice(g * GW, (g + 1) * GW)
        p = jnp.concatenate([kap[:, L], rt[:, L]], axis=0)
        zb = _dot(p, bd(b_inv[:, L], keep_rows), NT)
        zk = _dot(p, bd(k_inv[:, L], keep_rows), NT)
        yield
        a_b = jnp.where(strict, zb[:C], 0.0)
        b_b = jnp.where(incl, zb[C:], 0.0)
        a_k = jnp.where(strict, zk[:C], 0.0)
        b_k = jnp.where(incl, zk[C:], 0.0)
        bd_v = bd(vs[:, L], keep_rows)
        akv = _dot(a_k, bd_v)

        d_blk = jnp.where(same_sub, a_b, 0.0)
        x1 = -d_blk
        p1 = eye_cat + x1
        x2 = _dot(x1, bd(x1, keep_sq))
        hbd = h_scr[g]
        ph = _dot(p, hbd)
        yield
        rhs = ph[:C] + akv
        r2 = _dot(jnp.concatenate([x2, p1], axis=0), bd(x2, keep_sq))
        yield
        x4, p2 = r2[:C], p1 + r2[C:]
        r4 = _dot(jnp.concatenate([x4, p2], axis=0), bd(x4, keep_sq))
        yield
        x8, p3 = r4[:C], p2 + r4[C:]
        t_d = p3 + _dot(p3, bd(x8, keep_sq))
        yield

        n1 = _dot(t_d, bd(a_b - d_blk, keep_sq))
        s1 = _dot(t_d, bd(rhs, keep_rows))
        yield
        n2 = _dot(n1, bd(n1, keep_sq))
        yield
        s2 = s1 + _dot(n2, bd(s1, keep_rows))
        yield
        u = s2 - _dot(n1, bd(s2, keep_rows))
        yield
        y = ph[C:] + _dot(b_k, bd_v) - _dot(b_b, bd(u, keep_rows))
        m = _dot(k_hat[:, L], vs[:, L], TN) - _dot(b_hat[:, L], u, TN)
        g_col = _dot_exact_rhs(logw[:, L], ones_c, TN)[:, :1]
        yield
        h_new = jnp.exp(g_col) * hbd + jnp.where(keep_h, m, 0.0)
        h_scr[g] = h_new
        hout_ref[:, L] = (h_new[0:RWKV_N] + h_new[RWKV_N:2 * RWKV_N]
                          + h_new[2 * RWKV_N:3 * RWKV_N] + h_new[3 * RWKV_N:4 * RWKV_N])
        ys[g] = y

    yield from _lockstep([group(g) for g in range(n_groups)])
    y = jnp.concatenate(ys, axis=1)

    inv_n = 1.0 / RWKV_N
    mu = head_sum(y) * inv_n
    d = y - mu
    var = head_sum(d * d) * inv_n
    yn = d * lax.rsqrt(var + RWKV_GN_EPS) * gng_ref[...] + gnb_ref[...]
    bonus = head_sum(rs * km * rk_ref[...]) * vs
    y_ref[...] = ((yn + bonus) * gate).astype(y_ref.dtype)


def _rwkv(u, shift0, h0, wts, B, T, C):
    nc = T // C
    NB = MIXER_BATCH
    wide = lambda blk: pl.BlockSpec((NB, C, B_WIDTH), lambda b, c: (b, c, blk))
    narrow = lambda blk: pl.BlockSpec((NB, C, LANES), lambda b, c: (b, c, blk))
    sh = lambda w, blk: pl.BlockSpec((NB, 1, w), lambda b, c: (b, 0, blk))
    const = lambda shape: pl.BlockSpec(shape, lambda b, c: (0,) * len(shape))
    assert len(wts) == N_RWKV_SHARED
    in_specs = [wide(3), wide(4), wide(5), narrow(25), narrow(26),
                sh(B_WIDTH, 0), sh(B_WIDTH, 1), sh(B_WIDTH, 2), sh(LANES, 12), sh(LANES, 13),
                pl.BlockSpec((NB, RWKV_N, B_WIDTH), lambda b, c: (b, 0, 0))]
    in_specs += [const(w.shape) for w in wts]
    GW = GROUP * RWKV_N
    u = u.reshape(B, T, UW)
    y, h_fin = pl.pallas_call(
        functools.partial(_rwkv_kernel, C=C, NB=NB),
        grid=(B // NB, nc),
        in_specs=in_specs,
        out_specs=[
            pl.BlockSpec((NB, C, B_WIDTH), lambda b, c: (b, c, 0)),
            pl.BlockSpec((NB, RWKV_N, B_WIDTH), lambda b, c: (b, 0, 0)),
        ],
        out_shape=[
            jax.ShapeDtypeStruct((B, T, B_WIDTH), BF16),
            jax.ShapeDtypeStruct((B, RWKV_N, B_WIDTH), F32),
        ],
        scratch_shapes=[
            pltpu.VMEM((NB, 1, B_WIDTH), F32), pltpu.VMEM((NB, 1, B_WIDTH), F32), pltpu.VMEM((NB, 1, B_WIDTH), F32),
            pltpu.VMEM((NB, 1, LANES), F32), pltpu.VMEM((NB, 1, LANES), F32),
            pltpu.VMEM((NB, RWKV_HEADS // GROUP, GW, GW), F32),
        ],
        compiler_params=_params("parallel", "arbitrary"),
        name="rwkv7",
    )(u, u, u, u, u, shift0, shift0, shift0, shift0, shift0, h0, *wts)
    return y.reshape(B * T, B_WIDTH), h_fin


def _ret_kernel(q_ref, k_ref, v_ref, g_ref, cos_ref, sin_ref, r0_ref, gng_ref, gnb_ref,
                o_ref, rout_ref, r_scr, *, C):
    c = pl.program_id(1)

    @pl.when(c == 0)
    def _():
        r_scr[...] = r0_ref[0]

    cos = cos_ref[...]
    sin = sin_ref[...]
    half = RET_DK // 2

    def rope(x):
        x1, x2 = x[:, :half], x[:, half:]
        return jnp.concatenate([x1 * cos - x2 * sin, x1 * sin + x2 * cos], axis=1)

    ri = _iota((C, C), 0)
    ci = _iota((C, C), 1)
    causal = ri >= ci
    diff = (ri - ci).astype(F32)
    pos1 = (_iota((C, 1), 0) + 1).astype(F32)
    outs = [None] * RET_HEADS

    def head(h):
        lg = math.log1p(-(2.0 ** (-5.0 - h)))
        q = rope(q_ref[:, h * RET_DK:(h + 1) * RET_DK].astype(F32))
        k = rope(k_ref[:, h * RET_DK:(h + 1) * RET_DK].astype(F32)) * (RET_DK ** -0.5)
        v = v_ref[:, h * RET_DV:(h + 1) * RET_DV]
        r = r_scr[h]
        scores = _dot(q, k, NT)
        inter = _dot(q, r)
        yield
        att = scores * jnp.where(causal, jnp.exp(diff * lg), 0.0)
        o = _dot(att, v) + inter * jnp.exp(pos1 * lg)
        r_scr[h] = math.exp(C * lg) * r + _dot(k * jnp.exp((C - pos1) * lg), v, TN)
        yield
        mu = jnp.mean(o, axis=-1, keepdims=True)
        d = o - mu
        var = jnp.mean(d * d, axis=-1, keepdims=True)
        outs[h] = d * lax.rsqrt(var + RET_GN_EPS)

    for _ in _lockstep([head(h) for h in range(RET_HEADS)]):
        pass
    g = g_ref[...].astype(F32)
    o = jnp.concatenate(outs, axis=1) * gng_ref[...] + gnb_ref[...]
    o_ref[...] = ((g * jax.nn.sigmoid(g)) * o).astype(o_ref.dtype)
    rout_ref[0] = r_scr[...]


def _ret(u, cos, sin, r0, gn_g, gn_b, B, T, C):
    nc = T // C
    row = lambda b, c: b * nc + c
    return pl.pallas_call(
        functools.partial(_ret_kernel, C=C),
        grid=(B, nc),
        in_specs=[
            pl.BlockSpec((C, RET_QK), lambda b, c: (row(b, c), 0)),
            pl.BlockSpec((C, RET_QK), lambda b, c: (row(b, c), 1)),
            pl.BlockSpec((C, RET_V), lambda b, c: (row(b, c), 1)),
            pl.BlockSpec((C, RET_V), lambda b, c: (row(b, c), 2)),
            pl.BlockSpec((C, RET_DK // 2), lambda b, c: (c, 0)),
            pl.BlockSpec((C, RET_DK // 2), lambda b, c: (c, 0)),
            pl.BlockSpec((1, RET_HEADS, RET_DK, RET_DV), lambda b, c: (b, 0, 0, 0)),
            pl.BlockSpec((1, RET_V), lambda b, c: (0, 0)),
            pl.BlockSpec((1, RET_V), lambda b, c: (0, 0)),
        ],
        out_specs=[
            pl.BlockSpec((C, RET_V), lambda b, c: (row(b, c), 0)),
            pl.BlockSpec((1, RET_HEADS, RET_DK, RET_DV), lambda b, c: (b, 0, 0, 0)),
        ],
        out_shape=[
            jax.ShapeDtypeStruct((B * T, RET_V), BF16),
            jax.ShapeDtypeStruct((B, RET_HEADS, RET_DK, RET_DV), F32),
        ],
        scratch_shapes=[pltpu.VMEM((RET_HEADS, RET_DK, RET_DV), F32)],
        compiler_params=_params("parallel", "arbitrary"),
        name="retention",
    )(u, u, u, u, cos, sin, r0, gn_g, gn_b)


def _router_kernel(res_ref, a_ref, wo_ref, g_ref, wr_ref, x_ref, ids_ref, gsel_ref):
    x = res_ref[...] + jnp.dot(a_ref[...], wo_ref[...], preferred_element_type=F32)
    x_ref[...] = x
    h = _rms(x, g_ref[...])
    logits = _dot3(h, wr_ref[...])
    lane = _iota(logits.shape, 1)
    valid = lane < N_EXPERTS
    logits = jnp.where(valid, logits, -jnp.inf)
    m = jnp.max(logits, axis=-1, keepdims=True)
    e = jnp.where(valid, jnp.exp(logits - m), 0.0)
    p = e / jnp.sum(e, axis=-1, keepdims=True)
    big = jnp.int32(LANES)
    p1 = jnp.max(p, axis=-1, keepdims=True)
    i1 = jnp.min(jnp.where(p == p1, lane, big), axis=-1, keepdims=True)
    rest = jnp.where((lane == i1) | ~valid, -1.0, p)
    p2 = jnp.max(rest, axis=-1, keepdims=True)
    i2 = jnp.min(jnp.where(rest == p2, lane, big), axis=-1, keepdims=True)
    tot = p1 + p2
    ids_ref[...] = jnp.where(lane == 0, i1, i2)
    gsel_ref[...] = jnp.where(lane == 0, p1 / tot, p2 / tot)


def _router(res, a, wo, g, wr):
    n, d = res.shape
    tm = min(n, 512)
    row = lambda w: pl.BlockSpec((tm, w), lambda i: (i, 0))
    const = lambda s: pl.BlockSpec(s, lambda i: (0, 0))
    return pl.pallas_call(
        _router_kernel,
        grid=(n // tm,),
        in_specs=[row(d), row(a.shape[1]), const(wo.shape), const((1, d)), const((d, LANES))],
        out_specs=[row(d), row(LANES), row(LANES)],
        out_shape=[jax.ShapeDtypeStruct((n, d), F32), jax.ShapeDtypeStruct((n, LANES), jnp.int32),
                   jax.ShapeDtypeStruct((n, LANES), F32)],
        compiler_params=_params("parallel"),
        name="router",
    )(res, a, wo, g, wr)


def _dispatch_kernel(ends_ref, pos1_ref, pos2_ref, x_ref, xs_ref, zbuf, zsem, sem, *, TQ, TM):
    t = pl.program_id(0)

    @pl.when(t == 0)
    def _():
        zbuf[...] = jnp.zeros_like(zbuf)
        tails = [pltpu.make_async_copy(zbuf, xs_ref.at[pl.ds(pl.multiple_of(ends_ref[e] - TM, TM), TM)], zsem)
                 for e in range(N_EXPERTS)]
        for cp in tails:
            cp.start()
        for cp in tails:
            cp.wait()

        def zero_tile(i, carry):
            cp = pltpu.make_async_copy(zbuf, xs_ref.at[pl.ds(pl.multiple_of(i * TM, TM), TM)], zsem)
            cp.start()
            cp.wait()
            return carry

        lax.fori_loop(ends_ref[N_EXPERTS - 1] // TM, xs_ref.shape[0] // TM, zero_tile, 0)

    def row_copy(r, pos_ref):
        return pltpu.make_async_copy(x_ref.at[pl.ds(r, 1)], xs_ref.at[pl.ds(pos_ref[0, 0, r], 1)], sem)

    def issue(r, carry):
        row_copy(r, pos1_ref).start(priority=0)
        row_copy(r, pos2_ref).start(priority=1)
        return carry

    lax.fori_loop(0, TQ, issue, 0, unroll=8)

    def drain(r, carry):
        row_copy(r, pos1_ref).wait()
        row_copy(r, pos2_ref).wait()
        return carry

    lax.fori_loop(0, TQ, drain, 0, unroll=8)


def _dispatch(x, pos, ends, p, tm):
    n, d = x.shape
    tq = min(n, DISPATCH_ROWS)
    nt = n // tq
    pos3 = pos.reshape(2 * nt, 1, tq)
    grid_spec = pltpu.PrefetchScalarGridSpec(
        num_scalar_prefetch=1,
        grid=(nt,),
        in_specs=[pl.BlockSpec((1, 1, tq), lambda t, ends: (t, 0, 0), memory_space=pltpu.SMEM),
                  pl.BlockSpec((1, 1, tq), lambda t, ends: (t + nt, 0, 0), memory_space=pltpu.SMEM),
                  pl.BlockSpec((tq, d), lambda t, ends: (t, 0))],
        out_specs=pl.BlockSpec(memory_space=pl.ANY),
        scratch_shapes=[pltpu.VMEM((tm, d), F32), pltpu.SemaphoreType.DMA(()), pltpu.SemaphoreType.DMA(())],
    )
    return pl.pallas_call(
        functools.partial(_dispatch_kernel, TQ=tq, TM=tm),
        grid_spec=grid_spec,
        out_shape=jax.ShapeDtypeStruct((p, d), F32),
        compiler_params=_params("arbitrary"),
        name="dispatch",
    )(ends, pos3, pos3, x)


def _moe_ffn_kernel(te_ref, nu_ref, x_ref, g_ref, *refs, tf, residual, n_pre):
    pre_a = refs[:n_pre]
    pre_w = refs[n_pre:n_pre + min(n_pre, 1)]
    w1_ref, w3_ref, w2_ref, o_ref, acc = refs[n_pre + len(pre_w):]
    t = pl.program_id(0)

    @pl.when(t < nu_ref[0])
    def _():
        x = x_ref[...]
        if pre_a:
            a = jnp.concatenate([a_ref[...] for a_ref in pre_a], axis=1)
            x = x + jnp.dot(a, pre_w[0][...], preferred_element_type=F32)
        h = _rms(x, g_ref[...]).astype(BF16)
        if residual:
            acc[...] = x
        for i, f0 in enumerate(range(0, D_FF, tf)):
            a1 = jnp.dot(h, w1_ref[0, :, f0:f0 + tf], preferred_element_type=F32)
            a3 = jnp.dot(h, w3_ref[0, :, f0:f0 + tf], preferred_element_type=F32)
            act = (a1 * jax.nn.sigmoid(a1) * a3).astype(BF16)
            part = jnp.dot(act, w2_ref[0, f0:f0 + tf, :], preferred_element_type=F32)
            if i == 0 and not residual:
                acc[...] = part
            elif f0 + tf < D_FF:
                acc[...] += part
            else:
                o_ref[...] = acc[...] + part

    @pl.when(t >= nu_ref[0])
    def _():
        o_ref[...] = jnp.zeros_like(o_ref)


def _ffn(res, a_list, w_out, g, w1, w3, w2):
    tm = min(res.shape[0], MOE_TILE)
    nt = res.shape[0] // tm
    return _moe_ffn(res, g, jnp.zeros((nt,), jnp.int32), jnp.full((1,), nt, jnp.int32), w1, w3, w2, tm,
                    residual=True, pre=(a_list, [w_out]))


def _moe_ffn(xs, g, tile_expert, n_used, w1, w3, w2, tm, residual=False, pre=((), ())):
    p, d = xs.shape
    dff = w1.shape[2]
    row_tile = lambda t, te, nu: (jnp.minimum(t, nu[0] - 1), 0)
    pre_a, pre_w = pre
    grid_spec = pltpu.PrefetchScalarGridSpec(
        num_scalar_prefetch=2,
        grid=(p // tm,),
        in_specs=[
            pl.BlockSpec((tm, d), row_tile),
            pl.BlockSpec((1, d), lambda t, te, nu: (0, 0)),
            *[pl.BlockSpec((tm, a.shape[1]), row_tile) for a in pre_a],
            *[pl.BlockSpec(w.shape, lambda t, te, nu: (0, 0)) for w in pre_w],
            pl.BlockSpec((1, d, dff), lambda t, te, nu: (te[t], 0, 0)),
            pl.BlockSpec((1, d, dff), lambda t, te, nu: (te[t], 0, 0)),
            pl.BlockSpec((1, dff, d), lambda t, te, nu: (te[t], 0, 0)),
        ],
        out_specs=pl.BlockSpec((tm, d), lambda t, te, nu: (t, 0)),
        scratch_shapes=[pltpu.VMEM((tm, d), F32)],
    )
    return pl.pallas_call(
        functools.partial(_moe_ffn_kernel, tf=256, residual=residual, n_pre=len(pre_a)),
        grid_spec=grid_spec,
        out_shape=jax.ShapeDtypeStruct((p, d), F32),
        compiler_params=_params("arbitrary"),
        name="moe_ffn",
    )(tile_expert, n_used, xs, g, *pre_a, *pre_w, w1, w3, w2)


def _combine_kernel(pos1_ref, pos2_ref, pos1n_ref, pos2n_ref, x_ref, gsel_ref, gf_ref, ys_ref, o_ref,
                    buf, sems, *, TQ):
    t = pl.program_id(0)
    slot = t % 2

    def row_copy(r, pos_ref, s, k):
        return pltpu.make_async_copy(ys_ref.at[pl.ds(pos_ref[0, 0, r], 1)], buf.at[s, k, pl.ds(r, 1)], sems.at[s])

    def issue(p1_ref, p2_ref, s):
        def body(r, carry):
            row_copy(r, p1_ref, s, 0).start(priority=0)
            row_copy(r, p2_ref, s, 1).start(priority=1)
            return carry

        lax.fori_loop(0, TQ, body, 0, unroll=8)

    @pl.when(t == 0)
    def _():
        issue(pos1_ref, pos2_ref, 0)

    @pl.when(t + 1 < pl.num_programs(0))
    def _():
        issue(pos1n_ref, pos2n_ref, 1 - slot)

    def drain(r, carry):
        row_copy(r, pos1_ref, slot, 0).wait()
        row_copy(r, pos2_ref, slot, 1).wait()
        return carry

    lax.fori_loop(0, TQ, drain, 0, unroll=8)
    gs = gsel_ref[...]
    y = x_ref[...] + gs[:, 0:1] * buf[slot, 0] + gs[:, 1:2] * buf[slot, 1]
    o_ref[...] = _rms(y, gf_ref[...])


def _combine(x, ys, pos, gsel, g_final):
    n, d = x.shape
    tq = min(n, DISPATCH_ROWS)
    nt = n // tq
    pos3 = pos.reshape(2 * nt, 1, tq)
    smem = lambda f: pl.BlockSpec((1, 1, tq), f, memory_space=pltpu.SMEM)
    nxt = lambda t: jnp.minimum(t + 1, nt - 1)
    return pl.pallas_call(
        functools.partial(_combine_kernel, TQ=tq),
        grid=(nt,),
        in_specs=[smem(lambda t: (t, 0, 0)), smem(lambda t: (t + nt, 0, 0)),
                  smem(lambda t: (nxt(t), 0, 0)), smem(lambda t: (nxt(t) + nt, 0, 0)),
                  pl.BlockSpec((tq, d), lambda t: (t, 0)),
                  pl.BlockSpec((tq, LANES), lambda t: (t, 0)),
                  pl.BlockSpec((1, d), lambda t: (0, 0)),
                  pl.BlockSpec(memory_space=pl.ANY)],
        out_specs=pl.BlockSpec((tq, d), lambda t: (t, 0)),
        out_shape=jax.ShapeDtypeStruct((n, d), F32),
        scratch_shapes=[pltpu.VMEM((2, 2, tq, d), F32), pltpu.SemaphoreType.DMA((2,))],
        compiler_params=_params("arbitrary"),
        name="combine",
    )(pos3, pos3, pos3, pos3, x, gsel, g_final, ys)


def _moe_plan(ids, tm):
    n = ids.shape[0]
    e_flat = ids.T.reshape(-1)
    experts = jnp.arange(N_EXPERTS, dtype=jnp.int32)
    onehot = (e_flat[:, None] == experts[None, :]).astype(jnp.int32)
    csum = jnp.cumsum(onehot, axis=0)
    counts = csum[-1]
    padded = jnp.maximum((counts + tm - 1) // tm, 1) * tm
    ends = jnp.cumsum(padded).astype(jnp.int32)
    pos = jnp.sum((csum - 1 + (ends - padded)[None, :]) * onehot, axis=1).astype(jnp.int32)
    p = 2 * n + N_EXPERTS * tm
    n_used = ends[-1] // tm
    tile_start = jnp.minimum(jnp.arange(p // tm, dtype=jnp.int32) * tm, ends[-1] - 1)
    tile_expert = jnp.sum((tile_start[:, None] >= ends[None, :]).astype(jnp.int32), axis=1)
    return pos, ends, tile_expert, n_used.reshape(1), p


def _prep_weights(p):
    w_in = p["ab_w_in"][0]
    wa, wb = w_in[:, :A_COLS], w_in[:, A_COLS:]
    pad = jnp.zeros((D_MODEL, LANES - GLA_GATE_RANK), F32)
    w_in_p = jnp.concatenate(
        [wa[:, :2 * GLA_QK + A_WIDTH], wa[:, 2 * GLA_QK + A_WIDTH + GLA_GATE_RANK:], wb[:, :3 * B_WIDTH],
         wa[:, 2 * GLA_QK + A_WIDTH:2 * GLA_QK + A_WIDTH + GLA_GATE_RANK], pad, wb[:, 3 * B_WIDTH:]], axis=1)
    row = lambda v: v.reshape(1, -1).astype(F32)
    mu = p["rwkv_mu"][0]
    z64 = jnp.zeros((64, B_WIDTH), F32)
    w = dict(
        norm_mix=p["norm_mix"], norm_ffn=p["norm_ffn"], norm_final=row(p["norm_final"]),
        w_in=w_in_p.astype(BF16),
        w_out=p["ab_w_out"][0].astype(BF16),
        gla_w2=jnp.concatenate([p["gla_w_gate2"][0], jnp.zeros((LANES - GLA_GATE_RANK, GLA_QK), F32)], axis=0),
        gla_bg=row(p["gla_b_gate"][0]), gla_norm=row(p["gla_norm"][0]),
        rwkv=(row(mu[:B_WIDTH]), row(mu[B_WIDTH:2 * B_WIDTH]), row(mu[2 * B_WIDTH:3 * B_WIDTH]),
              row(mu[3 * B_WIDTH:3 * B_WIDTH + LANES]), row(mu[3 * B_WIDTH + LANES:]),
              row(p["rwkv_w0"][0]), jnp.concatenate([p["rwkv_w2"][0], z64], axis=0),
              row(p["rwkv_a0"][0]), jnp.concatenate([z64, p["rwkv_a2"][0]], axis=0),
              p["rwkv_g2"][0], row(p["rwkv_k_k"][0]), row(p["rwkv_k_a"][0]), row(p["rwkv_r_k"][0]),
              row(p["rwkv_gn_g"][0]), row(p["rwkv_gn_b"][0])),
        ret_w_in=p["ret_w_in"][0].astype(BF16), ret_w_out=p["ret_w_out"][0].astype(BF16),
        ret_gn_g=row(p["ret_gn_g"][0]), ret_gn_b=row(p["ret_gn_b"][0]),
        ffn_w1=p["ffn_w1"].astype(BF16), ffn_w3=p["ffn_w3"].astype(BF16), ffn_w2=p["ffn_w2"].astype(BF16),
        router=jnp.concatenate([p["moe_router"][0], jnp.zeros((D_MODEL, LANES - N_EXPERTS), F32)], axis=1),
        moe_w1=p["moe_w1"][0].astype(BF16), moe_w3=p["moe_w3"][0].astype(BF16), moe_w2=p["moe_w2"][0].astype(BF16),
    )
    return w


def _trunk(x, pos0, st_gla, st_rwkv, st_shift, st_ret, w):
    B, T, D = x.shape
    n = B * T
    x0 = x.reshape(n, D)
    c_ab = min(CHUNK, T)
    c_ret = min(256, T)

    u = _norm_matmul(x0, w["norm_mix"][0:1], w["w_in"], tm=512, tn=UW, out_dtype=BF16)
    oa, s_gla = _gla(u, st_gla, w["gla_w2"], w["gla_bg"], w["gla_norm"], B, T, c_ab)
    h0 = st_rwkv.transpose(0, 3, 1, 2).reshape(B, RWKV_N, B_WIDTH)
    yb, h_fin = _rwkv(u, st_shift, h0, w["rwkv"], B, T, c_ab)
    s_rwkv = h_fin.reshape(B, RWKV_N, RWKV_HEADS, RWKV_N).transpose(0, 2, 3, 1)
    u_last = u.reshape(B, T, UW)[:, T - 1:, :]
    s_shift = jnp.concatenate([u_last[..., 3 * A_WIDTH:6 * A_WIDTH], u_last[..., UW - 2 * LANES:]], axis=-1).astype(F32)
    x2 = _ffn(x0, [oa, yb], w["w_out"], w["norm_ffn"][0:1], w["ffn_w1"], w["ffn_w3"], w["ffn_w2"])

    ur = _norm_matmul(x2, w["norm_mix"][1:2], w["ret_w_in"], tm=1024, tn=1536, out_dtype=BF16)
    half = RET_DK // 2
    inv = 1.0 / (ROPE_BASE ** (jnp.arange(0, RET_DK, 2, dtype=F32) / RET_DK))
    ang = (pos0 + jnp.arange(T)).astype(F32)[:, None] * inv[None, :]
    o, s_ret = _ret(ur, jnp.cos(ang), jnp.sin(ang), st_ret, w["ret_gn_g"], w["ret_gn_b"], B, T, c_ret)
    x3, ids, gsel = _router(x2, o, w["ret_w_out"], w["norm_ffn"][1:2], w["router"])
    tm = min(MOE_TILE, n // 4)
    pos, ends, tile_expert, n_used, p = _moe_plan(ids[:, :2], tm)
    xs = _dispatch(x3, pos, ends, p, tm)
    ys = _moe_ffn(xs, w["norm_ffn"][1:2], tile_expert, n_used, w["moe_w1"], w["moe_w3"], w["moe_w2"], tm)
    y = _combine(x3, ys, pos, gsel, w["norm_final"])
    return y.reshape(B, T, D), s_gla[None], s_rwkv[None], s_shift[None], s_ret[None]


def kernel(x_prompt, x_sample, state_gla, state_rwkv, state_shift, state_ret, norm_mix, norm_ffn, norm_final, ab_w_in, ab_w_out, gla_w_gate2, gla_b_gate, gla_norm, rwkv_mu, rwkv_w0, rwkv_w2, rwkv_a0, rwkv_a2, rwkv_g2, rwkv_k_k, rwkv_k_a, rwkv_r_k, rwkv_gn_g, rwkv_gn_b, ret_w_in, ret_gn_g, ret_gn_b, ret_w_out, ffn_w1, ffn_w3, ffn_w2, moe_router, moe_w1, moe_w3, moe_w2):
    p = dict(norm_mix=norm_mix, norm_ffn=norm_ffn, norm_final=norm_final, ab_w_in=ab_w_in, ab_w_out=ab_w_out,
             gla_w_gate2=gla_w_gate2, gla_b_gate=gla_b_gate, gla_norm=gla_norm, rwkv_mu=rwkv_mu, rwkv_w0=rwkv_w0,
             rwkv_w2=rwkv_w2, rwkv_a0=rwkv_a0, rwkv_a2=rwkv_a2, rwkv_g2=rwkv_g2, rwkv_k_k=rwkv_k_k,
             rwkv_k_a=rwkv_k_a, rwkv_r_k=rwkv_r_k, rwkv_gn_g=rwkv_gn_g, rwkv_gn_b=rwkv_gn_b, ret_w_in=ret_w_in,
             ret_gn_g=ret_gn_g, ret_gn_b=ret_gn_b, ret_w_out=ret_w_out, ffn_w1=ffn_w1, ffn_w3=ffn_w3, ffn_w2=ffn_w2,
             moe_router=moe_router, moe_w1=moe_w1, moe_w3=moe_w3, moe_w2=moe_w2)
    w = _prep_weights(p)
    bp, tp = x_prompt.shape[0], x_prompt.shape[1]
    dt = x_prompt.dtype
    z_gla = jnp.zeros((bp, GLA_HEADS, GLA_DK, GLA_DV), dt)
    z_rwkv = jnp.zeros((bp, RWKV_HEADS, RWKV_N, RWKV_N), dt)
    z_shift = jnp.zeros((bp, 1, B_COLS), dt)
    z_ret = jnp.zeros((bp, RET_HEADS, RET_DK, RET_DV), dt)
    past = 2048
    yp = _trunk(x_prompt, 0, z_gla, z_rwkv, z_shift, z_ret, w)
    ys = _trunk(x_sample, past, state_gla[0], state_rwkv[0], state_shift[0], state_ret[0], w)
    return (yp[0], ys[0], yp[1], yp[2], yp[3], yp[4], ys[1], ys[2], ys[3], ys[4])
```

```python
import functools
import math

import jax
import jax.numpy as jnp
from jax import lax
from jax.experimental import pallas as pl
from jax.experimental.pallas import tpu as pltpu

F32 = jnp.float32
BF16 = jnp.bfloat16

D_MODEL = 1024
EPS = 1e-6
GLA_HEADS, GLA_DK, GLA_DV = 4, 64, 128
GLA_QK = GLA_HEADS * GLA_DK
A_WIDTH = GLA_HEADS * GLA_DV
GLA_GATE_RANK = 16
GLA_TAU = 16.0
RWKV_HEADS, RWKV_N = 8, 64
B_WIDTH = RWKV_HEADS * RWKV_N
RWKV_GN_EPS = 64e-5
B_COLS = 3 * B_WIDTH + 64 + 64 + 128
A_COLS = 2 * GLA_QK + A_WIDTH + GLA_GATE_RANK + A_WIDTH
RET_HEADS = 4
RET_DK = D_MODEL // RET_HEADS
RET_DV = 2 * D_MODEL // RET_HEADS
RET_QK = RET_HEADS * RET_DK
RET_V = RET_HEADS * RET_DV
RET_GN_EPS = 1e-5
ROPE_BASE = 10000.0
D_FF = 2816
N_EXPERTS = 8
CHUNK = 64
SUB = 16
LANES = 128
GROUP = 4
MIXER_BATCH = 8
MOE_TILE = 512
DISPATCH_ROWS = 512
UW = 3456
VMEM_LIMIT = 56 * 1024 * 1024

NN = (((1,), (0,)), ((), ()))
NT = (((1,), (1,)), ((), ()))
TN = (((0,), (0,)), ((), ()))


def _dot(a, b, dims=NN):
    return lax.dot_general(a.astype(BF16), b.astype(BF16), dims, preferred_element_type=F32)


def _split(x):
    hi = x.astype(BF16)
    lo = (x - hi.astype(F32)).astype(BF16)
    return hi, lo


def _dot3(a, b, dims=NN):
    ah, al = _split(a)
    bh, bl = _split(b)
    dg = lambda x, y: lax.dot_general(x, y, dims, preferred_element_type=F32)
    return dg(ah, bh) + dg(ah, bl) + dg(al, bh)


def _dot_exact_lhs(m, x, dims=NN):
    xh, xl = _split(x)
    mb = m.astype(BF16)
    dg = lambda y: lax.dot_general(mb, y, dims, preferred_element_type=F32)
    return dg(xh) + dg(xl)


def _dot_exact_rhs(x, m, dims=NN):
    xh, xl = _split(x)
    mb = m.astype(BF16)
    dg = lambda y: lax.dot_general(y, mb, dims, preferred_element_type=F32)
    return dg(xh) + dg(xl)


def _iota(shape, axis):
    return lax.broadcasted_iota(jnp.int32, shape, axis)


def _softplus(z):
    return jnp.maximum(z, 0.0) + jnp.log1p(jnp.exp(-jnp.abs(z)))


def _rms(x, g):
    return x * lax.rsqrt(jnp.mean(x * x, axis=-1, keepdims=True) + EPS) * g


def _lockstep(chains):
    chains = list(chains)
    while chains:
        alive = []
        for ch in chains:
            try:
                next(ch)
                alive.append(ch)
            except StopIteration:
                pass
        chains = alive
        yield


def _params(*sem):
    return pltpu.CompilerParams(dimension_semantics=sem, vmem_limit_bytes=VMEM_LIMIT)


def _norm_matmul_kernel(x_ref, g_ref, w_ref, o_ref, h_ref):
    @pl.when(pl.program_id(1) == 0)
    def _():
        h_ref[...] = _rms(x_ref[...], g_ref[...]).astype(BF16)

    o_ref[...] = jnp.dot(h_ref[...], w_ref[...], preferred_element_type=F32).astype(o_ref.dtype)


def _norm_matmul(x, g, w, tm, tn, out_dtype):
    n, d = x.shape
    nout = w.shape[1]
    tm = min(n, tm)
    return pl.pallas_call(
        _norm_matmul_kernel,
        grid=(n // tm, nout // tn),
        in_specs=[
            pl.BlockSpec((tm, d), lambda i, j: (i, 0)),
            pl.BlockSpec((1, d), lambda i, j: (0, 0)),
            pl.BlockSpec((d, tn), lambda i, j: (0, j)),
        ],
        out_specs=pl.BlockSpec((tm, tn), lambda i, j: (i, j)),
        out_shape=jax.ShapeDtypeStruct((n, nout), out_dtype),
        scratch_shapes=[pltpu.VMEM((tm, d), BF16)],
        compiler_params=_params("parallel", "arbitrary"),
        name="norm_matmul",
    )(x, g, w)


def _gla_kernel(q_ref, k_ref, v_ref, ga_ref, og_ref, s0_ref, w2_ref, bg_ref, gn_ref,
                o_ref, sout_ref, s_scr, *, C, NB):
    keep_s = (_iota((GLA_QK, A_WIDTH), 0) >> 6) == (_iota((GLA_QK, A_WIDTH), 1) >> 7)

    @pl.when(pl.program_id(1) == 0)
    def _():
        for bb in range(NB):
            s_scr[bb] = jnp.where(keep_s, jnp.concatenate([s0_ref[bb]] * GLA_HEADS, axis=1), 0.0)

    chains = [_gla_one(q_ref.at[bb], k_ref.at[bb], v_ref.at[bb], ga_ref.at[bb], og_ref.at[bb],
                       w2_ref, bg_ref, gn_ref, o_ref.at[bb], sout_ref.at[bb], s_scr.at[bb], keep_s, C=C)
              for bb in range(NB)]
    for _ in _lockstep(chains):
        pass


def _gla_one(q_ref, k_ref, v_ref, ga_ref, og_ref, w2_ref, bg_ref, gn_ref,
             o_ref, sout_ref, s_scr, keep_s, *, C):
    x = _dot3(ga_ref[...].astype(F32), w2_ref[...]) + bg_ref[...]
    yield
    logg = (jnp.minimum(x, 0.0) - jnp.log1p(jnp.exp(-jnp.abs(x)))) * (1.0 / GLA_TAU)
    tril = _iota((C, C), 0) >= _iota((C, C), 1)
    b = _dot_exact_lhs(tril, logg)
    g_col = _dot_exact_rhs(logg, jnp.ones((C, LANES), F32), TN)
    yield
    q = q_ref[...].astype(F32) * (GLA_DK ** -0.5)
    k = k_ref[...].astype(F32)
    v = v_ref[...].astype(F32)
    s = s_scr[...]
    o_inter = _dot(q * jnp.exp(b), s)
    bl = b[C - 1:C]
    upd = _dot(k * jnp.exp(bl - b), v, TN)

    head_k = _iota((1, GLA_QK), 1) >> 6
    head_v = _iota((1, A_WIDTH), 1) >> 7
    ones_kv = keep_s.astype(BF16)
    ridx = _iota((SUB, 1), 0)

    def bd(x, head_of_lane):
        xb = x.astype(BF16)
        zero = jnp.zeros_like(xb)
        return jnp.concatenate([jnp.where(head_of_lane == h, xb, zero) for h in range(GLA_HEADS)], axis=0)

    starts = range(0, C, SUB)
    atts = {}
    for i0 in starts[1:]:
        m = b[i0 - 1:i0]
        qi, bi = q[i0:i0 + SUB], b[i0:i0 + SUB]
        atts[i0] = _dot(qi * jnp.exp(bi - m), bd(k[:i0] * jnp.exp(m - b[:i0]), head_k), NT)
    yield
    scs = {}
    for i0 in starts:
        qi, bi, ki = q[i0:i0 + SUB], b[i0:i0 + SUB], k[i0:i0 + SUB]
        cols = []
        for j in range(SUB):
            e = jnp.exp(jnp.minimum(bi - bi[j:j + 1], 0.0))
            cols.append(jnp.where(ridx >= j, qi * e * ki[j:j + 1], 0.0))
        scs[i0] = _dot(jnp.concatenate(cols, axis=0), ones_kv)
    offs = {i0: _dot(atts[i0], bd(v[:i0], head_v)) for i0 in starts[1:]}
    yield
    blocks = []
    for i0 in starts:
        vi = v[i0:i0 + SUB]
        oi = o_inter[i0:i0 + SUB]
        if i0 > 0:
            oi = oi + offs[i0]
        for j in range(SUB):
            oi = oi + scs[i0][j * SUB:(j + 1) * SUB] * vi[j:j + 1]
        blocks.append(oi)
    o = jnp.concatenate(blocks, axis=0)

    e_col = jnp.concatenate([jnp.exp(g_col)] * (A_WIDTH // LANES), axis=1)
    s_new = e_col * s + jnp.where(keep_s, upd, 0.0)
    s_scr[...] = s_new
    sout_ref[...] = (s_new[:, 0:GLA_DV] + s_new[:, GLA_DV:2 * GLA_DV]
                     + s_new[:, 2 * GLA_DV:3 * GLA_DV] + s_new[:, 3 * GLA_DV:4 * GLA_DV])

    outs = [_rms(o[:, h * GLA_DV:(h + 1) * GLA_DV], gn_ref[...]) for h in range(GLA_HEADS)]
    og = og_ref[...].astype(F32)
    o_ref[...] = (jnp.concatenate(outs, axis=1) * (og * jax.nn.sigmoid(og))).astype(o_ref.dtype)


def _gla(u, s0, w2p, bg, gn, B, T, C):
    nc = T // C
    NB = MIXER_BATCH
    s0 = s0.reshape(B, GLA_QK, GLA_DV)
    u = u.reshape(B, T, UW)
    o, s_fin = pl.pallas_call(
        functools.partial(_gla_kernel, C=C, NB=NB),
        grid=(B // NB, nc),
        in_specs=[
            pl.BlockSpec((NB, C, GLA_QK), lambda b, c: (b, c, 0)),
            pl.BlockSpec((NB, C, GLA_QK), lambda b, c: (b, c, 1)),
            pl.BlockSpec((NB, C, A_WIDTH), lambda b, c: (b, c, 1)),
            pl.BlockSpec((NB, C, LANES), lambda b, c: (b, c, 24)),
            pl.BlockSpec((NB, C, A_WIDTH), lambda b, c: (b, c, 2)),
            pl.BlockSpec((NB, GLA_QK, GLA_DV), lambda b, c: (b, 0, 0)),
            pl.BlockSpec((LANES, GLA_QK), lambda b, c: (0, 0)),
            pl.BlockSpec((1, GLA_QK), lambda b, c: (0, 0)),
            pl.BlockSpec((1, GLA_DV), lambda b, c: (0, 0)),
        ],
        out_specs=[
            pl.BlockSpec((NB, C, A_WIDTH), lambda b, c: (b, c, 0)),
            pl.BlockSpec((NB, GLA_QK, GLA_DV), lambda b, c: (b, 0, 0)),
        ],
        out_shape=[
            jax.ShapeDtypeStruct((B, T, A_WIDTH), BF16),
            jax.ShapeDtypeStruct((B, GLA_QK, GLA_DV), F32),
        ],
        scratch_shapes=[pltpu.VMEM((NB, GLA_QK, A_WIDTH), F32)],
        compiler_params=_params("parallel", "arbitrary"),
        name="gla",
    )(u, u, u, u, u, s0, w2p, bg, gn)
    return o.reshape(B * T, A_WIDTH), s_fin.reshape(B, GLA_HEADS, GLA_DK, GLA_DV)


N_RWKV_PER_BATCH = (11, 2, 6)
N_RWKV_SHARED = 15


def _rwkv_kernel(*refs, C, NB):
    n_in, n_out, n_scr = N_RWKV_PER_BATCH
    ins, shared = refs[:n_in], refs[n_in:n_in + N_RWKV_SHARED]
    outs = refs[n_in + N_RWKV_SHARED:n_in + N_RWKV_SHARED + n_out]
    scr = refs[n_in + N_RWKV_SHARED + n_out:]
    GW = GROUP * RWKV_N
    keep_h = (_iota((GW, GW), 0) >> 6) == (_iota((GW, GW), 1) >> 6)

    @pl.when(pl.program_id(1) == 0)
    def _():
        h0_ref, h_scr = ins[10], scr[5]
        for bb in range(NB):
            for prev, sh in zip(scr[:5], ins[5:10]):
                prev[bb] = sh[bb]
            h0 = h0_ref[bb]
            for g in range(RWKV_HEADS // GROUP):
                t = jnp.concatenate([h0[:, g * GW:(g + 1) * GW]] * GROUP, axis=0)
                h_scr[bb, g] = jnp.where(keep_h, t, 0.0)

    at = lambda rs, bb: [r.at[bb] for r in rs]
    chains = [_rwkv_one(*at(ins[:5], bb), *shared, *at(outs, bb), *at(scr, bb), keep_h, C=C) for bb in range(NB)]
    for _ in _lockstep(chains):
        pass


def _rwkv_one(r_ref, k_ref, v_ref, wa_ref, gl_ref,
              mu_r, mu_k, mu_v, mu_wa, mu_gl, w0_ref, w2_ref, a0_ref, a2_ref, g2_ref,
              kk_ref, ka_ref, rk_ref, gng_ref, gnb_ref,
              y_ref, hout_ref,
              pr, pk, pv, pwa, pgl, h_scr, keep_h, *, C):
    GW = GROUP * RWKV_N
    CH = GROUP * C
    logc = C.bit_length() - 1
    n_groups = RWKV_HEADS // GROUP

    def head_mask(rows, cols, row_shift, col_of_lane):
        keep = (_iota((rows, cols), 0) >> row_shift) == col_of_lane(_iota((rows, cols), 1))
        return keep.astype(F32).astype(BF16)

    keep_rows = head_mask(CH, GW, logc, lambda l: l >> 6)
    keep_sq = head_mask(CH, CH, logc, lambda l: l >> logc)

    def bd(x, keep):
        return jnp.concatenate([x.astype(BF16)] * GROUP, axis=0) * keep

    rid = _iota((C, 1), 0)

    def shift(x_ref, prev, mu):
        x = x_ref[...].astype(F32)
        xp = jnp.where(rid == 0, prev[...], pltpu.roll(x, 1, axis=0))
        prev[...] = x[C - 1:C]
        return x + mu[...] * (xp - x)

    rs = shift(r_ref, pr, mu_r)
    ks = shift(k_ref, pk, mu_k)
    vs = shift(v_ref, pv, mu_v)
    was = shift(wa_ref, pwa, mu_wa)
    gls = shift(gl_ref, pgl, mu_gl)

    dec = -_softplus(-(w0_ref[...] + _dot3(jnp.tanh(was), w2_ref[...]))) - 0.5
    logw = -jnp.exp(dec)
    a = jax.nn.sigmoid(a0_ref[...] + _dot(was, a2_ref[...]))
    gate = _dot(jax.nn.sigmoid(gls), g2_ref[...])
    yield
    ones_h = keep_h.astype(F32).astype(BF16)

    def head_sum(x):
        return jnp.concatenate([_dot(x[:, g * GW:(g + 1) * GW], ones_h) for g in range(n_groups)], axis=1)

    kkr = ks * kk_ref[...]
    kk = kkr / jnp.maximum(jnp.sqrt(head_sum(kkr * kkr)), 1e-12)
    km = ks * (1.0 + (a - 1.0) * ka_ref[...])
    beta = kk * a
    tril = _iota((C, C), 0) >= _iota((C, C), 1)
    gc = _dot_exact_lhs(tril, logw)
    g_end = gc[C - 1:C]
    kap = kk * jnp.exp(gc - logw)
    rt = rs * jnp.exp(gc)
    e_neg = jnp.exp(-gc)
    b_inv = beta * e_neg
    k_inv = km * e_neg
    e_end = jnp.exp(g_end - gc)
    k_hat = km * e_end
    b_hat = beta * e_end

    col_s = _iota((C, CH), 1) & (C - 1)
    row_t = _iota((C, CH), 0)
    strict = row_t > col_s
    incl = row_t >= col_s
    same_sub = (row_t >> 4) == (col_s >> 4)
    eye_cat = (row_t == col_s).astype(F32)
    ones_c = jnp.ones((C, LANES), F32)

    yield
    ys = [None] * n_groups

    def group(g):
        L = slice(g * GW, (g + 1) * GW)
        p = jnp.concatenate([kap[:, L], rt[:, L]], axis=0)
        zb = _dot(p, bd(b_inv[:, L], keep_rows), NT)
        zk = _dot(p, bd(k_inv[:, L], keep_rows), NT)
        yield
        a_b = jnp.where(strict, zb[:C], 0.0)
        b_b = jnp.where(incl, zb[C:], 0.0)
        a_k = jnp.where(strict, zk[:C], 0.0)
        b_k = jnp.where(incl, zk[C:], 0.0)
        bd_v = bd(vs[:, L], keep_rows)
        akv = _dot(a_k, bd_v)

        d_blk = jnp.where(same_sub, a_b, 0.0)
        x1 = -d_blk
        p1 = eye_cat + x1
        x2 = _dot(x1, bd(x1, keep_sq))
        hbd = h_scr[g]
        ph = _dot(p, hbd)
        yield
        rhs = ph[:C] + akv
        r2 = _dot(jnp.concatenate([x2, p1], axis=0), bd(x2, keep_sq))
        yield
        x4, p2 = r2[:C], p1 + r2[C:]
        r4 = _dot(jnp.concatenate([x4, p2], axis=0), bd(x4, keep_sq))
        yield
        x8, p3 = r4[:C], p2 + r4[C:]
        t_d = p3 + _dot(p3, bd(x8, keep_sq))
        yield

        n1 = _dot(t_d, bd(a_b - d_blk, keep_sq))
        s1 = _dot(t_d, bd(rhs, keep_rows))
        yield
        n2 = _dot(n1, bd(n1, keep_sq))
        yield
        s2 = s1 + _dot(n2, bd(s1, keep_rows))
        yield
        u = s2 - _dot(n1, bd(s2, keep_rows))
        yield
        y = ph[C:] + _dot(b_k, bd_v) - _dot(b_b, bd(u, keep_rows))
        m = _dot(k_hat[:, L], vs[:, L], TN) - _dot(b_hat[:, L], u, TN)
        g_col = _dot_exact_rhs(logw[:, L], ones_c, TN)[:, :1]
        yield
        h_new = jnp.exp(g_col) * hbd + jnp.where(keep_h, m, 0.0)
        h_scr[g] = h_new
        hout_ref[:, L] = (h_new[0:RWKV_N] + h_new[RWKV_N:2 * RWKV_N]
                          + h_new[2 * RWKV_N:3 * RWKV_N] + h_new[3 * RWKV_N:4 * RWKV_N])
        ys[g] = y

    yield from _lockstep([group(g) for g in range(n_groups)])
    y = jnp.concatenate(ys, axis=1)

    inv_n = 1.0 / RWKV_N
    mu = head_sum(y) * inv_n
    d = y - mu
    var = head_sum(d * d) * inv_n
    yn = d * lax.rsqrt(var + RWKV_GN_EPS) * gng_ref[...] + gnb_ref[...]
    bonus = head_sum(rs * km * rk_ref[...]) * vs
    y_ref[...] = ((yn + bonus) * gate).astype(y_ref.dtype)


def _rwkv(u, shift0, h0, wts, B, T, C):
    nc = T // C
    NB = MIXER_BATCH
    wide = lambda blk: pl.BlockSpec((NB, C, B_WIDTH), lambda b, c: (b, c, blk))
    narrow = lambda blk: pl.BlockSpec((NB, C, LANES), lambda b, c: (b, c, blk))
    sh = lambda w, blk: pl.BlockSpec((NB, 1, w), lambda b, c: (b, 0, blk))
    const = lambda shape: pl.BlockSpec(shape, lambda b, c: (0,) * len(shape))
    assert len(wts) == N_RWKV_SHARED
    in_specs = [wide(3), wide(4), wide(5), narrow(25), narrow(26),
                sh(B_WIDTH, 0), sh(B_WIDTH, 1), sh(B_WIDTH, 2), sh(LANES, 12), sh(LANES, 13),
                pl.BlockSpec((NB, RWKV_N, B_WIDTH), lambda b, c: (b, 0, 0))]
    in_specs += [const(w.shape) for w in wts]
    GW = GROUP * RWKV_N
    u = u.reshape(B, T, UW)
    y, h_fin = pl.pallas_call(
        functools.partial(_rwkv_kernel, C=C, NB=NB),
        grid=(B // NB, nc),
        in_specs=in_specs,
        out_specs=[
            pl.BlockSpec((NB, C, B_WIDTH), lambda b, c: (b, c, 0)),
            pl.BlockSpec((NB, RWKV_N, B_WIDTH), lambda b, c: (b, 0, 0)),
        ],
        out_shape=[
            jax.ShapeDtypeStruct((B, T, B_WIDTH), BF16),
            jax.ShapeDtypeStruct((B, RWKV_N, B_WIDTH), F32),
        ],
        scratch_shapes=[
            pltpu.VMEM((NB, 1, B_WIDTH), F32), pltpu.VMEM((NB, 1, B_WIDTH), F32), pltpu.VMEM((NB, 1, B_WIDTH), F32),
            pltpu.VMEM((NB, 1, LANES), F32), pltpu.VMEM((NB, 1, LANES), F32),
            pltpu.VMEM((NB, RWKV_HEADS // GROUP, GW, GW), F32),
        ],
        compiler_params=_params("parallel", "arbitrary"),
        name="rwkv7",
    )(u, u, u, u, u, shift0, shift0, shift0, shift0, shift0, h0, *wts)
    return y.reshape(B * T, B_WIDTH), h_fin


def _ret_kernel(q_ref, k_ref, v_ref, g_ref, cos_ref, sin_ref, r0_ref, gng_ref, gnb_ref,
                o_ref, rout_ref, r_scr, *, C):
    c = pl.program_id(1)

    @pl.when(c == 0)
    def _():
        r_scr[...] = r0_ref[0]

    cos = cos_ref[...]
    sin = sin_ref[...]
    half = RET_DK // 2

    def rope(x):
        x1, x2 = x[:, :half], x[:, half:]
        return jnp.concatenate([x1 * cos - x2 * sin, x1 * sin + x2 * cos], axis=1)

    ri = _iota((C, C), 0)
    ci = _iota((C, C), 1)
    causal = ri >= ci
    diff = (ri - ci).astype(F32)
    pos1 = (_iota((C, 1), 0) + 1).astype(F32)
    outs = [None] * RET_HEADS

    def head(h):
        lg = math.log1p(-(2.0 ** (-5.0 - h)))
        q = rope(q_ref[:, h * RET_DK:(h + 1) * RET_DK].astype(F32))
        k = rope(k_ref[:, h * RET_DK:(h + 1) * RET_DK].astype(F32)) * (RET_DK ** -0.5)
        v = v_ref[:, h * RET_DV:(h + 1) * RET_DV]
        r = r_scr[h]
        scores = _dot(q, k, NT)
        inter = _dot(q, r)
        yield
        att = scores * jnp.where(causal, jnp.exp(diff * lg), 0.0)
        o = _dot(att, v) + inter * jnp.exp(pos1 * lg)
        r_scr[h] = math.exp(C * lg) * r + _dot(k * jnp.exp((C - pos1) * lg), v, TN)
        yield
        mu = jnp.mean(o, axis=-1, keepdims=True)
        d = o - mu
        var = jnp.mean(d * d, axis=-1, keepdims=True)
        outs[h] = d * lax.rsqrt(var + RET_GN_EPS)

    for _ in _lockstep([head(h) for h in range(RET_HEADS)]):
        pass
    g = g_ref[...].astype(F32)
    o = jnp.concatenate(outs, axis=1) * gng_ref[...] + gnb_ref[...]
    o_ref[...] = ((g * jax.nn.sigmoid(g)) * o).astype(o_ref.dtype)
    rout_ref[0] = r_scr[...]


def _ret(u, cos, sin, r0, gn_g, gn_b, B, T, C):
    nc = T // C
    row = lambda b, c: b * nc + c
    return pl.pallas_call(
        functools.partial(_ret_kernel, C=C),
        grid=(B, nc),
        in_specs=[
            pl.BlockSpec((C, RET_QK), lambda b, c: (row(b, c), 0)),
            pl.BlockSpec((C, RET_QK), lambda b, c: (row(b, c), 1)),
            pl.BlockSpec((C, RET_V), lambda b, c: (row(b, c), 1)),
            pl.BlockSpec((C, RET_V), lambda b, c: (row(b, c), 2)),
            pl.BlockSpec((C, RET_DK // 2), lambda b, c: (c, 0)),
            pl.BlockSpec((C, RET_DK // 2), lambda b, c: (c, 0)),
            pl.BlockSpec((1, RET_HEADS, RET_DK, RET_DV), lambda b, c: (b, 0, 0, 0)),
            pl.BlockSpec((1, RET_V), lambda b, c: (0, 0)),
            pl.BlockSpec((1, RET_V), lambda b, c: (0, 0)),
        ],
        out_specs=[
            pl.BlockSpec((C, RET_V), lambda b, c: (row(b, c), 0)),
            pl.BlockSpec((1, RET_HEADS, RET_DK, RET_DV), lambda b, c: (b, 0, 0, 0)),
        ],
        out_shape=[
            jax.ShapeDtypeStruct((B * T, RET_V), BF16),
            jax.ShapeDtypeStruct((B, RET_HEADS, RET_DK, RET_DV), F32),
        ],
        scratch_shapes=[pltpu.VMEM((RET_HEADS, RET_DK, RET_DV), F32)],
        compiler_params=_params("parallel", "arbitrary"),
        name="retention",
    )(u, u, u, u, cos, sin, r0, gn_g, gn_b)


def _router_kernel(res_ref, a_ref, wo_ref, g_ref, wr_ref, x_ref, ids_ref, gsel_ref):
    x = res_ref[...] + jnp.dot(a_ref[...], wo_ref[...], preferred_element_type=F32)
    x_ref[...] = x
    h = _rms(x, g_ref[...])
    hi, lo = _split(h)
    parts = jnp.dot(jnp.concatenate([hi, lo], axis=1), wr_ref[...], preferred_element_type=F32)
    logits = parts + pltpu.roll(parts, LANES - N_EXPERTS, axis=1)
    lane = _iota(logits.shape, 1)
    valid = lane < N_EXPERTS
    logits = jnp.where(valid, logits, -jnp.inf)
    m = jnp.max(logits, axis=-1, keepdims=True)
    e = jnp.where(valid, jnp.exp(logits - m), 0.0)
    p = e / jnp.sum(e, axis=-1, keepdims=True)
    big = jnp.int32(LANES)
    p1 = jnp.max(p, axis=-1, keepdims=True)
    i1 = jnp.min(jnp.where(p == p1, lane, big), axis=-1, keepdims=True)
    rest = jnp.where((lane == i1) | ~valid, -1.0, p)
    p2 = jnp.max(rest, axis=-1, keepdims=True)
    i2 = jnp.min(jnp.where(rest == p2, lane, big), axis=-1, keepdims=True)
    tot = p1 + p2
    ids_ref[...] = jnp.where(lane == 0, i1, i2)
    gsel_ref[...] = jnp.where(lane == 0, p1 / tot, p2 / tot)


def _router_operand(wr):
    d, e = wr.shape
    whi = wr.astype(BF16)
    wlo = (wr - whi.astype(F32)).astype(BF16)
    top = jnp.concatenate([whi, wlo, jnp.zeros((d, LANES - 2 * e), BF16)], axis=1)
    bottom = jnp.concatenate([whi, jnp.zeros((d, LANES - e), BF16)], axis=1)
    return jnp.concatenate([top, bottom], axis=0)


def _router(res, a, wo, g, wr):
    n, d = res.shape
    tm = min(n, 512)
    row = lambda w: pl.BlockSpec((tm, w), lambda i: (i, 0))
    const = lambda s: pl.BlockSpec(s, lambda i: (0, 0))
    return pl.pallas_call(
        _router_kernel,
        grid=(n // tm,),
        in_specs=[row(d), row(a.shape[1]), const(wo.shape), const((1, d)), const(wr.shape)],
        out_specs=[row(d), row(LANES), row(LANES)],
        out_shape=[jax.ShapeDtypeStruct((n, d), F32), jax.ShapeDtypeStruct((n, LANES), jnp.int32),
                   jax.ShapeDtypeStruct((n, LANES), F32)],
        compiler_params=_params("parallel"),
        name="router",
    )(res, a, wo, g, wr)


def _dispatch_kernel(ends_ref, pos1_ref, pos2_ref, x_ref, xs_ref, zbuf, zsem, sem, *, TQ, TM):
    t = pl.program_id(0)

    @pl.when(t == 0)
    def _():
        zbuf[...] = jnp.zeros_like(zbuf)
        tails = [pltpu.make_async_copy(zbuf, xs_ref.at[pl.ds(pl.multiple_of(ends_ref[e] - TM, TM), TM)], zsem)
                 for e in range(N_EXPERTS)]
        for cp in tails:
            cp.start()
        for cp in tails:
            cp.wait()

        def zero_tile(i, carry):
            cp = pltpu.make_async_copy(zbuf, xs_ref.at[pl.ds(pl.multiple_of(i * TM, TM), TM)], zsem)
            cp.start()
            cp.wait()
            return carry

        lax.fori_loop(ends_ref[N_EXPERTS - 1] // TM, xs_ref.shape[0] // TM, zero_tile, 0)

    def row_copy(r, pos_ref):
        return pltpu.make_async_copy(x_ref.at[pl.ds(r, 1)], xs_ref.at[pl.ds(pos_ref[0, 0, r], 1)], sem)

    def issue(r, carry):
        row_copy(r, pos1_ref).start(priority=0)
        row_copy(r, pos2_ref).start(priority=1)
        return carry

    lax.fori_loop(0, TQ, issue, 0, unroll=8)

    def drain(r, carry):
        row_copy(r, pos1_ref).wait()
        row_copy(r, pos2_ref).wait()
        return carry

    lax.fori_loop(0, TQ, drain, 0, unroll=8)


def _dispatch(x, pos, ends, p, tm):
    n, d = x.shape
    tq = min(n, DISPATCH_ROWS)
    nt = n // tq
    pos3 = pos.reshape(2 * nt, 1, tq)
    grid_spec = pltpu.PrefetchScalarGridSpec(
        num_scalar_prefetch=1,
        grid=(nt,),
        in_specs=[pl.BlockSpec((1, 1, tq), lambda t, ends: (t, 0, 0), memory_space=pltpu.SMEM),
                  pl.BlockSpec((1, 1, tq), lambda t, ends: (t + nt, 0, 0), memory_space=pltpu.SMEM),
                  pl.BlockSpec((tq, d), lambda t, ends: (t, 0))],
        out_specs=pl.BlockSpec(memory_space=pl.ANY),
        scratch_shapes=[pltpu.VMEM((tm, d), F32), pltpu.SemaphoreType.DMA(()), pltpu.SemaphoreType.DMA(())],
    )
    return pl.pallas_call(
        functools.partial(_dispatch_kernel, TQ=tq, TM=tm),
        grid_spec=grid_spec,
        out_shape=jax.ShapeDtypeStruct((p, d), F32),
        compiler_params=_params("arbitrary"),
        name="dispatch",
    )(ends, pos3, pos3, x)


def _moe_ffn_kernel(te_ref, nu_ref, x_ref, g_ref, *refs, tf, residual, n_pre):
    pre_a = refs[:n_pre]
    pre_w = refs[n_pre:n_pre + min(n_pre, 1)]
    w1_ref, w3_ref, w2_ref, o_ref, acc = refs[n_pre + len(pre_w):]
    t = pl.program_id(0)

    @pl.when(t < nu_ref[0])
    def _():
        x = x_ref[...]
        if pre_a:
            a = jnp.concatenate([a_ref[...] for a_ref in pre_a], axis=1)
            x = x + jnp.dot(a, pre_w[0][...], preferred_element_type=F32)
        h = _rms(x, g_ref[...]).astype(BF16)
        if residual:
            acc[...] = x
        for i, f0 in enumerate(range(0, D_FF, tf)):
            a1 = jnp.dot(h, w1_ref[0, :, f0:f0 + tf], preferred_element_type=F32)
            a3 = jnp.dot(h, w3_ref[0, :, f0:f0 + tf], preferred_element_type=F32)
            act = (a1 * jax.nn.sigmoid(a1) * a3).astype(BF16)
            part = jnp.dot(act, w2_ref[0, f0:f0 + tf, :], preferred_element_type=F32)
            if i == 0 and not residual:
                acc[...] = part
            elif f0 + tf < D_FF:
                acc[...] += part
            else:
                o_ref[...] = acc[...] + part

    @pl.when(t >= nu_ref[0])
    def _():
        o_ref[...] = jnp.zeros_like(o_ref)


def _ffn(res, a_list, w_out, g, w1, w3, w2):
    tm = min(res.shape[0], MOE_TILE)
    nt = res.shape[0] // tm
    return _moe_ffn(res, g, jnp.zeros((nt,), jnp.int32), jnp.full((1,), nt, jnp.int32), w1, w3, w2, tm,
                    residual=True, pre=(a_list, [w_out]))


def _moe_ffn(xs, g, tile_expert, n_used, w1, w3, w2, tm, residual=False, pre=((), ())):
    p, d = xs.shape
    dff = w1.shape[2]
    row_tile = lambda t, te, nu: (jnp.minimum(t, nu[0] - 1), 0)
    pre_a, pre_w = pre
    grid_spec = pltpu.PrefetchScalarGridSpec(
        num_scalar_prefetch=2,
        grid=(p // tm,),
        in_specs=[
            pl.BlockSpec((tm, d), row_tile),
            pl.BlockSpec((1, d), lambda t, te, nu: (0, 0)),
            *[pl.BlockSpec((tm, a.shape[1]), row_tile) for a in pre_a],
            *[pl.BlockSpec(w.shape, lambda t, te, nu: (0, 0)) for w in pre_w],
            pl.BlockSpec((1, d, dff), lambda t, te, nu: (te[t], 0, 0)),
            pl.BlockSpec((1, d, dff), lambda t, te, nu: (te[t], 0, 0)),
            pl.BlockSpec((1, dff, d), lambda t, te, nu: (te[t], 0, 0)),
        ],
        out_specs=pl.BlockSpec((tm, d), lambda t, te, nu: (t, 0)),
        scratch_shapes=[pltpu.VMEM((tm, d), F32)],
    )
    return pl.pallas_call(
        functools.partial(_moe_ffn_kernel, tf=256, residual=residual, n_pre=len(pre_a)),
        grid_spec=grid_spec,
        out_shape=jax.ShapeDtypeStruct((p, d), F32),
        compiler_params=_params("arbitrary"),
        name="moe_ffn",
    )(tile_expert, n_used, xs, g, *pre_a, *pre_w, w1, w3, w2)


def _combine_kernel(pos1_ref, pos2_ref, pos1n_ref, pos2n_ref, x_ref, gsel_ref, gf_ref, ys_ref, o_ref,
                    buf, sems, *, TQ):
    t = pl.program_id(0)
    slot = t % 2

    def row_copy(r, pos_ref, s, k):
        return pltpu.make_async_copy(ys_ref.at[pl.ds(pos_ref[0, 0, r], 1)], buf.at[s, k, pl.ds(r, 1)], sems.at[s])

    def issue(p1_ref, p2_ref, s):
        def body(r, carry):
            row_copy(r, p1_ref, s, 0).start(priority=0)
            row_copy(r, p2_ref, s, 1).start(priority=1)
            return carry

        lax.fori_loop(0, TQ, body, 0, unroll=8)

    @pl.when(t == 0)
    def _():
        issue(pos1_ref, pos2_ref, 0)

    @pl.when(t + 1 < pl.num_programs(0))
    def _():
        issue(pos1n_ref, pos2n_ref, 1 - slot)

    def drain(r, carry):
        row_copy(r, pos1_ref, slot, 0).wait()
        row_copy(r, pos2_ref, slot, 1).wait()
        return carry

    lax.fori_loop(0, TQ, drain, 0, unroll=8)
    gs = gsel_ref[...]
    y = x_ref[...] + gs[:, 0:1] * buf[slot, 0] + gs[:, 1:2] * buf[slot, 1]
    o_ref[...] = _rms(y, gf_ref[...])


def _combine(x, ys, pos, gsel, g_final):
    n, d = x.shape
    tq = min(n, DISPATCH_ROWS)
    nt = n // tq
    pos3 = pos.reshape(2 * nt, 1, tq)
    smem = lambda f: pl.BlockSpec((1, 1, tq), f, memory_space=pltpu.SMEM)
    nxt = lambda t: jnp.minimum(t + 1, nt - 1)
    return pl.pallas_call(
        functools.partial(_combine_kernel, TQ=tq),
        grid=(nt,),
        in_specs=[smem(lambda t: (t, 0, 0)), smem(lambda t: (t + nt, 0, 0)),
                  smem(lambda t: (nxt(t), 0, 0)), smem(lambda t: (nxt(t) + nt, 0, 0)),
                  pl.BlockSpec((tq, d), lambda t: (t, 0)),
                  pl.BlockSpec((tq, LANES), lambda t: (t, 0)),
                  pl.BlockSpec((1, d), lambda t: (0, 0)),
                  pl.BlockSpec(memory_space=pl.ANY)],
        out_specs=pl.BlockSpec((tq, d), lambda t: (t, 0)),
        out_shape=jax.ShapeDtypeStruct((n, d), F32),
        scratch_shapes=[pltpu.VMEM((2, 2, tq, d), F32), pltpu.SemaphoreType.DMA((2,))],
        compiler_params=_params("arbitrary"),
        name="combine",
    )(pos3, pos3, pos3, pos3, x, gsel, g_final, ys)


def _moe_plan(ids, tm):
    n = ids.shape[0]
    e_flat = ids.T.reshape(-1)
    experts = jnp.arange(N_EXPERTS, dtype=jnp.int32)
    onehot = (e_flat[:, None] == experts[None, :]).astype(jnp.int32)
    csum = jnp.cumsum(onehot, axis=0)
    counts = csum[-1]
    padded = jnp.maximum((counts + tm - 1) // tm, 1) * tm
    ends = jnp.cumsum(padded).astype(jnp.int32)
    pos = jnp.sum((csum - 1 + (ends - padded)[None, :]) * onehot, axis=1).astype(jnp.int32)
    p = 2 * n + N_EXPERTS * tm
    n_used = ends[-1] // tm
    tile_start = jnp.minimum(jnp.arange(p // tm, dtype=jnp.int32) * tm, ends[-1] - 1)
    tile_expert = jnp.sum((tile_start[:, None] >= ends[None, :]).astype(jnp.int32), axis=1)
    return pos, ends, tile_expert, n_used.reshape(1), p


def _prep_weights(p):
    w_in = p["ab_w_in"][0]
    wa, wb = w_in[:, :A_COLS], w_in[:, A_COLS:]
    pad = jnp.zeros((D_MODEL, LANES - GLA_GATE_RANK), F32)
    w_in_p = jnp.concatenate(
        [wa[:, :2 * GLA_QK + A_WIDTH], wa[:, 2 * GLA_QK + A_WIDTH + GLA_GATE_RANK:], wb[:, :3 * B_WIDTH],
         wa[:, 2 * GLA_QK + A_WIDTH:2 * GLA_QK + A_WIDTH + GLA_GATE_RANK], pad, wb[:, 3 * B_WIDTH:]], axis=1)
    row = lambda v: v.reshape(1, -1).astype(F32)
    mu = p["rwkv_mu"][0]
    z64 = jnp.zeros((64, B_WIDTH), F32)
    w = dict(
        norm_mix=p["norm_mix"], norm_ffn=p["norm_ffn"], norm_final=row(p["norm_final"]),
        w_in=w_in_p.astype(BF16),
        w_out=p["ab_w_out"][0].astype(BF16),
        gla_w2=jnp.concatenate([p["gla_w_gate2"][0], jnp.zeros((LANES - GLA_GATE_RANK, GLA_QK), F32)], axis=0),
        gla_bg=row(p["gla_b_gate"][0]), gla_norm=row(p["gla_norm"][0]),
        rwkv=(row(mu[:B_WIDTH]), row(mu[B_WIDTH:2 * B_WIDTH]), row(mu[2 * B_WIDTH:3 * B_WIDTH]),
              row(mu[3 * B_WIDTH:3 * B_WIDTH + LANES]), row(mu[3 * B_WIDTH + LANES:]),
              row(p["rwkv_w0"][0]), jnp.concatenate([p["rwkv_w2"][0], z64], axis=0),
              row(p["rwkv_a0"][0]), jnp.concatenate([z64, p["rwkv_a2"][0]], axis=0),
              p["rwkv_g2"][0], row(p["rwkv_k_k"][0]), row(p["rwkv_k_a"][0]), row(p["rwkv_r_k"][0]),
              row(p["rwkv_gn_g"][0]), row(p["rwkv_gn_b"][0])),
        ret_w_in=p["ret_w_in"][0].astype(BF16), ret_w_out=p["ret_w_out"][0].astype(BF16),
        ret_gn_g=row(p["ret_gn_g"][0]), ret_gn_b=row(p["ret_gn_b"][0]),
        ffn_w1=p["ffn_w1"].astype(BF16), ffn_w3=p["ffn_w3"].astype(BF16), ffn_w2=p["ffn_w2"].astype(BF16),
        router=_router_operand(p["moe_router"][0]),
        moe_w1=p["moe_w1"][0].astype(BF16), moe_w3=p["moe_w3"][0].astype(BF16), moe_w2=p["moe_w2"][0].astype(BF16),
    )
    return w


def _trunk(x, pos0, st_gla, st_rwkv, st_shift, st_ret, w):
    B, T, D = x.shape
    n = B * T
    x0 = x.reshape(n, D)
    c_ab = min(CHUNK, T)
    c_ret = min(256, T)

    u = _norm_matmul(x0, w["norm_mix"][0:1], w["w_in"], tm=1024, tn=UW, out_dtype=BF16)
    oa, s_gla = _gla(u, st_gla, w["gla_w2"], w["gla_bg"], w["gla_norm"], B, T, c_ab)
    h0 = st_rwkv.transpose(0, 3, 1, 2).reshape(B, RWKV_N, B_WIDTH)
    yb, h_fin = _rwkv(u, st_shift, h0, w["rwkv"], B, T, c_ab)
    s_rwkv = h_fin.reshape(B, RWKV_N, RWKV_HEADS, RWKV_N).transpose(0, 2, 3, 1)
    u_last = u.reshape(B, T, UW)[:, T - 1:, :]
    s_shift = jnp.concatenate([u_last[..., 3 * A_WIDTH:6 * A_WIDTH], u_last[..., UW - 2 * LANES:]], axis=-1).astype(F32)
    x2 = _ffn(x0, [oa, yb], w["w_out"], w["norm_ffn"][0:1], w["ffn_w1"], w["ffn_w3"], w["ffn_w2"])

    ur = _norm_matmul(x2, w["norm_mix"][1:2], w["ret_w_in"], tm=1024, tn=3072, out_dtype=BF16)
    half = RET_DK // 2
    inv = 1.0 / (ROPE_BASE ** (jnp.arange(0, RET_DK, 2, dtype=F32) / RET_DK))
    ang = (pos0 + jnp.arange(T)).astype(F32)[:, None] * inv[None, :]
    o, s_ret = _ret(ur, jnp.cos(ang), jnp.sin(ang), st_ret, w["ret_gn_g"], w["ret_gn_b"], B, T, c_ret)
    x3, ids, gsel = _router(x2, o, w["ret_w_out"], w["norm_ffn"][1:2], w["router"])
    tm = min(MOE_TILE, n // 4)
    pos, ends, tile_expert, n_used, p = _moe_plan(ids[:, :2], tm)
    xs = _dispatch(x3, pos, ends, p, tm)
    ys = _moe_ffn(xs, w["norm_ffn"][1:2], tile_expert, n_used, w["moe_w1"], w["moe_w3"], w["moe_w2"], tm)
    y = _combine(x3, ys, pos, gsel, w["norm_final"])
    return y.reshape(B, T, D), s_gla[None], s_rwkv[None], s_shift[None], s_ret[None]


def kernel(x_prompt, x_sample, state_gla, state_rwkv, state_shift, state_ret, norm_mix, norm_ffn, norm_final, ab_w_in, ab_w_out, gla_w_gate2, gla_b_gate, gla_norm, rwkv_mu, rwkv_w0, rwkv_w2, rwkv_a0, rwkv_a2, rwkv_g2, rwkv_k_k, rwkv_k_a, rwkv_r_k, rwkv_gn_g, rwkv_gn_b, ret_w_in, ret_gn_g, ret_gn_b, ret_w_out, ffn_w1, ffn_w3, ffn_w2, moe_router, moe_w1, moe_w3, moe_w2):
    p = dict(norm_mix=norm_mix, norm_ffn=norm_ffn, norm_final=norm_final, ab_w_in=ab_w_in, ab_w_out=ab_w_out,
             gla_w_gate2=gla_w_gate2, gla_b_gate=gla_b_gate, gla_norm=gla_norm, rwkv_mu=rwkv_mu, rwkv_w0=rwkv_w0,
             rwkv_w2=rwkv_w2, rwkv_a0=rwkv_a0, rwkv_a2=rwkv_a2, rwkv_g2=rwkv_g2, rwkv_k_k=rwkv_k_k,
             rwkv_k_a=rwkv_k_a, rwkv_r_k=rwkv_r_k, rwkv_gn_g=rwkv_gn_g, rwkv_gn_b=rwkv_gn_b, ret_w_in=ret_w_in,
             ret_gn_g=ret_gn_g, ret_gn_b=ret_gn_b, ret_w_out=ret_w_out, ffn_w1=ffn_w1, ffn_w3=ffn_w3, ffn_w2=ffn_w2,
             moe_router=moe_router, moe_w1=moe_w1, moe_w3=moe_w3, moe_w2=moe_w2)
    w = _prep_weights(p)
    bp, tp = x_prompt.shape[0], x_prompt.shape[1]
    dt = x_prompt.dtype
    z_gla = jnp.zeros((bp, GLA_HEADS, GLA_DK, GLA_DV), dt)
    z_rwkv = jnp.zeros((bp, RWKV_HEADS, RWKV_N, RWKV_N), dt)
    z_shift = jnp.zeros((bp, 1, B_COLS), dt)
    z_ret = jnp.zeros((bp, RET_HEADS, RET_DK, RET_DV), dt)
    past = 2048
    yp = _trunk(x_prompt, 0, z_gla, z_rwkv, z_shift, z_ret, w)
    ys = _trunk(x_sample, past, state_gla[0], state_rwkv[0], state_shift[0], state_ret[0], w)
    return (yp[0], ys[0], yp[1], yp[2], yp[3], yp[4], ys[1], ys[2], ys[3], ys[4])
```

```python
import functools
import math

import jax
import jax.numpy as jnp
from jax import lax
from jax.experimental import pallas as pl
from jax.experimental.pallas import tpu as pltpu

F32 = jnp.float32
BF16 = jnp.bfloat16

D_MODEL = 1024
EPS = 1e-6
GLA_HEADS, GLA_DK, GLA_DV = 4, 64, 128
GLA_QK = GLA_HEADS * GLA_DK
A_WIDTH = GLA_HEADS * GLA_DV
GLA_GATE_RANK = 16
GLA_TAU = 16.0
RWKV_HEADS, RWKV_N = 8, 64
B_WIDTH = RWKV_HEADS * RWKV_N
RWKV_GN_EPS = 64e-5
B_COLS = 3 * B_WIDTH + 64 + 64 + 128
A_COLS = 2 * GLA_QK + A_WIDTH + GLA_GATE_RANK + A_WIDTH
RET_HEADS = 4
RET_DK = D_MODEL // RET_HEADS
RET_DV = 2 * D_MODEL // RET_HEADS
RET_QK = RET_HEADS * RET_DK
RET_V = RET_HEADS * RET_DV
RET_GN_EPS = 1e-5
ROPE_BASE = 10000.0
D_FF = 2816
N_EXPERTS = 8
PAST_LEN = 2048
CHUNK = 64
SUB = 16
LANES = 128
GROUP = 4
MIXER_BATCH = 8
MOE_TILE = 512
DISPATCH_ROWS = 512
UW = 3456
VMEM_LIMIT = 56 * 1024 * 1024

NN = (((1,), (0,)), ((), ()))
NT = (((1,), (1,)), ((), ()))
TN = (((0,), (0,)), ((), ()))


def _dot(a, b, dims=NN):
    return lax.dot_general(a.astype(BF16), b.astype(BF16), dims, preferred_element_type=F32)


def _split(x):
    hi = x.astype(BF16)
    lo = (x - hi.astype(F32)).astype(BF16)
    return hi, lo


def _dot3(a, b, dims=NN):
    ah, al = _split(a)
    bh, bl = _split(b)
    dg = lambda x, y: lax.dot_general(x, y, dims, preferred_element_type=F32)
    return dg(ah, bh) + dg(ah, bl) + dg(al, bh)


def _dot_exact_lhs(m, x, dims=NN):
    xh, xl = _split(x)
    mb = m.astype(BF16)
    dg = lambda y: lax.dot_general(mb, y, dims, preferred_element_type=F32)
    return dg(xh) + dg(xl)


def _dot_exact_rhs(x, m, dims=NN):
    xh, xl = _split(x)
    mb = m.astype(BF16)
    dg = lambda y: lax.dot_general(y, mb, dims, preferred_element_type=F32)
    return dg(xh) + dg(xl)


def _iota(shape, axis):
    return lax.broadcasted_iota(jnp.int32, shape, axis)


def _softplus(z):
    return jnp.maximum(z, 0.0) + jnp.log1p(jnp.exp(-jnp.abs(z)))


def _rms(x, g):
    return x * lax.rsqrt(jnp.mean(x * x, axis=-1, keepdims=True) + EPS) * g


def _lockstep(chains):
    chains = list(chains)
    while chains:
        alive = []
        for ch in chains:
            try:
                next(ch)
                alive.append(ch)
            except StopIteration:
                pass
        chains = alive
        yield


def _params(*sem):
    return pltpu.CompilerParams(dimension_semantics=sem, vmem_limit_bytes=VMEM_LIMIT)


def _norm_matmul_kernel(x_ref, g_ref, w_ref, *refs, rope_scale):
    if rope_scale is None:
        o_ref, h_ref = refs
    else:
        cos_ref, sin_ref, o_ref, h_ref = refs
    j = pl.program_id(1)

    @pl.when(j == 0)
    def _():
        h_ref[...] = _rms(x_ref[...], g_ref[...]).astype(BF16)

    acc = jnp.dot(h_ref[...], w_ref[...], preferred_element_type=F32)
    if rope_scale is None:
        o_ref[...] = acc.astype(o_ref.dtype)
    else:
        @pl.when(j == 0)
        def _():
            cos, sin = cos_ref[...], sin_ref[...]
            half = RET_DK // 2
            n_heads = acc.shape[1] // RET_DK
            for hb in range(n_heads):
                s = rope_scale[hb * 2 // n_heads]
                x1 = acc[:, hb * RET_DK:hb * RET_DK + half]
                x2 = acc[:, hb * RET_DK + half:(hb + 1) * RET_DK]
                o_ref[:, hb * RET_DK:hb * RET_DK + half] = ((x1 * cos - x2 * sin) * s).astype(o_ref.dtype)
                o_ref[:, hb * RET_DK + half:(hb + 1) * RET_DK] = ((x1 * sin + x2 * cos) * s).astype(o_ref.dtype)

        @pl.when(j > 0)
        def _():
            o_ref[...] = acc.astype(o_ref.dtype)


def _norm_matmul(x, g, w, tm, tn, out_dtype, rope=None):
    n, d = x.shape
    nout = w.shape[1]
    tm = min(n, tm)
    in_specs = [
        pl.BlockSpec((tm, d), lambda i, j: (i, 0)),
        pl.BlockSpec((1, d), lambda i, j: (0, 0)),
        pl.BlockSpec((d, tn), lambda i, j: (0, j)),
    ]
    args = [x, g, w]
    rope_scale = None
    if rope is not None:
        cos, sin, q_scale, k_scale = rope
        assert tn == 2 * RET_QK
        reps = max(tm // cos.shape[0], 1)
        cos, sin = jnp.tile(cos, (reps, 1)), jnp.tile(sin, (reps, 1))
        nblk = cos.shape[0] // tm
        in_specs += [pl.BlockSpec((tm, RET_DK // 2), lambda i, j: (i % nblk, 0))] * 2
        args += [cos, sin]
        rope_scale = (q_scale, k_scale)
    return pl.pallas_call(
        functools.partial(_norm_matmul_kernel, rope_scale=rope_scale),
        grid=(n // tm, nout // tn),
        in_specs=in_specs,
        out_specs=pl.BlockSpec((tm, tn), lambda i, j: (i, j)),
        out_shape=jax.ShapeDtypeStruct((n, nout), out_dtype),
        scratch_shapes=[pltpu.VMEM((tm, d), BF16)],
        compiler_params=_params("parallel", "arbitrary"),
        name="norm_matmul",
    )(*args)


def _gla_kernel(q_ref, k_ref, v_ref, ga_ref, og_ref, s0_ref, w2_ref, bg_ref, gn_ref,
                o_ref, sout_ref, s_scr, *, C, NB):
    keep_s = (_iota((GLA_QK, A_WIDTH), 0) >> 6) == (_iota((GLA_QK, A_WIDTH), 1) >> 7)

    @pl.when(pl.program_id(1) == 0)
    def _():
        for bb in range(NB):
            s_scr[bb] = jnp.where(keep_s, jnp.concatenate([s0_ref[bb]] * GLA_HEADS, axis=1), 0.0)

    chains = [_gla_one(q_ref.at[bb], k_ref.at[bb], v_ref.at[bb], ga_ref.at[bb], og_ref.at[bb],
                       w2_ref, bg_ref, gn_ref, o_ref.at[bb], sout_ref.at[bb], s_scr.at[bb], keep_s, C=C)
              for bb in range(NB)]
    for _ in _lockstep(chains):
        pass


def _gla_one(q_ref, k_ref, v_ref, ga_ref, og_ref, w2_ref, bg_ref, gn_ref,
             o_ref, sout_ref, s_scr, keep_s, *, C):
    x = _dot3(ga_ref[...].astype(F32), w2_ref[...]) + bg_ref[...]
    yield
    logg = (jnp.minimum(x, 0.0) - jnp.log1p(jnp.exp(-jnp.abs(x)))) * (1.0 / GLA_TAU)
    tril = _iota((C, C), 0) >= _iota((C, C), 1)
    b = _dot_exact_lhs(tril, logg)
    g_col = _dot_exact_rhs(logg, jnp.ones((C, LANES), F32), TN)
    yield
    q = q_ref[...].astype(F32) * (GLA_DK ** -0.5)
    k = k_ref[...].astype(F32)
    v = v_ref[...].astype(F32)
    s = s_scr[...]
    o_inter = _dot(q * jnp.exp(b), s)
    bl = b[C - 1:C]
    upd = _dot(k * jnp.exp(bl - b), v, TN)

    head_k = _iota((1, GLA_QK), 1) >> 6
    head_v = _iota((1, A_WIDTH), 1) >> 7
    ones_kv = keep_s.astype(BF16)
    ridx = _iota((SUB, 1), 0)

    def bd(x, head_of_lane):
        xb = x.astype(BF16)
        zero = jnp.zeros_like(xb)
        return jnp.concatenate([jnp.where(head_of_lane == h, xb, zero) for h in range(GLA_HEADS)], axis=0)

    starts = range(0, C, SUB)
    atts = {}
    for i0 in starts[1:]:
        m = b[i0 - 1:i0]
        qi, bi = q[i0:i0 + SUB], b[i0:i0 + SUB]
        atts[i0] = _dot(qi * jnp.exp(bi - m), bd(k[:i0] * jnp.exp(m - b[:i0]), head_k), NT)
    yield
    scs = {}
    for i0 in starts:
        qi, bi, ki = q[i0:i0 + SUB], b[i0:i0 + SUB], k[i0:i0 + SUB]
        cols = []
        for j in range(SUB):
            e = jnp.exp(jnp.minimum(bi - bi[j:j + 1], 0.0))
            cols.append(jnp.where(ridx >= j, qi * e * ki[j:j + 1], 0.0))
        scs[i0] = _dot(jnp.concatenate(cols, axis=0), ones_kv)
    offs = {i0: _dot(atts[i0], bd(v[:i0], head_v)) for i0 in starts[1:]}
    yield
    blocks = []
    for i0 in starts:
        vi = v[i0:i0 + SUB]
        oi = o_inter[i0:i0 + SUB]
        if i0 > 0:
            oi = oi + offs[i0]
        for j in range(SUB):
            oi = oi + scs[i0][j * SUB:(j + 1) * SUB] * vi[j:j + 1]
        blocks.append(oi)
    o = jnp.concatenate(blocks, axis=0)

    e_col = jnp.concatenate([jnp.exp(g_col)] * (A_WIDTH // LANES), axis=1)
    s_new = e_col * s + jnp.where(keep_s, upd, 0.0)
    s_scr[...] = s_new
    sout_ref[...] = (s_new[:, 0:GLA_DV] + s_new[:, GLA_DV:2 * GLA_DV]
                     + s_new[:, 2 * GLA_DV:3 * GLA_DV] + s_new[:, 3 * GLA_DV:4 * GLA_DV])

    outs = [_rms(o[:, h * GLA_DV:(h + 1) * GLA_DV], gn_ref[...]) for h in range(GLA_HEADS)]
    og = og_ref[...].astype(F32)
    o_ref[...] = (jnp.concatenate(outs, axis=1) * (og * jax.nn.sigmoid(og))).astype(o_ref.dtype)


def _gla(u, s0, w2p, bg, gn, B, T, C):
    nc = T // C
    NB = MIXER_BATCH
    s0 = s0.reshape(B, GLA_QK, GLA_DV)
    u = u.reshape(B, T, UW)
    o, s_fin = pl.pallas_call(
        functools.partial(_gla_kernel, C=C, NB=NB),
        grid=(B // NB, nc),
        in_specs=[
            pl.BlockSpec((NB, C, GLA_QK), lambda b, c: (b, c, 0)),
            pl.BlockSpec((NB, C, GLA_QK), lambda b, c: (b, c, 1)),
            pl.BlockSpec((NB, C, A_WIDTH), lambda b, c: (b, c, 1)),
            pl.BlockSpec((NB, C, LANES), lambda b, c: (b, c, 24)),
            pl.BlockSpec((NB, C, A_WIDTH), lambda b, c: (b, c, 2)),
            pl.BlockSpec((NB, GLA_QK, GLA_DV), lambda b, c: (b, 0, 0)),
            pl.BlockSpec((LANES, GLA_QK), lambda b, c: (0, 0)),
            pl.BlockSpec((1, GLA_QK), lambda b, c: (0, 0)),
            pl.BlockSpec((1, GLA_DV), lambda b, c: (0, 0)),
        ],
        out_specs=[
            pl.BlockSpec((NB, C, A_WIDTH), lambda b, c: (b, c, 0)),
            pl.BlockSpec((NB, GLA_QK, GLA_DV), lambda b, c: (b, 0, 0)),
        ],
        out_shape=[
            jax.ShapeDtypeStruct((B, T, A_WIDTH), BF16),
            jax.ShapeDtypeStruct((B, GLA_QK, GLA_DV), F32),
        ],
        scratch_shapes=[pltpu.VMEM((NB, GLA_QK, A_WIDTH), F32)],
        compiler_params=_params("parallel", "arbitrary"),
        name="gla",
    )(u, u, u, u, u, s0, w2p, bg, gn)
    return o.reshape(B * T, A_WIDTH), s_fin.reshape(B, GLA_HEADS, GLA_DK, GLA_DV)


N_RWKV_PER_BATCH = (11, 2, 6)
N_RWKV_SHARED = 15


def _rwkv_kernel(*refs, C, NB):
    n_in, n_out, n_scr = N_RWKV_PER_BATCH
    ins, shared = refs[:n_in], refs[n_in:n_in + N_RWKV_SHARED]
    outs = refs[n_in + N_RWKV_SHARED:n_in + N_RWKV_SHARED + n_out]
    scr = refs[n_in + N_RWKV_SHARED + n_out:]
    GW = GROUP * RWKV_N
    keep_h = (_iota((GW, GW), 0) >> 6) == (_iota((GW, GW), 1) >> 6)

    @pl.when(pl.program_id(1) == 0)
    def _():
        h0_ref, h_scr = ins[10], scr[5]
        for bb in range(NB):
            for prev, sh in zip(scr[:5], ins[5:10]):
                prev[bb] = sh[bb]
            h0 = h0_ref[bb]
            for g in range(RWKV_HEADS // GROUP):
                t = jnp.concatenate([h0[:, g * GW:(g + 1) * GW]] * GROUP, axis=0)
                h_scr[bb, g] = jnp.where(keep_h, t, 0.0)

    at = lambda rs, bb: [r.at[bb] for r in rs]
    chains = [_rwkv_one(*at(ins[:5], bb), *shared, *at(outs, bb), *at(scr, bb), keep_h, C=C) for bb in range(NB)]
    for _ in _lockstep(chains):
        pass


def _rwkv_one(r_ref, k_ref, v_ref, wa_ref, gl_ref,
              mu_r, mu_k, mu_v, mu_wa, mu_gl, w0_ref, w2_ref, a0_ref, a2_ref, g2_ref,
              kk_ref, ka_ref, rk_ref, gng_ref, gnb_ref,
              y_ref, hout_ref,
              pr, pk, pv, pwa, pgl, h_scr, keep_h, *, C):
    GW = GROUP * RWKV_N
    CH = GROUP * C
    logc = C.bit_length() - 1
    n_groups = RWKV_HEADS // GROUP

    def head_mask(rows, cols, row_shift, col_of_lane):
        keep = (_iota((rows, cols), 0) >> row_shift) == col_of_lane(_iota((rows, cols), 1))
        return keep.astype(F32).astype(BF16)

    keep_rows = head_mask(CH, GW, logc, lambda l: l >> 6)
    keep_sq = head_mask(CH, CH, logc, lambda l: l >> logc)

    def bd(x, keep):
        return jnp.concatenate([x.astype(BF16)] * GROUP, axis=0) * keep

    rid = _iota((C, 1), 0)

    def shift(x_ref, prev, mu):
        x = x_ref[...].astype(F32)
        xp = jnp.where(rid == 0, prev[...], pltpu.roll(x, 1, axis=0))
        prev[...] = x[C - 1:C]
        return x + mu[...] * (xp - x)

    rs = shift(r_ref, pr, mu_r)
    ks = shift(k_ref, pk, mu_k)
    vs = shift(v_ref, pv, mu_v)
    was = shift(wa_ref, pwa, mu_wa)
    gls = shift(gl_ref, pgl, mu_gl)

    dec = -_softplus(-(w0_ref[...] + _dot3(jnp.tanh(was), w2_ref[...]))) - 0.5
    logw = -jnp.exp(dec)
    a = jax.nn.sigmoid(a0_ref[...] + _dot(was, a2_ref[...]))
    gate = _dot(jax.nn.sigmoid(gls), g2_ref[...])
    yield
    ones_h = keep_h.astype(F32).astype(BF16)

    def head_sum(x):
        return jnp.concatenate([_dot(x[:, g * GW:(g + 1) * GW], ones_h) for g in range(n_groups)], axis=1)

    kkr = ks * kk_ref[...]
    kk = kkr / jnp.maximum(jnp.sqrt(head_sum(kkr * kkr)), 1e-12)
    km = ks * (1.0 + (a - 1.0) * ka_ref[...])
    beta = kk * a
    tril = _iota((C, C), 0) >= _iota((C, C), 1)
    gc = _dot_exact_lhs(tril, logw)
    g_end = gc[C - 1:C]
    kap = kk * jnp.exp(gc - logw)
    rt = rs * jnp.exp(gc)
    e_neg = jnp.exp(-gc)
    b_inv = beta * e_neg
    k_inv = km * e_neg
    e_end = jnp.exp(g_end - gc)
    k_hat = km * e_end
    b_hat = beta * e_end

    col_s = _iota((C, CH), 1) & (C - 1)
    row_t = _iota((C, CH), 0)
    strict = row_t > col_s
    incl = row_t >= col_s
    same_sub = (row_t >> 4) == (col_s >> 4)
    eye_cat = (row_t == col_s).astype(F32)
    ones_c = jnp.ones((C, LANES), F32)

    yield
    ys = [None] * n_groups

    def group(g):
        L = slice(g * GW, (g + 1) * GW)
        p = jnp.concatenate([kap[:, L], rt[:, L]], axis=0)
        zb = _dot(p, bd(b_inv[:, L], keep_rows), NT)
        zk = _dot(p, bd(k_inv[:, L], keep_rows), NT)
        yield
        a_b = jnp.where(strict, zb[:C], 0.0)
        b_b = jnp.where(incl, zb[C:], 0.0)
        a_k = jnp.where(strict, zk[:C], 0.0)
        b_k = jnp.where(incl, zk[C:], 0.0)
        bd_v = bd(vs[:, L], keep_rows)
        akv = _dot(a_k, bd_v)

        d_blk = jnp.where(same_sub, a_b, 0.0)
        x1 = -d_blk
        p1 = eye_cat + x1
        x2 = _dot(x1, bd(x1, keep_sq))
        hbd = h_scr[g]
        ph = _dot(p, hbd)
        yield
        rhs = ph[:C] + akv
        r2 = _dot(jnp.concatenate([x2, p1], axis=0), bd(x2, keep_sq))
        yield
        x4, p2 = r2[:C], p1 + r2[C:]
        r4 = _dot(jnp.concatenate([x4, p2], axis=0), bd(x4, keep_sq))
        yield
        x8, p3 = r4[:C], p2 + r4[C:]
        t_d = p3 + _dot(p3, bd(x8, keep_sq))
        yield

        n1 = _dot(t_d, bd(a_b - d_blk, keep_sq))
        s1 = _dot(t_d, bd(rhs, keep_rows))
        yield
        n2 = _dot(n1, bd(n1, keep_sq))
        yield
        s2 = s1 + _dot(n2, bd(s1, keep_rows))
        yield
        u = s2 - _dot(n1, bd(s2, keep_rows))
        yield
        y = ph[C:] + _dot(b_k, bd_v) - _dot(b_b, bd(u, keep_rows))
        m = _dot(k_hat[:, L], vs[:, L], TN) - _dot(b_hat[:, L], u, TN)
        g_col = _dot_exact_rhs(logw[:, L], ones_c, TN)[:, :1]
        yield
        h_new = jnp.exp(g_col) * hbd + jnp.where(keep_h, m, 0.0)
        h_scr[g] = h_new
        hout_ref[:, L] = (h_new[0:RWKV_N] + h_new[RWKV_N:2 * RWKV_N]
                          + h_new[2 * RWKV_N:3 * RWKV_N] + h_new[3 * RWKV_N:4 * RWKV_N])
        ys[g] = y

    yield from _lockstep([group(g) for g in range(n_groups)])
    y = jnp.concatenate(ys, axis=1)

    inv_n = 1.0 / RWKV_N
    mu = head_sum(y) * inv_n
    d = y - mu
    var = head_sum(d * d) * inv_n
    yn = d * lax.rsqrt(var + RWKV_GN_EPS) * gng_ref[...] + gnb_ref[...]
    bonus = head_sum(rs * km * rk_ref[...]) * vs
    y_ref[...] = ((yn + bonus) * gate).astype(y_ref.dtype)


def _rwkv(u, shift0, h0, wts, B, T, C):
    nc = T // C
    NB = MIXER_BATCH
    wide = lambda blk: pl.BlockSpec((NB, C, B_WIDTH), lambda b, c: (b, c, blk))
    narrow = lambda blk: pl.BlockSpec((NB, C, LANES), lambda b, c: (b, c, blk))
    sh = lambda w, blk: pl.BlockSpec((NB, 1, w), lambda b, c: (b, 0, blk))
    const = lambda shape: pl.BlockSpec(shape, lambda b, c: (0,) * len(shape))
    assert len(wts) == N_RWKV_SHARED
    in_specs = [wide(3), wide(4), wide(5), narrow(25), narrow(26),
                sh(B_WIDTH, 0), sh(B_WIDTH, 1), sh(B_WIDTH, 2), sh(LANES, 12), sh(LANES, 13),
                pl.BlockSpec((NB, RWKV_N, B_WIDTH), lambda b, c: (b, 0, 0))]
    in_specs += [const(w.shape) for w in wts]
    GW = GROUP * RWKV_N
    u = u.reshape(B, T, UW)
    y, h_fin = pl.pallas_call(
        functools.partial(_rwkv_kernel, C=C, NB=NB),
        grid=(B // NB, nc),
        in_specs=in_specs,
        out_specs=[
            pl.BlockSpec((NB, C, B_WIDTH), lambda b, c: (b, c, 0)),
            pl.BlockSpec((NB, RWKV_N, B_WIDTH), lambda b, c: (b, 0, 0)),
        ],
        out_shape=[
            jax.ShapeDtypeStruct((B, T, B_WIDTH), BF16),
            jax.ShapeDtypeStruct((B, RWKV_N, B_WIDTH), F32),
        ],
        scratch_shapes=[
            pltpu.VMEM((NB, 1, B_WIDTH), F32), pltpu.VMEM((NB, 1, B_WIDTH), F32), pltpu.VMEM((NB, 1, B_WIDTH), F32),
            pltpu.VMEM((NB, 1, LANES), F32), pltpu.VMEM((NB, 1, LANES), F32),
            pltpu.VMEM((NB, RWKV_HEADS // GROUP, GW, GW), F32),
        ],
        compiler_params=_params("parallel", "arbitrary"),
        name="rwkv7",
    )(u, u, u, u, u, shift0, shift0, shift0, shift0, shift0, h0, *wts)
    return y.reshape(B * T, B_WIDTH), h_fin


def _ret_kernel(q_ref, k_ref, v_ref, g_ref, r0_ref, gng_ref, gnb_ref,
                o_ref, rout_ref, r_scr, *, C):
    c = pl.program_id(1)

    @pl.when(c == 0)
    def _():
        r_scr[...] = r0_ref[0]

    ri = _iota((C, C), 0)
    ci = _iota((C, C), 1)
    causal = ri >= ci
    diff = (ri - ci).astype(F32)
    pos1 = (_iota((C, 1), 0) + 1).astype(F32)
    outs = [None] * RET_HEADS

    def head(h):
        lg = math.log1p(-(2.0 ** (-5.0 - h)))
        q = q_ref[:, h * RET_DK:(h + 1) * RET_DK]
        k = k_ref[:, h * RET_DK:(h + 1) * RET_DK]
        v = v_ref[:, h * RET_DV:(h + 1) * RET_DV]
        r = r_scr[h]
        scores = _dot(q, k, NT)
        inter = _dot(q, r)
        yield
        att = scores * jnp.where(causal, jnp.exp(diff * lg), 0.0)
        o = _dot(att, v) + inter * jnp.exp(pos1 * lg)
        r_scr[h] = math.exp(C * lg) * r + _dot(k.astype(F32) * jnp.exp((C - pos1) * lg), v, TN)
        yield
        mu = jnp.mean(o, axis=-1, keepdims=True)
        d = o - mu
        var = jnp.mean(d * d, axis=-1, keepdims=True)
        outs[h] = d * lax.rsqrt(var + RET_GN_EPS)

    for _ in _lockstep([head(h) for h in range(RET_HEADS)]):
        pass
    g = g_ref[...].astype(F32)
    o = jnp.concatenate(outs, axis=1) * gng_ref[...] + gnb_ref[...]
    o_ref[...] = ((g * jax.nn.sigmoid(g)) * o).astype(o_ref.dtype)
    rout_ref[0] = r_scr[...]


def _ret(u, r0, gn_g, gn_b, B, T, C):
    nc = T // C
    row = lambda b, c: b * nc + c
    return pl.pallas_call(
        functools.partial(_ret_kernel, C=C),
        grid=(B, nc),
        in_specs=[
            pl.BlockSpec((C, RET_QK), lambda b, c: (row(b, c), 0)),
            pl.BlockSpec((C, RET_QK), lambda b, c: (row(b, c), 1)),
            pl.BlockSpec((C, RET_V), lambda b, c: (row(b, c), 1)),
            pl.BlockSpec((C, RET_V), lambda b, c: (row(b, c), 2)),
            pl.BlockSpec((1, RET_HEADS, RET_DK, RET_DV), lambda b, c: (b, 0, 0, 0)),
            pl.BlockSpec((1, RET_V), lambda b, c: (0, 0)),
            pl.BlockSpec((1, RET_V), lambda b, c: (0, 0)),
        ],
        out_specs=[
            pl.BlockSpec((C, RET_V), lambda b, c: (row(b, c), 0)),
            pl.BlockSpec((1, RET_HEADS, RET_DK, RET_DV), lambda b, c: (b, 0, 0, 0)),
        ],
        out_shape=[
            jax.ShapeDtypeStruct((B * T, RET_V), BF16),
            jax.ShapeDtypeStruct((B, RET_HEADS, RET_DK, RET_DV), F32),
        ],
        scratch_shapes=[pltpu.VMEM((RET_HEADS, RET_DK, RET_DV), F32)],
        compiler_params=_params("parallel", "arbitrary"),
        name="retention",
    )(u, u, u, u, r0, gn_g, gn_b)


def _router_kernel(res_ref, a_ref, wo_ref, g_ref, wr_ref, x_ref, ids_ref, gsel_ref):
    x = res_ref[...] + jnp.dot(a_ref[...], wo_ref[...], preferred_element_type=F32)
    x_ref[...] = x
    h = _rms(x, g_ref[...])
    hi, lo = _split(h)
    parts = jnp.dot(jnp.concatenate([hi, lo], axis=1), wr_ref[...], preferred_element_type=F32)
    logits = parts + pltpu.roll(parts, LANES - N_EXPERTS, axis=1)
    lane = _iota(logits.shape, 1)
    valid = lane < N_EXPERTS
    logits = jnp.where(valid, logits, -jnp.inf)
    m = jnp.max(logits, axis=-1, keepdims=True)
    e = jnp.where(valid, jnp.exp(logits - m), 0.0)
    p = e / jnp.sum(e, axis=-1, keepdims=True)
    big = jnp.int32(LANES)
    p1 = jnp.max(p, axis=-1, keepdims=True)
    i1 = jnp.min(jnp.where(p == p1, lane, big), axis=-1, keepdims=True)
    rest = jnp.where((lane == i1) | ~valid, -1.0, p)
    p2 = jnp.max(rest, axis=-1, keepdims=True)
    i2 = jnp.min(jnp.where(rest == p2, lane, big), axis=-1, keepdims=True)
    tot = p1 + p2
    ids_ref[...] = jnp.where(lane == 0, i1, i2)
    gsel_ref[...] = jnp.where(lane == 0, p1 / tot, p2 / tot)


def _router_operand(wr):
    d, e = wr.shape
    whi = wr.astype(BF16)
    wlo = (wr - whi.astype(F32)).astype(BF16)
    top = jnp.concatenate([whi, wlo, jnp.zeros((d, LANES - 2 * e), BF16)], axis=1)
    bottom = jnp.concatenate([whi, jnp.zeros((d, LANES - e), BF16)], axis=1)
    return jnp.concatenate([top, bottom], axis=0)


def _router(res, a, wo, g, wr):
    n, d = res.shape
    tm = min(n, 512)
    row = lambda w: pl.BlockSpec((tm, w), lambda i: (i, 0))
    const = lambda s: pl.BlockSpec(s, lambda i: (0, 0))
    return pl.pallas_call(
        _router_kernel,
        grid=(n // tm,),
        in_specs=[row(d), row(a.shape[1]), const(wo.shape), const((1, d)), const(wr.shape)],
        out_specs=[row(d), row(LANES), row(LANES)],
        out_shape=[jax.ShapeDtypeStruct((n, d), F32), jax.ShapeDtypeStruct((n, LANES), jnp.int32),
                   jax.ShapeDtypeStruct((n, LANES), F32)],
        compiler_params=_params("parallel"),
        name="router",
    )(res, a, wo, g, wr)


def _dispatch_kernel(ends_ref, pos1_ref, pos2_ref, x_ref, xs_ref, zbuf, zsem, sem, *, TQ, TM):
    t = pl.program_id(0)

    @pl.when(t == 0)
    def _():
        zbuf[...] = jnp.zeros_like(zbuf)
        tails = [pltpu.make_async_copy(zbuf, xs_ref.at[pl.ds(pl.multiple_of(ends_ref[e] - TM, TM), TM)], zsem)
                 for e in range(N_EXPERTS)]
        for cp in tails:
            cp.start()
        for cp in tails:
            cp.wait()

        def zero_tile(i, carry):
            cp = pltpu.make_async_copy(zbuf, xs_ref.at[pl.ds(pl.multiple_of(i * TM, TM), TM)], zsem)
            cp.start()
            cp.wait()
            return carry

        lax.fori_loop(ends_ref[N_EXPERTS - 1] // TM, xs_ref.shape[0] // TM, zero_tile, 0)

    def row_copy(r, pos_ref):
        return pltpu.make_async_copy(x_ref.at[pl.ds(r, 1)], xs_ref.at[pl.ds(pos_ref[0, 0, r], 1)], sem)

    def issue(r, carry):
        row_copy(r, pos1_ref).start(priority=0)
        row_copy(r, pos2_ref).start(priority=1)
        return carry

    lax.fori_loop(0, TQ, issue, 0, unroll=8)

    def drain(r, carry):
        row_copy(r, pos1_ref).wait()
        row_copy(r, pos2_ref).wait()
        return carry

    lax.fori_loop(0, TQ, drain, 0, unroll=8)


def _dispatch(x, pos, ends, p, tm):
    n, d = x.shape
    tq = min(n, DISPATCH_ROWS)
    nt = n // tq
    pos3 = pos.reshape(2 * nt, 1, tq)
    grid_spec = pltpu.PrefetchScalarGridSpec(
        num_scalar_prefetch=1,
        grid=(nt,),
        in_specs=[pl.BlockSpec((1, 1, tq), lambda t, ends: (t, 0, 0), memory_space=pltpu.SMEM),
                  pl.BlockSpec((1, 1, tq), lambda t, ends: (t + nt, 0, 0), memory_space=pltpu.SMEM),
                  pl.BlockSpec((tq, d), lambda t, ends: (t, 0))],
        out_specs=pl.BlockSpec(memory_space=pl.ANY),
        scratch_shapes=[pltpu.VMEM((tm, d), F32), pltpu.SemaphoreType.DMA(()), pltpu.SemaphoreType.DMA(())],
    )
    return pl.pallas_call(
        functools.partial(_dispatch_kernel, TQ=tq, TM=tm),
        grid_spec=grid_spec,
        out_shape=jax.ShapeDtypeStruct((p, d), F32),
        compiler_params=_params("arbitrary"),
        name="dispatch",
    )(ends, pos3, pos3, x)


def _moe_ffn_kernel(te_ref, nu_ref, x_ref, g_ref, *refs, tf, residual, n_pre):
    pre_a = refs[:n_pre]
    pre_w = refs[n_pre:n_pre + min(n_pre, 1)]
    w1_ref, w3_ref, w2_ref, o_ref, acc = refs[n_pre + len(pre_w):]
    t = pl.program_id(0)

    @pl.when(t < nu_ref[0])
    def _():
        x = x_ref[...]
        if pre_a:
            a = jnp.concatenate([a_ref[...] for a_ref in pre_a], axis=1)
            x = x + jnp.dot(a, pre_w[0][...], preferred_element_type=F32)
        h = _rms(x, g_ref[...]).astype(BF16)
        if residual:
            acc[...] = x
        for i, f0 in enumerate(range(0, D_FF, tf)):
            a1 = jnp.dot(h, w1_ref[0, :, f0:f0 + tf], preferred_element_type=F32)
            a3 = jnp.dot(h, w3_ref[0, :, f0:f0 + tf], preferred_element_type=F32)
            act = (a1 * jax.nn.sigmoid(a1) * a3).astype(BF16)
            part = jnp.dot(act, w2_ref[0, f0:f0 + tf, :], preferred_element_type=F32)
            if i == 0 and not residual:
                acc[...] = part
            elif f0 + tf < D_FF:
                acc[...] += part
            else:
                o_ref[...] = acc[...] + part

    @pl.when(t >= nu_ref[0])
    def _():
        o_ref[...] = jnp.zeros_like(o_ref)


def _ffn(res, a_list, w_out, g, w1, w3, w2):
    tm = min(res.shape[0], MOE_TILE)
    nt = res.shape[0] // tm
    return _moe_ffn(res, g, jnp.zeros((nt,), jnp.int32), jnp.full((1,), nt, jnp.int32), w1, w3, w2, tm,
                    residual=True, pre=(a_list, [w_out]))


def _moe_ffn(xs, g, tile_expert, n_used, w1, w3, w2, tm, residual=False, pre=((), ())):
    p, d = xs.shape
    dff = w1.shape[2]
    row_tile = lambda t, te, nu: (jnp.minimum(t, nu[0] - 1), 0)
    pre_a, pre_w = pre
    grid_spec = pltpu.PrefetchScalarGridSpec(
        num_scalar_prefetch=2,
        grid=(p // tm,),
        in_specs=[
            pl.BlockSpec((tm, d), row_tile),
            pl.BlockSpec((1, d), lambda t, te, nu: (0, 0)),
            *[pl.BlockSpec((tm, a.shape[1]), row_tile) for a in pre_a],
            *[pl.BlockSpec(w.shape, lambda t, te, nu: (0, 0)) for w in pre_w],
            pl.BlockSpec((1, d, dff), lambda t, te, nu: (te[t], 0, 0)),
            pl.BlockSpec((1, d, dff), lambda t, te, nu: (te[t], 0, 0)),
            pl.BlockSpec((1, dff, d), lambda t, te, nu: (te[t], 0, 0)),
        ],
        out_specs=pl.BlockSpec((tm, d), lambda t, te, nu: (t, 0)),
        scratch_shapes=[pltpu.VMEM((tm, d), F32)],
    )
    return pl.pallas_call(
        functools.partial(_moe_ffn_kernel, tf=256, residual=residual, n_pre=len(pre_a)),
        grid_spec=grid_spec,
        out_shape=jax.ShapeDtypeStruct((p, d), F32),
        compiler_params=_params("arbitrary"),
        name="moe_ffn",
    )(tile_expert, n_used, xs, g, *pre_a, *pre_w, w1, w3, w2)


def _combine_kernel(pos1_ref, pos2_ref, pos1n_ref, pos2n_ref, x_ref, gsel_ref, gf_ref, ys_ref, o_ref,
                    buf, sems, *, TQ):
    t = pl.program_id(0)
    slot = t % 2

    def row_copy(r, pos_ref, s, k):
        return pltpu.make_async_copy(ys_ref.at[pl.ds(pos_ref[0, 0, r], 1)], buf.at[s, k, pl.ds(r, 1)], sems.at[s])

    def issue(p1_ref, p2_ref, s):
        def body(r, carry):
            row_copy(r, p1_ref, s, 0).start(priority=0)
            row_copy(r, p2_ref, s, 1).start(priority=1)
            return carry

        lax.fori_loop(0, TQ, body, 0, unroll=8)

    @pl.when(t == 0)
    def _():
        issue(pos1_ref, pos2_ref, 0)

    @pl.when(t + 1 < pl.num_programs(0))
    def _():
        issue(pos1n_ref, pos2n_ref, 1 - slot)

    def drain(r, carry):
        row_copy(r, pos1_ref, slot, 0).wait()
        row_copy(r, pos2_ref, slot, 1).wait()
        return carry

    lax.fori_loop(0, TQ, drain, 0, unroll=8)
    gs = gsel_ref[...]
    y = x_ref[...] + gs[:, 0:1] * buf[slot, 0] + gs[:, 1:2] * buf[slot, 1]
    o_ref[...] = _rms(y, gf_ref[...])


def _combine(x, ys, pos, gsel, g_final):
    n, d = x.shape
    tq = min(n, DISPATCH_ROWS)
    nt = n // tq
    pos3 = pos.reshape(2 * nt, 1, tq)
    smem = lambda f: pl.BlockSpec((1, 1, tq), f, memory_space=pltpu.SMEM)
    nxt = lambda t: jnp.minimum(t + 1, nt - 1)
    return pl.pallas_call(
        functools.partial(_combine_kernel, TQ=tq),
        grid=(nt,),
        in_specs=[smem(lambda t: (t, 0, 0)), smem(lambda t: (t + nt, 0, 0)),
                  smem(lambda t: (nxt(t), 0, 0)), smem(lambda t: (nxt(t) + nt, 0, 0)),
                  pl.BlockSpec((tq, d), lambda t: (t, 0)),
                  pl.BlockSpec((tq, LANES), lambda t: (t, 0)),
                  pl.BlockSpec((1, d), lambda t: (0, 0)),
                  pl.BlockSpec(memory_space=pl.ANY)],
        out_specs=pl.BlockSpec((tq, d), lambda t: (t, 0)),
        out_shape=jax.ShapeDtypeStruct((n, d), F32),
        scratch_shapes=[pltpu.VMEM((2, 2, tq, d), F32), pltpu.SemaphoreType.DMA((2,))],
        compiler_params=_params("arbitrary"),
        name="combine",
    )(pos3, pos3, pos3, pos3, x, gsel, g_final, ys)


def _moe_plan(ids, tm):
    n = ids.shape[0]
    e_flat = ids.T.reshape(-1)
    experts = jnp.arange(N_EXPERTS, dtype=jnp.int32)
    onehot = (e_flat[:, None] == experts[None, :]).astype(jnp.int32)
    csum = jnp.cumsum(onehot, axis=0)
    counts = csum[-1]
    padded = jnp.maximum((counts + tm - 1) // tm, 1) * tm
    ends = jnp.cumsum(padded).astype(jnp.int32)
    pos = jnp.sum((csum - 1 + (ends - padded)[None, :]) * onehot, axis=1).astype(jnp.int32)
    p = 2 * n + N_EXPERTS * tm
    n_used = ends[-1] // tm
    tile_start = jnp.minimum(jnp.arange(p // tm, dtype=jnp.int32) * tm, ends[-1] - 1)
    tile_expert = jnp.sum((tile_start[:, None] >= ends[None, :]).astype(jnp.int32), axis=1)
    return pos, ends, tile_expert, n_used.reshape(1), p


def _prep_weights(p):
    w_in = p["ab_w_in"][0]
    wa, wb = w_in[:, :A_COLS], w_in[:, A_COLS:]
    pad = jnp.zeros((D_MODEL, LANES - GLA_GATE_RANK), F32)
    w_in_p = jnp.concatenate(
        [wa[:, :2 * GLA_QK + A_WIDTH], wa[:, 2 * GLA_QK + A_WIDTH + GLA_GATE_RANK:], wb[:, :3 * B_WIDTH],
         wa[:, 2 * GLA_QK + A_WIDTH:2 * GLA_QK + A_WIDTH + GLA_GATE_RANK], pad, wb[:, 3 * B_WIDTH:]], axis=1)
    row = lambda v: v.reshape(1, -1).astype(F32)
    mu = p["rwkv_mu"][0]
    z64 = jnp.zeros((64, B_WIDTH), F32)
    w = dict(
        norm_mix=p["norm_mix"], norm_ffn=p["norm_ffn"], norm_final=row(p["norm_final"]),
        w_in=w_in_p.astype(BF16),
        w_out=p["ab_w_out"][0].astype(BF16),
        gla_w2=jnp.concatenate([p["gla_w_gate2"][0], jnp.zeros((LANES - GLA_GATE_RANK, GLA_QK), F32)], axis=0),
        gla_bg=row(p["gla_b_gate"][0]), gla_norm=row(p["gla_norm"][0]),
        rwkv=(row(mu[:B_WIDTH]), row(mu[B_WIDTH:2 * B_WIDTH]), row(mu[2 * B_WIDTH:3 * B_WIDTH]),
              row(mu[3 * B_WIDTH:3 * B_WIDTH + LANES]), row(mu[3 * B_WIDTH + LANES:]),
              row(p["rwkv_w0"][0]), jnp.concatenate([p["rwkv_w2"][0], z64], axis=0),
              row(p["rwkv_a0"][0]), jnp.concatenate([z64, p["rwkv_a2"][0]], axis=0),
              p["rwkv_g2"][0], row(p["rwkv_k_k"][0]), row(p["rwkv_k_a"][0]), row(p["rwkv_r_k"][0]),
              row(p["rwkv_gn_g"][0]), row(p["rwkv_gn_b"][0])),
        ret_w_in=p["ret_w_in"][0].astype(BF16), ret_w_out=p["ret_w_out"][0].astype(BF16),
        ret_gn_g=row(p["ret_gn_g"][0]), ret_gn_b=row(p["ret_gn_b"][0]),
        ffn_w1=p["ffn_w1"].astype(BF16), ffn_w3=p["ffn_w3"].astype(BF16), ffn_w2=p["ffn_w2"].astype(BF16),
        router=_router_operand(p["moe_router"][0]),
        moe_w1=p["moe_w1"][0].astype(BF16), moe_w3=p["moe_w3"][0].astype(BF16), moe_w2=p["moe_w2"][0].astype(BF16),
    )
    return w


def _trunk(x, pos0, st_gla, st_rwkv, st_shift, st_ret, w):
    B, T, D = x.shape
    n = B * T
    x0 = x.reshape(n, D)
    c_ab = min(CHUNK, T)
    c_ret = min(256, T)

    u = _norm_matmul(x0, w["norm_mix"][0:1], w["w_in"], tm=1024, tn=UW, out_dtype=BF16)
    oa, s_gla = _gla(u, st_gla, w["gla_w2"], w["gla_bg"], w["gla_norm"], B, T, c_ab)
    h0 = st_rwkv.transpose(0, 3, 1, 2).reshape(B, RWKV_N, B_WIDTH)
    yb, h_fin = _rwkv(u, st_shift, h0, w["rwkv"], B, T, c_ab)
    s_rwkv = h_fin.reshape(B, RWKV_N, RWKV_HEADS, RWKV_N).transpose(0, 2, 3, 1)
    u_last = u.reshape(B, T, UW)[:, T - 1:, :]
    s_shift = jnp.concatenate([u_last[..., 3 * A_WIDTH:6 * A_WIDTH], u_last[..., UW - 2 * LANES:]], axis=-1).astype(F32)
    x2 = _ffn(x0, [oa, yb], w["w_out"], w["norm_ffn"][0:1], w["ffn_w1"], w["ffn_w3"], w["ffn_w2"])

    inv = 1.0 / (ROPE_BASE ** (jnp.arange(0, RET_DK, 2, dtype=F32) / RET_DK))
    ang = (pos0 + jnp.arange(T)).astype(F32)[:, None] * inv[None, :]
    ur = _norm_matmul(x2, w["norm_mix"][1:2], w["ret_w_in"], tm=1024, tn=2 * RET_QK, out_dtype=BF16,
                      rope=(jnp.cos(ang), jnp.sin(ang), 1.0, RET_DK ** -0.5))
    o, s_ret = _ret(ur, st_ret, w["ret_gn_g"], w["ret_gn_b"], B, T, c_ret)
    x3, ids, gsel = _router(x2, o, w["ret_w_out"], w["norm_ffn"][1:2], w["router"])
    tm = min(MOE_TILE, n // 4)
    pos, ends, tile_expert, n_used, p = _moe_plan(ids[:, :2], tm)
    xs = _dispatch(x3, pos, ends, p, tm)
    ys = _moe_ffn(xs, w["norm_ffn"][1:2], tile_expert, n_used, w["moe_w1"], w["moe_w3"], w["moe_w2"], tm)
    y = _combine(x3, ys, pos, gsel, w["norm_final"])
    return y.reshape(B, T, D), s_gla[None], s_rwkv[None], s_shift[None], s_ret[None]


def kernel(x_prompt, x_sample, state_gla, state_rwkv, state_shift, state_ret, norm_mix, norm_ffn, norm_final, ab_w_in, ab_w_out, gla_w_gate2, gla_b_gate, gla_norm, rwkv_mu, rwkv_w0, rwkv_w2, rwkv_a0, rwkv_a2, rwkv_g2, rwkv_k_k, rwkv_k_a, rwkv_r_k, rwkv_gn_g, rwkv_gn_b, ret_w_in, ret_gn_g, ret_gn_b, ret_w_out, ffn_w1, ffn_w3, ffn_w2, moe_router, moe_w1, moe_w3, moe_w2):
    p = dict(norm_mix=norm_mix, norm_ffn=norm_ffn, norm_final=norm_final, ab_w_in=ab_w_in, ab_w_out=ab_w_out,
             gla_w_gate2=gla_w_gate2, gla_b_gate=gla_b_gate, gla_norm=gla_norm, rwkv_mu=rwkv_mu, rwkv_w0=rwkv_w0,
             rwkv_w2=rwkv_w2, rwkv_a0=rwkv_a0, rwkv_a2=rwkv_a2, rwkv_g2=rwkv_g2, rwkv_k_k=rwkv_k_k,
             rwkv_k_a=rwkv_k_a, rwkv_r_k=rwkv_r_k, rwkv_gn_g=rwkv_gn_g, rwkv_gn_b=rwkv_gn_b, ret_w_in=ret_w_in,
             ret_gn_g=ret_gn_g, ret_gn_b=ret_gn_b, ret_w_out=ret_w_out, ffn_w1=ffn_w1, ffn_w3=ffn_w3, ffn_w2=ffn_w2,
             moe_router=moe_router, moe_w1=moe_w1, moe_w3=moe_w3, moe_w2=moe_w2)
    w = _prep_weights(p)
    bp, tp = x_prompt.shape[0], x_prompt.shape[1]
    dt = x_prompt.dtype
    z_gla = jnp.zeros((bp, GLA_HEADS, GLA_DK, GLA_DV), dt)
    z_rwkv = jnp.zeros((bp, RWKV_HEADS, RWKV_N, RWKV_N), dt)
    z_shift = jnp.zeros((bp, 1, B_COLS), dt)
    z_ret = jnp.zeros((bp, RET_HEADS, RET_DK, RET_DV), dt)
    yp = _trunk(x_prompt, 0, z_gla, z_rwkv, z_shift, z_ret, w)
    ys = _trunk(x_sample, PAST_LEN, state_gla[0], state_rwkv[0], state_shift[0], state_ret[0], w)
    return (yp[0], ys[0], yp[1], yp[2], yp[3], yp[4], ys[1], ys[2], ys[3], ys[4])
```

```python
import functools
import math

import jax
import jax.numpy as jnp
from jax import lax
from jax.experimental import pallas as pl
from jax.experimental.pallas import tpu as pltpu

F32 = jnp.float32
BF16 = jnp.bfloat16

D_MODEL = 1024
EPS = 1e-6
GLA_HEADS, GLA_DK, GLA_DV = 4, 64, 128
GLA_QK = GLA_HEADS * GLA_DK
A_WIDTH = GLA_HEADS * GLA_DV
GLA_GATE_RANK = 16
GLA_TAU = 16.0
RWKV_HEADS, RWKV_N = 8, 64
B_WIDTH = RWKV_HEADS * RWKV_N
RWKV_GN_EPS = 64e-5
B_COLS = 3 * B_WIDTH + 64 + 64 + 128
A_COLS = 2 * GLA_QK + A_WIDTH + GLA_GATE_RANK + A_WIDTH
RET_HEADS = 4
RET_DK = D_MODEL // RET_HEADS
RET_DV = 2 * D_MODEL // RET_HEADS
RET_QK = RET_HEADS * RET_DK
RET_V = RET_HEADS * RET_DV
RET_GN_EPS = 1e-5
ROPE_BASE = 10000.0
D_FF = 2816
N_EXPERTS = 8
CHUNK = 64
SUB = 16
LANES = 128
GROUP = 4
MIXER_BATCH = 8
MOE_TILE = 512
DISPATCH_ROWS = 512
ROUTER_PARTS = 4
UW = 3456
VMEM_LIMIT = 56 * 1024 * 1024

NN = (((1,), (0,)), ((), ()))
NT = (((1,), (1,)), ((), ()))
TN = (((0,), (0,)), ((), ()))


def _dot(a, b, dims=NN):
    return lax.dot_general(a.astype(BF16), b.astype(BF16), dims, preferred_element_type=F32)


def _split(x):
    hi = x.astype(BF16)
    lo = (x - hi.astype(F32)).astype(BF16)
    return hi, lo


def _dot3(a, b, dims=NN):
    ah, al = _split(a)
    bh, bl = _split(b)
    dg = lambda x, y: lax.dot_general(x, y, dims, preferred_element_type=F32)
    return dg(ah, bh) + dg(ah, bl) + dg(al, bh)


def _dot_exact_lhs(m, x, dims=NN):
    xh, xl = _split(x)
    mb = m.astype(BF16)
    dg = lambda y: lax.dot_general(mb, y, dims, preferred_element_type=F32)
    return dg(xh) + dg(xl)


def _dot_exact_rhs(x, m, dims=NN):
    xh, xl = _split(x)
    mb = m.astype(BF16)
    dg = lambda y: lax.dot_general(y, mb, dims, preferred_element_type=F32)
    return dg(xh) + dg(xl)


def _iota(shape, axis):
    return lax.broadcasted_iota(jnp.int32, shape, axis)


def _softplus(z):
    return jnp.maximum(z, 0.0) + jnp.log1p(jnp.exp(-jnp.abs(z)))


def _rms(x, g):
    return x * lax.rsqrt(jnp.mean(x * x, axis=-1, keepdims=True) + EPS) * g


def _lockstep(chains):
    chains = list(chains)
    while chains:
        alive = []
        for ch in chains:
            try:
                next(ch)
                alive.append(ch)
            except StopIteration:
                pass
        chains = alive
        yield


def _params(*sem):
    return pltpu.CompilerParams(dimension_semantics=sem, vmem_limit_bytes=VMEM_LIMIT)


def _norm_matmul_kernel(x_ref, g_ref, w_ref, o_ref, h_ref):
    @pl.when(pl.program_id(1) == 0)
    def _():
        h_ref[...] = _rms(x_ref[...], g_ref[...]).astype(BF16)

    o_ref[...] = jnp.dot(h_ref[...], w_ref[...], preferred_element_type=F32).astype(o_ref.dtype)


def _norm_matmul(x, g, w, tm, tn, out_dtype):
    n, d = x.shape
    nout = w.shape[1]
    tm = min(n, tm)
    return pl.pallas_call(
        _norm_matmul_kernel,
        grid=(n // tm, nout // tn),
        in_specs=[
            pl.BlockSpec((tm, d), lambda i, j: (i, 0)),
            pl.BlockSpec((1, d), lambda i, j: (0, 0)),
            pl.BlockSpec((d, tn), lambda i, j: (0, j)),
        ],
        out_specs=pl.BlockSpec((tm, tn), lambda i, j: (i, j)),
        out_shape=jax.ShapeDtypeStruct((n, nout), out_dtype),
        scratch_shapes=[pltpu.VMEM((tm, d), BF16)],
        compiler_params=_params("parallel", "arbitrary"),
        name="norm_matmul",
    )(x, g, w)


def _gla_kernel(q_ref, k_ref, v_ref, ga_ref, og_ref, s0_ref, w2_ref, bg_ref, gn_ref,
                o_ref, sout_ref, s_scr, *, C, NB):
    keep_s = (_iota((GLA_QK, A_WIDTH), 0) >> 6) == (_iota((GLA_QK, A_WIDTH), 1) >> 7)

    @pl.when(pl.program_id(1) == 0)
    def _():
        for bb in range(NB):
            s_scr[bb] = jnp.where(keep_s, jnp.concatenate([s0_ref[bb]] * GLA_HEADS, axis=1), 0.0)

    chains = [_gla_one(q_ref.at[bb], k_ref.at[bb], v_ref.at[bb], ga_ref.at[bb], og_ref.at[bb],
                       w2_ref, bg_ref, gn_ref, o_ref.at[bb], sout_ref.at[bb], s_scr.at[bb], keep_s, C=C)
              for bb in range(NB)]
    for _ in _lockstep(chains):
        pass


def _gla_one(q_ref, k_ref, v_ref, ga_ref, og_ref, w2_ref, bg_ref, gn_ref,
             o_ref, sout_ref, s_scr, keep_s, *, C):
    x = _dot3(ga_ref[...].astype(F32), w2_ref[...]) + bg_ref[...]
    yield
    logg = (jnp.minimum(x, 0.0) - jnp.log1p(jnp.exp(-jnp.abs(x)))) * (1.0 / GLA_TAU)
    tril = _iota((C, C), 0) >= _iota((C, C), 1)
    b = _dot_exact_lhs(tril, logg)
    g_col = _dot_exact_rhs(logg, jnp.ones((C, LANES), F32), TN)
    yield
    q = q_ref[...].astype(F32) * (GLA_DK ** -0.5)
    k = k_ref[...].astype(F32)
    v = v_ref[...].astype(F32)
    s = s_scr[...]
    o_inter = _dot(q * jnp.exp(b), s)
    bl = b[C - 1:C]
    upd = _dot(k * jnp.exp(bl - b), v, TN)

    head_k = _iota((1, GLA_QK), 1) >> 6
    head_v = _iota((1, A_WIDTH), 1) >> 7
    ones_kv = keep_s.astype(BF16)
    ridx = _iota((SUB, 1), 0)

    def bd(x, head_of_lane):
        xb = x.astype(BF16)
        zero = jnp.zeros_like(xb)
        return jnp.concatenate([jnp.where(head_of_lane == h, xb, zero) for h in range(GLA_HEADS)], axis=0)

    starts = range(0, C, SUB)
    atts = {}
    for i0 in starts[1:]:
        m = b[i0 - 1:i0]
        qi, bi = q[i0:i0 + SUB], b[i0:i0 + SUB]
        atts[i0] = _dot(qi * jnp.exp(bi - m), bd(k[:i0] * jnp.exp(m - b[:i0]), head_k), NT)
    yield
    scs = {}
    for i0 in starts:
        qi, bi, ki = q[i0:i0 + SUB], b[i0:i0 + SUB], k[i0:i0 + SUB]
        cols = []
        for j in range(SUB):
            e = jnp.exp(jnp.minimum(bi - bi[j:j + 1], 0.0))
            cols.append(jnp.where(ridx >= j, qi * e * ki[j:j + 1], 0.0))
        scs[i0] = _dot(jnp.concatenate(cols, axis=0), ones_kv)
    offs = {i0: _dot(atts[i0], bd(v[:i0], head_v)) for i0 in starts[1:]}
    yield
    blocks = []
    for i0 in starts:
        vi = v[i0:i0 + SUB]
        oi = o_inter[i0:i0 + SUB]
        if i0 > 0:
            oi = oi + offs[i0]
        for j in range(SUB):
            oi = oi + scs[i0][j * SUB:(j + 1) * SUB] * vi[j:j + 1]
        blocks.append(oi)
    o = jnp.concatenate(blocks, axis=0)

    e_col = jnp.concatenate([jnp.exp(g_col)] * (A_WIDTH // LANES), axis=1)
    s_new = e_col * s + jnp.where(keep_s, upd, 0.0)
    s_scr[...] = s_new
    sout_ref[...] = (s_new[:, 0:GLA_DV] + s_new[:, GLA_DV:2 * GLA_DV]
                     + s_new[:, 2 * GLA_DV:3 * GLA_DV] + s_new[:, 3 * GLA_DV:4 * GLA_DV])

    outs = [_rms(o[:, h * GLA_DV:(h + 1) * GLA_DV], gn_ref[...]) for h in range(GLA_HEADS)]
    og = og_ref[...].astype(F32)
    o_ref[...] = (jnp.concatenate(outs, axis=1) * (og * jax.nn.sigmoid(og))).astype(o_ref.dtype)


def _gla(u, s0, w2p, bg, gn, B, T, C):
    nc = T // C
    NB = MIXER_BATCH
    s0 = s0.reshape(B, GLA_QK, GLA_DV)
    u = u.reshape(B, T, UW)
    o, s_fin = pl.pallas_call(
        functools.partial(_gla_kernel, C=C, NB=NB),
        grid=(B // NB, nc),
        in_specs=[
            pl.BlockSpec((NB, C, GLA_QK), lambda b, c: (b, c, 0)),
            pl.BlockSpec((NB, C, GLA_QK), lambda b, c: (b, c, 1)),
            pl.BlockSpec((NB, C, A_WIDTH), lambda b, c: (b, c, 1)),
            pl.BlockSpec((NB, C, LANES), lambda b, c: (b, c, 24)),
            pl.BlockSpec((NB, C, A_WIDTH), lambda b, c: (b, c, 2)),
            pl.BlockSpec((NB, GLA_QK, GLA_DV), lambda b, c: (b, 0, 0)),
            pl.BlockSpec((LANES, GLA_QK), lambda b, c: (0, 0)),
            pl.BlockSpec((1, GLA_QK), lambda b, c: (0, 0)),
            pl.BlockSpec((1, GLA_DV), lambda b, c: (0, 0)),
        ],
        out_specs=[
            pl.BlockSpec((NB, C, A_WIDTH), lambda b, c: (b, c, 0)),
            pl.BlockSpec((NB, GLA_QK, GLA_DV), lambda b, c: (b, 0, 0)),
        ],
        out_shape=[
            jax.ShapeDtypeStruct((B, T, A_WIDTH), BF16),
            jax.ShapeDtypeStruct((B, GLA_QK, GLA_DV), F32),
        ],
        scratch_shapes=[pltpu.VMEM((NB, GLA_QK, A_WIDTH), F32)],
        compiler_params=_params("parallel", "arbitrary"),
        name="gla",
    )(u, u, u, u, u, s0, w2p, bg, gn)
    return o.reshape(B * T, A_WIDTH), s_fin.reshape(B, GLA_HEADS, GLA_DK, GLA_DV)


N_RWKV_PER_BATCH = (11, 2, 6)
N_RWKV_SHARED = 15


def _rwkv_kernel(*refs, C, NB):
    n_in, n_out, n_scr = N_RWKV_PER_BATCH
    ins, shared = refs[:n_in], refs[n_in:n_in + N_RWKV_SHARED]
    outs = refs[n_in + N_RWKV_SHARED:n_in + N_RWKV_SHARED + n_out]
    scr = refs[n_in + N_RWKV_SHARED + n_out:]
    GW = GROUP * RWKV_N
    keep_h = (_iota((GW, GW), 0) >> 6) == (_iota((GW, GW), 1) >> 6)

    @pl.when(pl.program_id(1) == 0)
    def _():
        h0_ref, h_scr = ins[10], scr[5]
        for bb in range(NB):
            for prev, sh in zip(scr[:5], ins[5:10]):
                prev[bb] = sh[bb]
            h0 = h0_ref[bb]
            for g in range(RWKV_HEADS // GROUP):
                t = jnp.concatenate([h0[:, g * GW:(g + 1) * GW]] * GROUP, axis=0)
                h_scr[bb, g] = jnp.where(keep_h, t, 0.0)

    at = lambda rs, bb: [r.at[bb] for r in rs]
    chains = [_rwkv_one(*at(ins[:5], bb), *shared, *at(outs, bb), *at(scr, bb), keep_h, C=C) for bb in range(NB)]
    for _ in _lockstep(chains):
        pass


def _rwkv_one(r_ref, k_ref, v_ref, wa_ref, gl_ref,
              mu_r, mu_k, mu_v, mu_wa, mu_gl, w0_ref, w2_ref, a0_ref, a2_ref, g2_ref,
              kk_ref, ka_ref, rk_ref, gng_ref, gnb_ref,
              y_ref, hout_ref,
              pr, pk, pv, pwa, pgl, h_scr, keep_h, *, C):
    GW = GROUP * RWKV_N
    CH = GROUP * C
    logc = C.bit_length() - 1
    n_groups = RWKV_HEADS // GROUP

    def head_mask(rows, cols, row_shift, col_of_lane):
        keep = (_iota((rows, cols), 0) >> row_shift) == col_of_lane(_iota((rows, cols), 1))
        return keep.astype(F32).astype(BF16)

    keep_rows = head_mask(CH, GW, logc, lambda l: l >> 6)
    keep_sq = head_mask(CH, CH, logc, lambda l: l >> logc)

    def bd(x, keep):
        return jnp.concatenate([x.astype(BF16)] * GROUP, axis=0) * keep

    rid = _iota((C, 1), 0)

    def shift(x_ref, prev, mu):
        x = x_ref[...].astype(F32)
        xp = jnp.where(rid == 0, prev[...], pltpu.roll(x, 1, axis=0))
        prev[...] = x[C - 1:C]
        return x + mu[...] * (xp - x)

    rs = shift(r_ref, pr, mu_r)
    ks = shift(k_ref, pk, mu_k)
    vs = shift(v_ref, pv, mu_v)
    was = shift(wa_ref, pwa, mu_wa)
    gls = shift(gl_ref, pgl, mu_gl)

    dec = -_softplus(-(w0_ref[...] + _dot3(jnp.tanh(was), w2_ref[...]))) - 0.5
    logw = -jnp.exp(dec)
    a = jax.nn.sigmoid(a0_ref[...] + _dot(was, a2_ref[...]))
    gate = _dot(jax.nn.sigmoid(gls), g2_ref[...])
    yield
    ones_h = keep_h.astype(F32).astype(BF16)

    def head_sum(x):
        return jnp.concatenate([_dot(x[:, g * GW:(g + 1) * GW], ones_h) for g in range(n_groups)], axis=1)

    kkr = ks * kk_ref[...]
    kk = kkr / jnp.maximum(jnp.sqrt(head_sum(kkr * kkr)), 1e-12)
    km = ks * (1.0 + (a - 1.0) * ka_ref[...])
    beta = kk * a
    tril = _iota((C, C), 0) >= _iota((C, C), 1)
    gc = _dot_exact_lhs(tril, logw)
    g_end = gc[C - 1:C]
    kap = kk * jnp.exp(gc - logw)
    rt = rs * jnp.exp(gc)
    e_neg = jnp.exp(-gc)
    b_inv = beta * e_neg
    k_inv = km * e_neg
    e_end = jnp.exp(g_end - gc)
    k_hat = km * e_end
    b_hat = beta * e_end

    col_s = _iota((C, CH), 1) & (C - 1)
    row_t = _iota((C, CH), 0)
    strict = row_t > col_s
    incl = row_t >= col_s
    same_sub = (row_t >> 4) == (col_s >> 4)
    eye_cat = (row_t == col_s).astype(F32)
    ones_c = jnp.ones((C, LANES), F32)

    yield
    ys = [None] * n_groups

    def group(g):
        L = slice(g * GW, (g + 1) * GW)
        p = jnp.concatenate([kap[:, L], rt[:, L]], axis=0)
        zb = _dot(p, bd(b_inv[:, L], keep_rows), NT)
        zk = _dot(p, bd(k_inv[:, L], keep_rows), NT)
        yield
        a_b = jnp.where(strict, zb[:C], 0.0)
        b_b = jnp.where(incl, zb[C:], 0.0)
        a_k = jnp.where(strict, zk[:C], 0.0)
        b_k = jnp.where(incl, zk[C:], 0.0)
        bd_v = bd(vs[:, L], keep_rows)
        akv = _dot(a_k, bd_v)

        d_blk = jnp.where(same_sub, a_b, 0.0)
        x1 = -d_blk
        p1 = eye_cat + x1
        x2 = _dot(x1, bd(x1, keep_sq))
        hbd = h_scr[g]
        ph = _dot(p, hbd)
        yield
        rhs = ph[:C] + akv
        r2 = _dot(jnp.concatenate([x2, p1], axis=0), bd(x2, keep_sq))
        yield
        x4, p2 = r2[:C], p1 + r2[C:]
        r4 = _dot(jnp.concatenate([x4, p2], axis=0), bd(x4, keep_sq))
        yield
        x8, p3 = r4[:C], p2 + r4[C:]
        t_d = p3 + _dot(p3, bd(x8, keep_sq))
        yield

        n1 = _dot(t_d, bd(a_b - d_blk, keep_sq))
        s1 = _dot(t_d, bd(rhs, keep_rows))
        yield
        n2 = _dot(n1, bd(n1, keep_sq))
        yield
        s2 = s1 + _dot(n2, bd(s1, keep_rows))
        yield
        u = s2 - _dot(n1, bd(s2, keep_rows))
        yield
        y = ph[C:] + _dot(b_k, bd_v) - _dot(b_b, bd(u, keep_rows))
        m = _dot(k_hat[:, L], vs[:, L], TN) - _dot(b_hat[:, L], u, TN)
        g_col = _dot_exact_rhs(logw[:, L], ones_c, TN)[:, :1]
        yield
        h_new = jnp.exp(g_col) * hbd + jnp.where(keep_h, m, 0.0)
        h_scr[g] = h_new
        hout_ref[:, L] = (h_new[0:RWKV_N] + h_new[RWKV_N:2 * RWKV_N]
                          + h_new[2 * RWKV_N:3 * RWKV_N] + h_new[3 * RWKV_N:4 * RWKV_N])
        ys[g] = y

    yield from _lockstep([group(g) for g in range(n_groups)])
    y = jnp.concatenate(ys, axis=1)

    inv_n = 1.0 / RWKV_N
    mu = head_sum(y) * inv_n
    d = y - mu
    var = head_sum(d * d) * inv_n
    yn = d * lax.rsqrt(var + RWKV_GN_EPS) * gng_ref[...] + gnb_ref[...]
    bonus = head_sum(rs * km * rk_ref[...]) * vs
    y_ref[...] = ((yn + bonus) * gate).astype(y_ref.dtype)


def _rwkv(u, shift0, h0, wts, B, T, C):
    nc = T // C
    NB = MIXER_BATCH
    wide = lambda blk: pl.BlockSpec((NB, C, B_WIDTH), lambda b, c: (b, c, blk))
    narrow = lambda blk: pl.BlockSpec((NB, C, LANES), lambda b, c: (b, c, blk))
    sh = lambda w, blk: pl.BlockSpec((NB, 1, w), lambda b, c: (b, 0, blk))
    const = lambda shape: pl.BlockSpec(shape, lambda b, c: (0,) * len(shape))
    assert len(wts) == N_RWKV_SHARED
    in_specs = [wide(3), wide(4), wide(5), narrow(25), narrow(26),
                sh(B_WIDTH, 0), sh(B_WIDTH, 1), sh(B_WIDTH, 2), sh(LANES, 12), sh(LANES, 13),
                pl.BlockSpec((NB, RWKV_N, B_WIDTH), lambda b, c: (b, 0, 0))]
    in_specs += [const(w.shape) for w in wts]
    GW = GROUP * RWKV_N
    u = u.reshape(B, T, UW)
    y, h_fin = pl.pallas_call(
        functools.partial(_rwkv_kernel, C=C, NB=NB),
        grid=(B // NB, nc),
        in_specs=in_specs,
        out_specs=[
            pl.BlockSpec((NB, C, B_WIDTH), lambda b, c: (b, c, 0)),
            pl.BlockSpec((NB, RWKV_N, B_WIDTH), lambda b, c: (b, 0, 0)),
        ],
        out_shape=[
            jax.ShapeDtypeStruct((B, T, B_WIDTH), BF16),
            jax.ShapeDtypeStruct((B, RWKV_N, B_WIDTH), F32),
        ],
        scratch_shapes=[
            pltpu.VMEM((NB, 1, B_WIDTH), F32), pltpu.VMEM((NB, 1, B_WIDTH), F32), pltpu.VMEM((NB, 1, B_WIDTH), F32),
            pltpu.VMEM((NB, 1, LANES), F32), pltpu.VMEM((NB, 1, LANES), F32),
            pltpu.VMEM((NB, RWKV_HEADS // GROUP, GW, GW), F32),
        ],
        compiler_params=_params("parallel", "arbitrary"),
        name="rwkv7",
    )(u, u, u, u, u, shift0, shift0, shift0, shift0, shift0, h0, *wts)
    return y.reshape(B * T, B_WIDTH), h_fin


def _ret_kernel(q_ref, k_ref, v_ref, g_ref, cos_ref, sin_ref, r0_ref, gng_ref, gnb_ref,
                o_ref, rout_ref, r_scr, *, C):
    c = pl.program_id(1)

    @pl.when(c == 0)
    def _():
        r_scr[...] = r0_ref[0]

    cos = cos_ref[...]
    sin = sin_ref[...]
    half = RET_DK // 2

    def rope(x):
        x1, x2 = x[:, :half], x[:, half:]
        return jnp.concatenate([x1 * cos - x2 * sin, x1 * sin + x2 * cos], axis=1)

    ri = _iota((C, C), 0)
    ci = _iota((C, C), 1)
    causal = ri >= ci
    diff = (ri - ci).astype(F32)
    pos1 = (_iota((C, 1), 0) + 1).astype(F32)
    outs = [None] * RET_HEADS

    def head(h):
        lg = math.log1p(-(2.0 ** (-5.0 - h)))
        q = rope(q_ref[:, h * RET_DK:(h + 1) * RET_DK].astype(F32))
        k = rope(k_ref[:, h * RET_DK:(h + 1) * RET_DK].astype(F32)) * (RET_DK ** -0.5)
        v = v_ref[:, h * RET_DV:(h + 1) * RET_DV]
        r = r_scr[h]
        scores = _dot(q, k, NT)
        inter = _dot(q, r)
        yield
        att = scores * jnp.where(causal, jnp.exp(diff * lg), 0.0)
        o = _dot(att, v) + inter * jnp.exp(pos1 * lg)
        r_scr[h] = math.exp(C * lg) * r + _dot(k * jnp.exp((C - pos1) * lg), v, TN)
        yield
        mu = jnp.mean(o, axis=-1, keepdims=True)
        d = o - mu
        var = jnp.mean(d * d, axis=-1, keepdims=True)
        outs[h] = d * lax.rsqrt(var + RET_GN_EPS)

    for _ in _lockstep([head(h) for h in range(RET_HEADS)]):
        pass
    g = g_ref[...].astype(F32)
    o = jnp.concatenate(outs, axis=1) * gng_ref[...] + gnb_ref[...]
    o_ref[...] = ((g * jax.nn.sigmoid(g)) * o).astype(o_ref.dtype)
    rout_ref[0] = r_scr[...]


def _ret(u, cos, sin, r0, gn_g, gn_b, B, T, C):
    nc = T // C
    row = lambda b, c: b * nc + c
    return pl.pallas_call(
        functools.partial(_ret_kernel, C=C),
        grid=(B, nc),
        in_specs=[
            pl.BlockSpec((C, RET_QK), lambda b, c: (row(b, c), 0)),
            pl.BlockSpec((C, RET_QK), lambda b, c: (row(b, c), 1)),
            pl.BlockSpec((C, RET_V), lambda b, c: (row(b, c), 1)),
            pl.BlockSpec((C, RET_V), lambda b, c: (row(b, c), 2)),
            pl.BlockSpec((C, RET_DK // 2), lambda b, c: (c, 0)),
            pl.BlockSpec((C, RET_DK // 2), lambda b, c: (c, 0)),
            pl.BlockSpec((1, RET_HEADS, RET_DK, RET_DV), lambda b, c: (b, 0, 0, 0)),
            pl.BlockSpec((1, RET_V), lambda b, c: (0, 0)),
            pl.BlockSpec((1, RET_V), lambda b, c: (0, 0)),
        ],
        out_specs=[
            pl.BlockSpec((C, RET_V), lambda b, c: (row(b, c), 0)),
            pl.BlockSpec((1, RET_HEADS, RET_DK, RET_DV), lambda b, c: (b, 0, 0, 0)),
        ],
        out_shape=[
            jax.ShapeDtypeStruct((B * T, RET_V), BF16),
            jax.ShapeDtypeStruct((B, RET_HEADS, RET_DK, RET_DV), F32),
        ],
        scratch_shapes=[pltpu.VMEM((RET_HEADS, RET_DK, RET_DV), F32)],
        compiler_params=_params("parallel", "arbitrary"),
        name="retention",
    )(u, u, u, u, cos, sin, r0, gn_g, gn_b)


def _router_kernel(res_ref, a_ref, wo_ref, g_ref, wr_ref, x_ref, ids_ref, gsel_ref):
    tm = res_ref.shape[0]
    rows_per_part = tm // ROUTER_PARTS

    def part(r0):
        rows = pl.ds(r0, rows_per_part)
        x = res_ref[rows, :] + jnp.dot(a_ref[rows, :], wo_ref[...], preferred_element_type=F32)
        yield
        x_ref[rows, :] = x
        h = _rms(x, g_ref[...])
        hi, lo = _split(h)
        parts = jnp.dot(jnp.concatenate([hi, lo], axis=1), wr_ref[...], preferred_element_type=F32)
        yield
        logits = parts + pltpu.roll(parts, LANES - N_EXPERTS, axis=1)
        lane = _iota(logits.shape, 1)
        valid = lane < N_EXPERTS
        logits = jnp.where(valid, logits, -jnp.inf)
        m = jnp.max(logits, axis=-1, keepdims=True)
        e = jnp.where(valid, jnp.exp(logits - m), 0.0)
        p = e / jnp.sum(e, axis=-1, keepdims=True)
        big = jnp.int32(LANES)
        p1 = jnp.max(p, axis=-1, keepdims=True)
        i1 = jnp.min(jnp.where(p == p1, lane, big), axis=-1, keepdims=True)
        rest = jnp.where((lane == i1) | ~valid, -1.0, p)
        p2 = jnp.max(rest, axis=-1, keepdims=True)
        i2 = jnp.min(jnp.where(rest == p2, lane, big), axis=-1, keepdims=True)
        tot = p1 + p2
        ids_ref[rows, :] = jnp.where(lane == 0, i1, i2)
        gsel_ref[rows, :] = jnp.where(lane == 0, p1 / tot, p2 / tot)

    for _ in _lockstep([part(i * rows_per_part) for i in range(ROUTER_PARTS)]):
        pass


def _router_operand(wr):
    d, e = wr.shape
    whi = wr.astype(BF16)
    wlo = (wr - whi.astype(F32)).astype(BF16)
    top = jnp.concatenate([whi, wlo, jnp.zeros((d, LANES - 2 * e), BF16)], axis=1)
    bottom = jnp.concatenate([whi, jnp.zeros((d, LANES - e), BF16)], axis=1)
    return jnp.concatenate([top, bottom], axis=0)


def _router(res, a, wo, g, wr):
    n, d = res.shape
    tm = min(n, 1024)
    row = lambda w: pl.BlockSpec((tm, w), lambda i: (i, 0))
    const = lambda s: pl.BlockSpec(s, lambda i: (0, 0))
    return pl.pallas_call(
        _router_kernel,
        grid=(n // tm,),
        in_specs=[row(d), row(a.shape[1]), const(wo.shape), const((1, d)), const(wr.shape)],
        out_specs=[row(d), row(LANES), row(LANES)],
        out_shape=[jax.ShapeDtypeStruct((n, d), F32), jax.ShapeDtypeStruct((n, LANES), jnp.int32),
                   jax.ShapeDtypeStruct((n, LANES), F32)],
        compiler_params=_params("parallel"),
        name="router",
    )(res, a, wo, g, wr)


def _dispatch_kernel(ends_ref, pos1_ref, pos2_ref, x_ref, xs_ref, zbuf, zsem, sem, *, TQ, TM):
    t = pl.program_id(0)

    @pl.when(t == 0)
    def _():
        zbuf[...] = jnp.zeros_like(zbuf)
        tails = [pltpu.make_async_copy(zbuf, xs_ref.at[pl.ds(pl.multiple_of(ends_ref[e] - TM, TM), TM)], zsem)
                 for e in range(N_EXPERTS)]
        for cp in tails:
            cp.start()
        for cp in tails:
            cp.wait()

        def zero_tile(i, carry):
            cp = pltpu.make_async_copy(zbuf, xs_ref.at[pl.ds(pl.multiple_of(i * TM, TM), TM)], zsem)
            cp.start()
            cp.wait()
            return carry

        lax.fori_loop(ends_ref[N_EXPERTS - 1] // TM, xs_ref.shape[0] // TM, zero_tile, 0)

    def row_copy(r, pos_ref):
        return pltpu.make_async_copy(x_ref.at[pl.ds(r, 1)], xs_ref.at[pl.ds(pos_ref[0, 0, r], 1)], sem)

    def issue(r, carry):
        row_copy(r, pos1_ref).start(priority=0)
        row_copy(r, pos2_ref).start(priority=1)
        return carry

    lax.fori_loop(0, TQ, issue, 0, unroll=8)

    def drain(r, carry):
        row_copy(r, pos1_ref).wait()
        row_copy(r, pos2_ref).wait()
        return carry

    lax.fori_loop(0, TQ, drain, 0, unroll=8)


def _dispatch(x, pos, ends, p, tm):
    n, d = x.shape
    tq = min(n, DISPATCH_ROWS)
    nt = n // tq
    pos3 = pos.reshape(2 * nt, 1, tq)
    grid_spec = pltpu.PrefetchScalarGridSpec(
        num_scalar_prefetch=1,
        grid=(nt,),
        in_specs=[pl.BlockSpec((1, 1, tq), lambda t, ends: (t, 0, 0), memory_space=pltpu.SMEM),
                  pl.BlockSpec((1, 1, tq), lambda t, ends: (t + nt, 0, 0), memory_space=pltpu.SMEM),
                  pl.BlockSpec((tq, d), lambda t, ends: (t, 0))],
        out_specs=pl.BlockSpec(memory_space=pl.ANY),
        scratch_shapes=[pltpu.VMEM((tm, d), F32), pltpu.SemaphoreType.DMA(()), pltpu.SemaphoreType.DMA(())],
    )
    return pl.pallas_call(
        functools.partial(_dispatch_kernel, TQ=tq, TM=tm),
        grid_spec=grid_spec,
        out_shape=jax.ShapeDtypeStruct((p, d), F32),
        compiler_params=_params("arbitrary"),
        name="dispatch",
    )(ends, pos3, pos3, x)


def _moe_ffn_kernel(te_ref, nu_ref, x_ref, g_ref, *refs, tf, residual, n_pre):
    pre_a = refs[:n_pre]
    pre_w = refs[n_pre:n_pre + min(n_pre, 1)]
    w1_ref, w3_ref, w2_ref, o_ref, acc = refs[n_pre + len(pre_w):]
    t = pl.program_id(0)

    @pl.when(t < nu_ref[0])
    def _():
        x = x_ref[...]
        if pre_a:
            a = jnp.concatenate([a_ref[...] for a_ref in pre_a], axis=1)
            x = x + jnp.dot(a, pre_w[0][...], preferred_element_type=F32)
        h = _rms(x, g_ref[...]).astype(BF16)
        if residual:
            acc[...] = x
        for i, f0 in enumerate(range(0, D_FF, tf)):
            a1 = jnp.dot(h, w1_ref[0, :, f0:f0 + tf], preferred_element_type=F32)
            a3 = jnp.dot(h, w3_ref[0, :, f0:f0 + tf], preferred_element_type=F32)
            act = (a1 * jax.nn.sigmoid(a1) * a3).astype(BF16)
            part = jnp.dot(act, w2_ref[0, f0:f0 + tf, :], preferred_element_type=F32)
            if i == 0 and not residual:
                acc[...] = part
            elif f0 + tf < D_FF:
                acc[...] += part
            else:
                o_ref[...] = acc[...] + part

    @pl.when(t >= nu_ref[0])
    def _():
        o_ref[...] = jnp.zeros_like(o_ref)


def _ffn(res, a_list, w_out, g, w1, w3, w2):
    tm = min(res.shape[0], MOE_TILE)
    nt = res.shape[0] // tm
    return _moe_ffn(res, g, jnp.zeros((nt,), jnp.int32), jnp.full((1,), nt, jnp.int32), w1, w3, w2, tm,
                    residual=True, pre=(a_list, [w_out]))


def _moe_ffn(xs, g, tile_expert, n_used, w1, w3, w2, tm, residual=False, pre=((), ())):
    p, d = xs.shape
    dff = w1.shape[2]
    row_tile = lambda t, te, nu: (jnp.minimum(t, nu[0] - 1), 0)
    pre_a, pre_w = pre
    grid_spec = pltpu.PrefetchScalarGridSpec(
        num_scalar_prefetch=2,
        grid=(p // tm,),
        in_specs=[
            pl.BlockSpec((tm, d), row_tile),
            pl.BlockSpec((1, d), lambda t, te, nu: (0, 0)),
            *[pl.BlockSpec((tm, a.shape[1]), row_tile) for a in pre_a],
            *[pl.BlockSpec(w.shape, lambda t, te, nu: (0, 0)) for w in pre_w],
            pl.BlockSpec((1, d, dff), lambda t, te, nu: (te[t], 0, 0)),
            pl.BlockSpec((1, d, dff), lambda t, te, nu: (te[t], 0, 0)),
            pl.BlockSpec((1, dff, d), lambda t, te, nu: (te[t], 0, 0)),
        ],
        out_specs=pl.BlockSpec((tm, d), lambda t, te, nu: (t, 0)),
        scratch_shapes=[pltpu.VMEM((tm, d), F32)],
    )
    return pl.pallas_call(
        functools.partial(_moe_ffn_kernel, tf=256, residual=residual, n_pre=len(pre_a)),
        grid_spec=grid_spec,
        out_shape=jax.ShapeDtypeStruct((p, d), F32),
        compiler_params=_params("arbitrary"),
        name="moe_ffn",
    )(tile_expert, n_used, xs, g, *pre_a, *pre_w, w1, w3, w2)


def _combine_kernel(pos1_ref, pos2_ref, pos1n_ref, pos2n_ref, x_ref, gsel_ref, gf_ref, ys_ref, o_ref,
                    buf, sems, *, TQ):
    t = pl.program_id(0)
    slot = t % 2

    def row_copy(r, pos_ref, s, k):
        return pltpu.make_async_copy(ys_ref.at[pl.ds(pos_ref[0, 0, r], 1)], buf.at[s, k, pl.ds(r, 1)], sems.at[s])

    def issue(p1_ref, p2_ref, s):
        def body(r, carry):
            row_copy(r, p1_ref, s, 0).start(priority=0)
            row_copy(r, p2_ref, s, 1).start(priority=1)
            return carry

        lax.fori_loop(0, TQ, body, 0, unroll=8)

    @pl.when(t == 0)
    def _():
        issue(pos1_ref, pos2_ref, 0)

    @pl.when(t + 1 < pl.num_programs(0))
    def _():
        issue(pos1n_ref, pos2n_ref, 1 - slot)

    def drain(r, carry):
        row_copy(r, pos1_ref, slot, 0).wait()
        row_copy(r, pos2_ref, slot, 1).wait()
        return carry

    lax.fori_loop(0, TQ, drain, 0, unroll=8)
    gs = gsel_ref[...]
    y = x_ref[...] + gs[:, 0:1] * buf[slot, 0] + gs[:, 1:2] * buf[slot, 1]
    o_ref[...] = _rms(y, gf_ref[...])


def _combine(x, ys, pos, gsel, g_final):
    n, d = x.shape
    tq = min(n, DISPATCH_ROWS)
    nt = n // tq
    pos3 = pos.reshape(2 * nt, 1, tq)
    smem = lambda f: pl.BlockSpec((1, 1, tq), f, memory_space=pltpu.SMEM)
    nxt = lambda t: jnp.minimum(t + 1, nt - 1)
    return pl.pallas_call(
        functools.partial(_combine_kernel, TQ=tq),
        grid=(nt,),
        in_specs=[smem(lambda t: (t, 0, 0)), smem(lambda t: (t + nt, 0, 0)),
                  smem(lambda t: (nxt(t), 0, 0)), smem(lambda t: (nxt(t) + nt, 0, 0)),
                  pl.BlockSpec((tq, d), lambda t: (t, 0)),
                  pl.BlockSpec((tq, LANES), lambda t: (t, 0)),
                  pl.BlockSpec((1, d), lambda t: (0, 0)),
                  pl.BlockSpec(memory_space=pl.ANY)],
        out_specs=pl.BlockSpec((tq, d), lambda t: (t, 0)),
        out_shape=jax.ShapeDtypeStruct((n, d), F32),
        scratch_shapes=[pltpu.VMEM((2, 2, tq, d), F32), pltpu.SemaphoreType.DMA((2,))],
        compiler_params=_params("arbitrary"),
        name="combine",
    )(pos3, pos3, pos3, pos3, x, gsel, g_final, ys)


def _moe_plan(ids, tm):
    n = ids.shape[0]
    e_flat = ids.T.reshape(-1)
    experts = jnp.arange(N_EXPERTS, dtype=jnp.int32)
    onehot = (e_flat[:, None] == experts[None, :]).astype(jnp.int32)
    csum = jnp.cumsum(onehot, axis=0)
    counts = csum[-1]
    padded = jnp.maximum((counts + tm - 1) // tm, 1) * tm
    ends = jnp.cumsum(padded).astype(jnp.int32)
    pos = jnp.sum((csum - 1 + (ends - padded)[None, :]) * onehot, axis=1).astype(jnp.int32)
    p = 2 * n + N_EXPERTS * tm
    n_used = ends[-1] // tm
    tile_start = jnp.minimum(jnp.arange(p // tm, dtype=jnp.int32) * tm, ends[-1] - 1)
    tile_expert = jnp.sum((tile_start[:, None] >= ends[None, :]).astype(jnp.int32), axis=1)
    return pos, ends, tile_expert, n_used.reshape(1), p


def _prep_weights(p):
    w_in = p["ab_w_in"][0]
    wa, wb = w_in[:, :A_COLS], w_in[:, A_COLS:]
    pad = jnp.zeros((D_MODEL, LANES - GLA_GATE_RANK), F32)
    w_in_p = jnp.concatenate(
        [wa[:, :2 * GLA_QK + A_WIDTH], wa[:, 2 * GLA_QK + A_WIDTH + GLA_GATE_RANK:], wb[:, :3 * B_WIDTH],
         wa[:, 2 * GLA_QK + A_WIDTH:2 * GLA_QK + A_WIDTH + GLA_GATE_RANK], pad, wb[:, 3 * B_WIDTH:]], axis=1)
    row = lambda v: v.reshape(1, -1).astype(F32)
    mu = p["rwkv_mu"][0]
    z64 = jnp.zeros((64, B_WIDTH), F32)
    w = dict(
        norm_mix=p["norm_mix"], norm_ffn=p["norm_ffn"], norm_final=row(p["norm_final"]),
        w_in=w_in_p.astype(BF16),
        w_out=p["ab_w_out"][0].astype(BF16),
        gla_w2=jnp.concatenate([p["gla_w_gate2"][0], jnp.zeros((LANES - GLA_GATE_RANK, GLA_QK), F32)], axis=0),
        gla_bg=row(p["gla_b_gate"][0]), gla_norm=row(p["gla_norm"][0]),
        rwkv=(row(mu[:B_WIDTH]), row(mu[B_WIDTH:2 * B_WIDTH]), row(mu[2 * B_WIDTH:3 * B_WIDTH]),
              row(mu[3 * B_WIDTH:3 * B_WIDTH + LANES]), row(mu[3 * B_WIDTH + LANES:]),
              row(p["rwkv_w0"][0]), jnp.concatenate([p["rwkv_w2"][0], z64], axis=0),
              row(p["rwkv_a0"][0]), jnp.concatenate([z64, p["rwkv_a2"][0]], axis=0),
              p["rwkv_g2"][0], row(p["rwkv_k_k"][0]), row(p["rwkv_k_a"][0]), row(p["rwkv_r_k"][0]),
              row(p["rwkv_gn_g"][0]), row(p["rwkv_gn_b"][0])),
        ret_w_in=p["ret_w_in"][0].astype(BF16), ret_w_out=p["ret_w_out"][0].astype(BF16),
        ret_gn_g=row(p["ret_gn_g"][0]), ret_gn_b=row(p["ret_gn_b"][0]),
        ffn_w1=p["ffn_w1"].astype(BF16), ffn_w3=p["ffn_w3"].astype(BF16), ffn_w2=p["ffn_w2"].astype(BF16),
        router=_router_operand(p["moe_router"][0]),
        moe_w1=p["moe_w1"][0].astype(BF16), moe_w3=p["moe_w3"][0].astype(BF16), moe_w2=p["moe_w2"][0].astype(BF16),
    )
    return w


def _trunk(x, pos0, st_gla, st_rwkv, st_shift, st_ret, w):
    B, T, D = x.shape
    n = B * T
    x0 = x.reshape(n, D)
    c_ab = min(CHUNK, T)
    c_ret = min(256, T)

    u = _norm_matmul(x0, w["norm_mix"][0:1], w["w_in"], tm=1024, tn=UW, out_dtype=BF16)
    oa, s_gla = _gla(u, st_gla, w["gla_w2"], w["gla_bg"], w["gla_norm"], B, T, c_ab)
    h0 = st_rwkv.transpose(0, 3, 1, 2).reshape(B, RWKV_N, B_WIDTH)
    yb, h_fin = _rwkv(u, st_shift, h0, w["rwkv"], B, T, c_ab)
    s_rwkv = h_fin.reshape(B, RWKV_N, RWKV_HEADS, RWKV_N).transpose(0, 2, 3, 1)
    u_last = u.reshape(B, T, UW)[:, T - 1:, :]
    s_shift = jnp.concatenate([u_last[..., 3 * A_WIDTH:6 * A_WIDTH], u_last[..., UW - 2 * LANES:]], axis=-1).astype(F32)
    x2 = _ffn(x0, [oa, yb], w["w_out"], w["norm_ffn"][0:1], w["ffn_w1"], w["ffn_w3"], w["ffn_w2"])

    ur = _norm_matmul(x2, w["norm_mix"][1:2], w["ret_w_in"], tm=1024, tn=3072, out_dtype=BF16)
    half = RET_DK // 2
    inv = 1.0 / (ROPE_BASE ** (jnp.arange(0, RET_DK, 2, dtype=F32) / RET_DK))
    ang = (pos0 + jnp.arange(T)).astype(F32)[:, None] * inv[None, :]
    o, s_ret = _ret(ur, jnp.cos(ang), jnp.sin(ang), st_ret, w["ret_gn_g"], w["ret_gn_b"], B, T, c_ret)
    x3, ids, gsel = _router(x2, o, w["ret_w_out"], w["norm_ffn"][1:2], w["router"])
    tm = min(MOE_TILE, n // 4)
    pos, ends, tile_expert, n_used, p = _moe_plan(ids[:, :2], tm)
    xs = _dispatch(x3, pos, ends, p, tm)
    ys = _moe_ffn(xs, w["norm_ffn"][1:2], tile_expert, n_used, w["moe_w1"], w["moe_w3"], w["moe_w2"], tm)
    y = _combine(x3, ys, pos, gsel, w["norm_final"])
    return y.reshape(B, T, D), s_gla[None], s_rwkv[None], s_shift[None], s_ret[None]


def kernel(x_prompt, x_sample, state_gla, state_rwkv, state_shift, state_ret, norm_mix, norm_ffn, norm_final, ab_w_in, ab_w_out, gla_w_gate2, gla_b_gate, gla_norm, rwkv_mu, rwkv_w0, rwkv_w2, rwkv_a0, rwkv_a2, rwkv_g2, rwkv_k_k, rwkv_k_a, rwkv_r_k, rwkv_gn_g, rwkv_gn_b, ret_w_in, ret_gn_g, ret_gn_b, ret_w_out, ffn_w1, ffn_w3, ffn_w2, moe_router, moe_w1, moe_w3, moe_w2):
    p = dict(norm_mix=norm_mix, norm_ffn=norm_ffn, norm_final=norm_final, ab_w_in=ab_w_in, ab_w_out=ab_w_out,
             gla_w_gate2=gla_w_gate2, gla_b_gate=gla_b_gate, gla_norm=gla_norm, rwkv_mu=rwkv_mu, rwkv_w0=rwkv_w0,
             rwkv_w2=rwkv_w2, rwkv_a0=rwkv_a0, rwkv_a2=rwkv_a2, rwkv_g2=rwkv_g2, rwkv_k_k=rwkv_k_k,
             rwkv_k_a=rwkv_k_a, rwkv_r_k=rwkv_r_k, rwkv_gn_g=rwkv_gn_g, rwkv_gn_b=rwkv_gn_b, ret_w_in=ret_w_in,
             ret_gn_g=ret_gn_g, ret_gn_b=ret_gn_b, ret_w_out=ret_w_out, ffn_w1=ffn_w1, ffn_w3=ffn_w3, ffn_w2=ffn_w2,
             moe_router=moe_router, moe_w1=moe_w1, moe_w3=moe_w3, moe_w2=moe_w2)
    w = _prep_weights(p)
    bp, tp = x_prompt.shape[0], x_prompt.shape[1]
    dt = x_prompt.dtype
    z_gla = jnp.zeros((bp, GLA_HEADS, GLA_DK, GLA_DV), dt)
    z_rwkv = jnp.zeros((bp, RWKV_HEADS, RWKV_N, RWKV_N), dt)
    z_shift = jnp.zeros((bp, 1, B_COLS), dt)
    z_ret = jnp.zeros((bp, RET_HEADS, RET_DK, RET_DV), dt)
    past = 2048
    yp = _trunk(x_prompt, 0, z_gla, z_rwkv, z_shift, z_ret, w)
    ys = _trunk(x_sample, past, state_gla[0], state_rwkv[0], state_shift[0], state_ret[0], w)
    return (yp[0], ys[0], yp[1], yp[2], yp[3], yp[4], ys[1], ys[2], ys[3], ys[4])
```
